```python
import math
import jax, jax.numpy as jnp
from jax import lax
import numpy as np

D_MODEL = 1024
BATCH = 8
SEQ = 4096
DEPTH = 2

CHUNK = 64
Q_BLOCK = 128
N_EVEN = (DEPTH + 1) // 2
N_ODD = DEPTH // 2
DN_ALPHA = (2.0 * DEPTH) ** 0.25
DN_BETA = (8.0 * DEPTH) ** -0.25
LN_EPS = 1e-5
RMS_EPS = 1e-6

GLA_HEADS = 4
GLA_DK = 64
GLA_DV = 128
GLA_GATE_RANK = 16
GLA_TAU = 16.0
GLA_QK = GLA_HEADS * GLA_DK
GLA_V = GLA_HEADS * GLA_DV
GLA_SPLITS = (GLA_QK, GLA_QK, GLA_V, GLA_V, GLA_GATE_RANK)
GLA_IN = sum(GLA_SPLITS)
GLA_OFFSETS = tuple(int(o) for o in np.cumsum(GLA_SPLITS)[:-1])

RW_HEADS = 8
RW_N = 64
RW_W = RW_HEADS * RW_N
RW_DECAY_RANK = 64
RW_A_RANK = 64
RW_GATE_RANK = 160
RW_GN_EPS = 64e-5
RW_SPLITS = (RW_W, RW_W, RW_W, RW_DECAY_RANK, RW_A_RANK, RW_GATE_RANK)
RW_IN = sum(RW_SPLITS)
RW_OFFSETS = tuple(int(o) for o in np.cumsum(RW_SPLITS)[:-1])

EVEN_IN = GLA_IN + RW_IN
EVEN_MIX = GLA_V + RW_W

MLA_HEADS = 16
MLA_NOPE = 64
MLA_ROPE = 32
MLA_V = 64
MLA_Q_RANK = 768
MLA_KV_RANK = 256
ODD_IN = MLA_Q_RANK + MLA_KV_RANK + MLA_ROPE
MLA_MIX = MLA_HEADS * MLA_V
ROPE_THETA = 10000.0

FFN_HIDDEN = math.ceil(8 * D_MODEL / 3 / 256) * 256

kernel_name = "hybrid_gla_rwkv7_mla_deepnorm_trunk"


def layer_norm(x, g, b):
    xf = x.astype(jnp.float32)
    mu = xf.mean(-1, keepdims=True)
    var = jnp.square(xf - mu).mean(-1, keepdims=True)
    return ((xf - mu) * lax.rsqrt(var + LN_EPS) * g + b).astype(x.dtype)


def rms_norm(x, g):
    xf = x.astype(jnp.float32)
    return (xf * lax.rsqrt(jnp.mean(xf * xf, -1, keepdims=True) + RMS_EPS) * g).astype(x.dtype)


def token_shift(t):
    return jnp.pad(t[:, :-1], ((0, 0), (1, 0), (0, 0)))


def gla_chunked(q, k, v, log_a):
    B, S, H, DK = q.shape
    DV = v.shape[-1]
    n = S // CHUNK

    def to_chunks(t):
        return t.astype(jnp.float32).reshape(B, n, CHUNK, H, t.shape[-1]).transpose(1, 0, 3, 2, 4)

    qc, kc, vc, gc = to_chunks(q), to_chunks(k), to_chunks(v), to_chunks(log_a)
    causal = jnp.tril(jnp.ones((CHUNK, CHUNK), bool))[None, None, :, :, None]

    def step(state, inp):
        qi, ki, vi, gi = inp
        b = jnp.cumsum(gi, axis=2)
        o_inter = jnp.einsum('bhck,bhkv->bhcv', qi * jnp.exp(b), state)
        rel = jnp.where(causal, b[:, :, :, None, :] - b[:, :, None, :, :], -jnp.inf)
        attn = jnp.einsum('bhik,bhjk,bhijk->bhij', qi, ki, jnp.exp(rel))
        o_intra = jnp.einsum('bhij,bhjv->bhiv', attn, vi)
        b_last = b[:, :, -1:, :]
        new_state = (jnp.exp(b_last[:, :, 0, :])[..., None] * state
                     + jnp.einsum('bhck,bhcv->bhkv', ki * jnp.exp(b_last - b), vi))
        return new_state, o_inter + o_intra

    s0 = jnp.zeros((B, H, DK, DV), jnp.float32)
    _, o = lax.scan(step, s0, (qc, kc, vc, gc))
    return o.transpose(1, 0, 3, 2, 4).reshape(B, S, H, DV)


def rwkv7_scan(r, w, k, v, a_vec, b_vec):
    B, S, H, N = r.shape

    def tm(t):
        return t.astype(jnp.float32).transpose(1, 0, 2, 3)

    def step(state, inp):
        rt, wt, kt, vt, at, bt = inp
        sa = jnp.einsum('bhvk,bhk->bhv', state, at)
        state = state * wt[:, :, None, :] + sa[..., None] * bt[:, :, None, :] + vt[..., None] * kt[:, :, None, :]
        return state, jnp.einsum('bhvk,bhk->bhv', state, rt)

    s0 = jnp.zeros((B, H, N, N), jnp.float32)
    _, y = lax.scan(step, s0, (tm(r), tm(w), tm(k), tm(v), tm(a_vec), tm(b_vec)))
    return y.transpose(1, 0, 2, 3)


def even_mixer(x, w_in, gla_gate_w2, gla_gate_b, gla_norm_g, rw_mu, rw_w0, rw_w2, rw_a0, rw_a2,
               rw_g2, rw_k_k, rw_k_a, rw_r_k, rw_ln_g, rw_ln_b, w_out):
    B, S, _ = x.shape
    p = x @ w_in
    p_gla, p_rw = p[..., :GLA_IN], p[..., GLA_IN:]

    def heads(t, d):
        return t.reshape(B, S, -1, d)

    gq, gk, gv, gg, glr = jnp.split(p_gla, GLA_OFFSETS, axis=-1)
    log_a = jax.nn.log_sigmoid((glr @ gla_gate_w2 + gla_gate_b).astype(jnp.float32)) / GLA_TAU
    o_a = gla_chunked(heads(gq, GLA_DK) * GLA_DK ** -0.5, heads(gk, GLA_DK),
                      heads(gv, GLA_DV), heads(log_a, GLA_DK))
    o_a = (rms_norm(o_a, gla_norm_g) * jax.nn.silu(heads(gg, GLA_DV).astype(jnp.float32)))
    o_a = o_a.reshape(B, S, GLA_V).astype(x.dtype)

    p_rw = p_rw + rw_mu * (token_shift(p_rw) - p_rw)
    r, k, v, wl, al, gl = jnp.split(p_rw, RW_OFFSETS, axis=-1)
    w = -jax.nn.softplus(-(rw_w0 + jnp.tanh(wl) @ rw_w2)) - 0.5
    decay = jnp.exp(-jnp.exp(w.astype(jnp.float32)))
    a = jax.nn.sigmoid(rw_a0 + al @ rw_a2)
    g = jax.nn.sigmoid(gl) @ rw_g2
    kk = heads(k * rw_k_k, RW_N).astype(jnp.float32)
    kk = kk * lax.rsqrt(jnp.maximum(jnp.sum(kk * kk, -1, keepdims=True), 1e-24))
    k = k * (1 + (a - 1) * rw_k_a)
    r_h, k_h, v_h, a_h = heads(r, RW_N), heads(k, RW_N), heads(v, RW_N), heads(a, RW_N)
    y = rwkv7_scan(r_h, heads(decay, RW_N), k_h, v_h, -kk, kk * a_h)
    mu = y.mean(-1, keepdims=True)
    var = jnp.square(y - mu).mean(-1, keepdims=True)
    y = ((y - mu) * lax.rsqrt(var + RW_GN_EPS)).reshape(B, S, RW_W) * rw_ln_g + rw_ln_b
    bonus = jnp.sum(r_h * k_h * rw_r_k, -1, keepdims=True) * v_h
    o_b = ((heads(y, RW_N) + bonus).reshape(B, S, RW_W) * g).astype(x.dtype)

    return jnp.concatenate([o_a, o_b], axis=-1) @ w_out


def apply_rope(t, cos, sin):
    t2 = t.astype(jnp.float32).reshape(t.shape[:-1] + (-1, 2))
    x1, x2 = t2[..., 0], t2[..., 1]
    out = jnp.stack([x1 * cos - x2 * sin, x1 * sin + x2 * cos], axis=-1)
    return out.reshape(t.shape).astype(t.dtype)


def block_causal_mla_attention(q_nope, q_pe, k_nope, k_pe, v):
    B, S, H, _ = q_nope.shape
    scale = (MLA_NOPE + MLA_ROPE) ** -0.5
    key_chunk = jnp.arange(S) // CHUNK

    def block(i):
        start = i * Q_BLOCK
        qn = lax.dynamic_slice_in_dim(q_nope, start, Q_BLOCK, axis=1)
        qp = lax.dynamic_slice_in_dim(q_pe, start, Q_BLOCK, axis=1)
        s = (jnp.einsum('bqhd,bkhd->bhqk', qn, k_nope)
             + jnp.einsum('bqhr,bkr->bhqk', qp, k_pe)).astype(jnp.float32) * scale
        q_chunk = (start + jnp.arange(Q_BLOCK)) // CHUNK
        s = jnp.where(key_chunk[None, :] <= q_chunk[:, None], s, -jnp.inf)
        pr = jax.nn.softmax(s, axis=-1).astype(v.dtype)
        return jnp.einsum('bhqk,bkhd->bqhd', pr, v)

    o = lax.map(block, jnp.arange(S // Q_BLOCK))
    return o.transpose(1, 0, 2, 3, 4).reshape(B, S, H, MLA_V)


def odd_mixer(x, positions, w_in, q_norm_g, w_q_b, kv_norm_g, w_kv_b, w_out):
    B, S, _ = x.shape
    p = x @ w_in
    cq = p[..., :MLA_Q_RANK]
    ckv = p[..., MLA_Q_RANK:MLA_Q_RANK + MLA_KV_RANK]
    k_pe = p[..., MLA_Q_RANK + MLA_KV_RANK:]
    q = (rms_norm(cq, q_norm_g) @ w_q_b).reshape(B, S, MLA_HEADS, MLA_NOPE + MLA_ROPE)
    kv = (rms_norm(ckv, kv_norm_g) @ w_kv_b).reshape(B, S, MLA_HEADS, MLA_NOPE + MLA_V)
    q_nope, q_pe = q[..., :MLA_NOPE], q[..., MLA_NOPE:]
    k_nope, v = kv[..., :MLA_NOPE], kv[..., MLA_NOPE:]
    inv_freq = ROPE_THETA ** (-jnp.arange(0, MLA_ROPE, 2, dtype=jnp.float32) / MLA_ROPE)
    ang = positions.astype(jnp.float32)[..., None] * inv_freq
    cos, sin = jnp.cos(ang), jnp.sin(ang)
    q_pe = apply_rope(q_pe, cos[:, :, None, :], sin[:, :, None, :])
    k_pe = apply_rope(k_pe, cos, sin)
    o = block_causal_mla_attention(q_nope, q_pe, k_nope, k_pe, v)
    return o.reshape(B, S, MLA_MIX) @ w_out


def swiglu_ffn(x, w_gate_up, w_down):
    gu = x @ w_gate_up
    gate, up = gu[..., :FFN_HIDDEN], gu[..., FFN_HIDDEN:]
    return (jax.nn.silu(gate) * up) @ w_down


def setup_inputs(seed: int = 0) -> dict:
    key = jax.random.key(seed)
    ks = iter(jax.random.split(key, 64))

    def dense(lead, fan_in, fan_out, scale=1.0):
        return jax.random.normal(next(ks), lead + (fan_in, fan_out), jnp.float32) * (scale * fan_in ** -0.5)

    def vec(shape, base, noise):
        return base + noise * jax.random.normal(next(ks), shape, jnp.float32)

    E, O = (N_EVEN,), (N_ODD,)
    x = jax.random.normal(next(ks), (BATCH, SEQ, D_MODEL), jnp.float32)
    positions = (jax.random.randint(next(ks), (BATCH, 1), 0, 4096, jnp.int32)
                 + jnp.arange(SEQ, dtype=jnp.int32)[None, :])

    even_scales = (1.0, 1.0, DN_BETA, 1.0, 1.0) + (1.0, 1.0, DN_BETA, 1.0, 1.0, 1.0)
    even_w_in = jnp.concatenate([dense(E, D_MODEL, c, s) for c, s in zip(GLA_SPLITS + RW_SPLITS, even_scales)], -1)

    mla_w_kv_b = jnp.concatenate([
        dense(O, MLA_KV_RANK, MLA_NOPE, 1.0).reshape(O + (MLA_KV_RANK, 1, MLA_NOPE)) * jnp.ones((1, MLA_HEADS, 1)),
        dense(O, MLA_KV_RANK, MLA_HEADS * MLA_V, DN_BETA).reshape(O + (MLA_KV_RANK, MLA_HEADS, MLA_V)),
    ], -1)
    mla_w_kv_b = mla_w_kv_b + 0.3 * dense(O, MLA_KV_RANK, MLA_HEADS * (MLA_NOPE + MLA_V)).reshape(mla_w_kv_b.shape) * jnp.concatenate([jnp.ones((MLA_NOPE,)), jnp.zeros((MLA_V,))])

    return {
        "x": x,
        "positions": positions,
        "even_w_in": even_w_in,
        "gla_gate_w2": dense(E, GLA_GATE_RANK, GLA_QK),
        "gla_gate_b": vec(E + (GLA_QK,), 0.0, 0.1),
        "gla_norm_g": vec(E + (GLA_DV,), 1.0, 0.02),
        "rwkv_mu": jax.random.uniform(next(ks), E + (RW_IN,), jnp.float32),
        "rwkv_w0": vec(E + (RW_W,), 0.0, 0.5),
        "rwkv_w2": dense(E, RW_DECAY_RANK, RW_W),
        "rwkv_a0": vec(E + (RW_W,), 0.0, 0.1),
        "rwkv_a2": dense(E, RW_A_RANK, RW_W),
        "rwkv_g2": dense(E, RW_GATE_RANK, RW_W),
        "rwkv_k_k": vec(E + (RW_W,), 0.85, 0.02),
        "rwkv_k_a": vec(E + (RW_W,), 1.0, 0.02),
        "rwkv_r_k": vec(E + (RW_HEADS, RW_N), 0.0, 0.1),
        "rwkv_ln_g": vec(E + (RW_W,), 1.0, 0.02),
        "rwkv_ln_b": vec(E + (RW_W,), 0.0, 0.02),
        "even_w_out": dense(E, EVEN_MIX, D_MODEL, DN_BETA),
        "mla_w_in": dense(O, D_MODEL, ODD_IN),
        "mla_q_norm_g": vec(O + (MLA_Q_RANK,), 1.0, 0.02),
        "mla_w_q_b": dense(O, MLA_Q_RANK, MLA_HEADS * (MLA_NOPE + MLA_ROPE)),
        "mla_kv_norm_g": vec(O + (MLA_KV_RANK,), 1.0, 0.02),
        "mla_w_kv_b": mla_w_kv_b.reshape(O + (MLA_KV_RANK, MLA_HEADS * (MLA_NOPE + MLA_V))),
        "mla_w_out": dense(O, MLA_MIX, D_MODEL, DN_BETA),
        "ffn_w_gate_up": dense((DEPTH,), D_MODEL, 2 * FFN_HIDDEN, DN_BETA),
        "ffn_w_down": dense((DEPTH,), FFN_HIDDEN, D_MODEL, DN_BETA),
        "ln_g": vec((DEPTH, 2, D_MODEL), 1.0, 0.02),
        "ln_b": vec((DEPTH, 2, D_MODEL), 0.0, 0.02),
    }


def reference(x, positions, even_w_in, gla_gate_w2, gla_gate_b, gla_norm_g, rwkv_mu, rwkv_w0, rwkv_w2,
              rwkv_a0, rwkv_a2, rwkv_g2, rwkv_k_k, rwkv_k_a, rwkv_r_k, rwkv_ln_g, rwkv_ln_b, even_w_out,
              mla_w_in, mla_q_norm_g, mla_w_q_b, mla_kv_norm_g, mla_w_kv_b, mla_w_out,
              ffn_w_gate_up, ffn_w_down, ln_g, ln_b):
    for i in range(DEPTH):
        j = i // 2
        if i % 2 == 0:
            h = even_mixer(x, even_w_in[j], gla_gate_w2[j], gla_gate_b[j], gla_norm_g[j], rwkv_mu[j],
                           rwkv_w0[j], rwkv_w2[j], rwkv_a0[j], rwkv_a2[j], rwkv_g2[j], rwkv_k_k[j],
                           rwkv_k_a[j], rwkv_r_k[j], rwkv_ln_g[j], rwkv_ln_b[j], even_w_out[j])
        else:
            h = odd_mixer(x, positions, mla_w_in[j], mla_q_norm_g[j], mla_w_q_b[j], mla_kv_norm_g[j],
                          mla_w_kv_b[j], mla_w_out[j])
        x = layer_norm(DN_ALPHA * x + h, ln_g[i, 0], ln_b[i, 0])
        x = layer_norm(DN_ALPHA * x + swiglu_ffn(x, ffn_w_gate_up[i], ffn_w_down[i]), ln_g[i, 1], ln_b[i, 1])
    return x
```

```python
import functools
import math

import jax
import jax.numpy as jnp
import numpy as np
from jax import lax
from jax.experimental import pallas as pl
from jax.experimental.pallas import tpu as pltpu

F32 = jnp.float32
BF16 = jnp.bfloat16

DEPTH = 2
CHUNK = 64
DN_ALPHA = (2.0 * DEPTH) ** 0.25
LN_EPS = 1e-5
RMS_EPS = 1e-6

GLA_HEADS = 4
GLA_DK = 64
GLA_DV = 128
GLA_GATE_RANK = 16
GLA_TAU = 16.0
GLA_QK = GLA_HEADS * GLA_DK
GLA_V = GLA_HEADS * GLA_DV
GLA_SUB = 16
GLA_EXP_CLAMP = 60.0

RW_HEADS = 8
RW_N = 64
RW_W = RW_HEADS * RW_N
RW_DECAY_RANK = 64
RW_A_RANK = 64
RW_GATE_RANK = 160
RW_GN_EPS = 64e-5

MLA_HEADS = 16
MLA_NOPE = 64
MLA_ROPE = 32
MLA_V = 64
MLA_Q_RANK = 768
MLA_KV_RANK = 256
ROPE_THETA = 10000.0

LANE = 128
VMEM_LIMIT = 48 * 1024 * 1024

RW_COLS = 3 * RW_W + LANE + LANE + 2 * LANE
GLA_COLS = 2 * GLA_QK + 2 * GLA_V + LANE


def _cparams(sem):
    return pltpu.CompilerParams(dimension_semantics=sem, vmem_limit_bytes=VMEM_LIMIT)


def _bdot(a, b):
    return jnp.dot(a.astype(BF16), b.astype(BF16), preferred_element_type=F32)


def _dot_nt(a, b):
    return lax.dot_general(a.astype(BF16), b.astype(BF16), (((1,), (1,)), ((), ())),
                           preferred_element_type=F32)


def _dot_tn(a, b):
    return lax.dot_general(a.astype(BF16), b.astype(BF16), (((0,), (0,)), ((), ())),
                           preferred_element_type=F32)


def _dot_split(m01, x):
    hi = x.astype(BF16)
    lo = (x - hi.astype(F32)).astype(BF16)
    m = m01.astype(BF16)
    return (jnp.dot(m, hi, preferred_element_type=F32)
            + jnp.dot(m, lo, preferred_element_type=F32))


def _softplus(x):
    return jnp.maximum(x, 0.0) + jnp.log(1.0 + jnp.exp(-jnp.abs(x)))


def _layer_norm(y, g, b):
    mu = jnp.mean(y, axis=-1, keepdims=True)
    d = y - mu
    var = jnp.mean(d * d, axis=-1, keepdims=True)
    return d * lax.rsqrt(var + LN_EPS) * g + b


def _inproj_kernel(x_ref, w_ref, *o_refs, widths, tn):
    xb = x_ref[...].astype(BF16)
    off = 0
    for o_ref, width in zip(o_refs, widths):
        for j in range(0, width, tn):
            w = min(tn, width - j)
            o_ref[:, j:j + w] = jnp.dot(xb, w_ref[:, off + j:off + j + w],
                                        preferred_element_type=F32).astype(o_ref.dtype)
        off += width


def _inproj(x2d, w_bf16, widths, out_dtype, tm):
    m, k = x2d.shape
    tm = min(tm, m)
    n = sum(widths)
    return pl.pallas_call(
        functools.partial(_inproj_kernel, widths=tuple(widths), tn=512),
        grid=(m // tm,),
        in_specs=[pl.BlockSpec((tm, k), lambda i: (i, 0)),
                  pl.BlockSpec((k, n), lambda i: (0, 0))],
        out_specs=[pl.BlockSpec((tm, wd), lambda i: (i, 0)) for wd in widths],
        out_shape=[jax.ShapeDtypeStruct((m, wd), out_dtype) for wd in widths],
        compiler_params=_cparams(("parallel",)),
    )(x2d, w_bf16)


def _outproj_ln_kernel(*refs, n_in):
    a_refs = refs[:n_in]
    w_refs = refs[n_in:2 * n_in]
    x_ref, g_ref, b_ref, o_ref = refs[2 * n_in:]
    y = DN_ALPHA * x_ref[...]
    for a_ref, w_ref in zip(a_refs, w_refs):
        y = y + jnp.dot(a_ref[...], w_ref[...], preferred_element_type=F32)
    o_ref[...] = _layer_norm(y, g_ref[...], b_ref[...])


def _outproj_ln(acts, weights, x2d, g, b, tm):
    m, d = x2d.shape
    tm = min(tm, m)
    n_in = len(acts)
    in_specs = ([pl.BlockSpec((tm, a.shape[1]), lambda i: (i, 0)) for a in acts]
                + [pl.BlockSpec(w.shape, lambda i: (0, 0)) for w in weights]
                + [pl.BlockSpec((tm, d), lambda i: (i, 0)),
                   pl.BlockSpec((1, d), lambda i: (0, 0)),
                   pl.BlockSpec((1, d), lambda i: (0, 0))])
    return pl.pallas_call(
        functools.partial(_outproj_ln_kernel, n_in=n_in),
        grid=(m // tm,),
        in_specs=in_specs,
        out_specs=pl.BlockSpec((tm, d), lambda i: (i, 0)),
        out_shape=jax.ShapeDtypeStruct((m, d), F32),
        compiler_params=_cparams(("parallel",)),
    )(*acts, *weights, x2d, g.reshape(1, d), b.reshape(1, d))


def _ffn_kernel(x_ref, wg_ref, wu_ref, wd_ref, g_ref, b_ref, o_ref, xb_ref, acc_ref):
    j = pl.program_id(1)

    @pl.when(j == 0)
    def _():
        xb_ref[...] = x_ref[...].astype(BF16)
        acc_ref[...] = jnp.zeros_like(acc_ref)

    xb = xb_ref[...]
    gate = jnp.dot(xb, wg_ref[...], preferred_element_type=F32)
    up = jnp.dot(xb, wu_ref[...], preferred_element_type=F32)
    h = (gate * jax.nn.sigmoid(gate) * up).astype(BF16)
    acc_ref[...] += jnp.dot(h, wd_ref[...], preferred_element_type=F32)

    @pl.when(j == pl.num_programs(1) - 1)
    def _():
        y = DN_ALPHA * x_ref[...] + acc_ref[...]
        o_ref[...] = _layer_norm(y, g_ref[...], b_ref[...])


def _ffn_ln(x2d, w_gate_up, w_down, g, b, tm, th):
    m, d = x2d.shape
    hidden = w_down.shape[0]
    tm = min(tm, m)
    nh = hidden // th
    return pl.pallas_call(
        _ffn_kernel,
        grid=(m // tm, nh),
        in_specs=[pl.BlockSpec((tm, d), lambda i, j: (i, 0)),
                  pl.BlockSpec((d, th), lambda i, j: (0, j)),
                  pl.BlockSpec((d, th), lambda i, j: (0, j + nh)),
                  pl.BlockSpec((th, d), lambda i, j: (j, 0)),
                  pl.BlockSpec((1, d), lambda i, j: (0, 0)),
                  pl.BlockSpec((1, d), lambda i, j: (0, 0))],
        out_specs=pl.BlockSpec((tm, d), lambda i, j: (i, 0)),
        out_shape=jax.ShapeDtypeStruct((m, d), F32),
        scratch_shapes=[pltpu.VMEM((tm, d), BF16), pltpu.VMEM((tm, d), F32)],
        compiler_params=_cparams(("parallel", "arbitrary")),
    )(x2d, w_gate_up, w_gate_up, w_down, g.reshape(1, d), b.reshape(1, d))


def _gla_kernel(p_ref, w2_ref, gb_ref, ng_ref, o_ref, st_ref, *, n_chunks):
    @pl.when(pl.program_id(1) == 0)
    def _():
        st_ref[...] = jnp.zeros_like(st_ref)

    c_row = lax.broadcasted_iota(jnp.int32, (CHUNK, CHUNK), 0)
    c_col = lax.broadcasted_iota(jnp.int32, (CHUNK, CHUNK), 1)
    tri = (c_col <= c_row)
    tri_loc = tri & (c_col >= (c_row // GLA_SUB) * GLA_SUB)
    cum_mat = jnp.concatenate([tri, tri_loc], axis=0).astype(BF16)
    lane = lax.broadcasted_iota(jnp.int32, (1, GLA_QK), 1)
    head_masks = [(lane // GLA_DK == h).astype(F32) for h in range(GLA_HEADS)]
    n_sub = CHUNK // GLA_SUB

    def chunk(c, carry):
        r0 = pl.multiple_of(c * CHUNK, CHUNK)
        q = p_ref[pl.ds(r0, CHUNK), 0:GLA_QK] * (GLA_DK ** -0.5)
        k = p_ref[pl.ds(r0, CHUNK), GLA_QK:2 * GLA_QK]
        v = p_ref[pl.ds(r0, CHUNK), 2 * GLA_QK:2 * GLA_QK + GLA_V]
        gg = p_ref[pl.ds(r0, CHUNK), 2 * GLA_QK + GLA_V:2 * GLA_QK + 2 * GLA_V]
        lr = p_ref[pl.ds(r0, CHUNK), 2 * GLA_QK + 2 * GLA_V:GLA_COLS]
        z = _bdot(lr, w2_ref[...]) + gb_ref[...]
        log_a = -_softplus(-z) / GLA_TAU
        cums = _dot_split(cum_mat, log_a)
        b = cums[:CHUNK]
        loc = cums[CHUNK:]
        b_last = b[CHUNK - 1:CHUNK, :]
        vb = v.astype(BF16)

        q_loc = q * jnp.exp(loc)
        a_blocks = []
        for i in range(n_sub):
            if i == 0:
                ref_i = jnp.zeros((1, GLA_QK), F32)
            else:
                ref_i = b[i * GLA_SUB - 1:i * GLA_SUB, :]
            k_i = k * jnp.exp(jnp.minimum(ref_i - b, GLA_EXP_CLAMP))
            q_i = q_loc[i * GLA_SUB:(i + 1) * GLA_SUB, :]
            lhs = jnp.concatenate([q_i * m for m in head_masks], axis=0)
            a_blocks.append(_dot_nt(lhs, k_i))

        q_b = q * jnp.exp(b)
        lhs_inter = jnp.concatenate([q_b * m for m in head_masks], axis=0)
        st = st_ref[...]
        o_inter = _dot_nt(lhs_inter, st)

        for h in range(GLA_HEADS):
            a_h = jnp.concatenate(
                [blk[h * GLA_SUB:(h + 1) * GLA_SUB, :] for blk in a_blocks], axis=0)
            a_h = jnp.where(tri, a_h, 0.0)
            v_h = vb[:, h * GLA_DV:(h + 1) * GLA_DV]
            o_h = o_inter[h * CHUNK:(h + 1) * CHUNK, :] + _bdot(a_h, v_h)
            ms = jnp.mean(o_h * o_h, axis=-1, keepdims=True)
            o_h = o_h * lax.rsqrt(ms + RMS_EPS) * ng_ref[...]
            g_h = gg[:, h * GLA_DV:(h + 1) * GLA_DV]
            o_ref[pl.ds(r0, CHUNK), h * GLA_DV:(h + 1) * GLA_DV] = (
                o_h * (g_h * jax.nn.sigmoid(g_h))).astype(o_ref.dtype)

        k_bar = k * jnp.exp(b_last - b)
        kv = _dot_tn(vb, k_bar)
        new_st = st * jnp.exp(b_last)
        for h in range(GLA_HEADS):
            new_st = new_st + kv[h * GLA_DV:(h + 1) * GLA_DV, :] * head_masks[h]
        st_ref[...] = new_st
        return carry

    lax.fori_loop(0, n_chunks, chunk, 0)


def _gla(p_gla, w2p, gate_b, norm_g, tt):
    bsz, seq, _ = p_gla.shape
    tt = min(tt, seq)
    return pl.pallas_call(
        functools.partial(_gla_kernel, n_chunks=tt // CHUNK),
        grid=(bsz, seq // tt),
        in_specs=[pl.BlockSpec((None, tt, GLA_COLS), lambda b, t: (b, t, 0)),
                  pl.BlockSpec((LANE, GLA_QK), lambda b, t: (0, 0)),
                  pl.BlockSpec((1, GLA_QK), lambda b, t: (0, 0)),
                  pl.BlockSpec((1, GLA_DV), lambda b, t: (0, 0))],
        out_specs=pl.BlockSpec((None, tt, GLA_V), lambda b, t: (b, t, 0)),
        out_shape=jax.ShapeDtypeStruct((bsz, seq, GLA_V), BF16),
        scratch_shapes=[pltpu.VMEM((GLA_DV, GLA_QK), F32)],
        compiler_params=_cparams(("parallel", "arbitrary")),
    )(p_gla, w2p, gate_b.reshape(1, GLA_QK), norm_g.reshape(1, GLA_DV))


RW_PAIR = 2 * RW_N
RW_NPAIR = RW_HEADS // 2


def _tri_inverse(l_strict, eye):
    acc = eye + l_strict
    pw = l_strict
    n = 2
    while n < CHUNK:
        pw = _bdot(pw, pw)
        acc = acc + _bdot(acc, pw)
        n *= 2
    return acc


def _rwkv_kernel(p_ref, prev_ref, mu_ref, vec_ref, w2_ref, a2_ref, g2_ref, seg_ref, o_ref,
                 st_ref, r_s, k_s, v_s, lw_s, av_s, bv_s, y_s, bonus_s, g_s, *, n_chunks):
    t_idx = pl.program_id(1)

    @pl.when(t_idx == 0)
    def _():
        st_ref[...] = jnp.zeros_like(st_ref)

    tt = p_ref.shape[0]
    w0, a0, k_k, k_a, r_k, ln_g, ln_b = [vec_ref[i:i + 1, :] for i in range(7)]
    seg = seg_ref[...]

    def seg_sum(x):
        return jnp.concatenate(
            [_dot_split_rhs(x[:, i * LANE:(i + 1) * LANE], seg) for i in range(RW_W // LANE)], axis=1)

    p = p_ref[...]
    rows = lax.broadcasted_iota(jnp.int32, (tt, 1), 0)
    first = jnp.where(t_idx > 0, prev_ref[7:8, :], 0.0)
    shifted = jnp.where(rows == 0, first, pltpu.roll(p, 1, 0))
    p = p + mu_ref[...] * (shifted - p)
    r = p[:, 0:RW_W]
    k = p[:, RW_W:2 * RW_W]
    v = p[:, 2 * RW_W:3 * RW_W]
    wl = p[:, 3 * RW_W:3 * RW_W + LANE]
    al = p[:, 3 * RW_W + LANE:3 * RW_W + 2 * LANE]
    gl = p[:, 3 * RW_W + 2 * LANE:RW_COLS]
    w_raw = -_softplus(-(w0 + _bdot(jnp.tanh(wl), w2_ref[...]))) - 0.5
    lw = -jnp.exp(w_raw)
    a = jax.nn.sigmoid(a0 + _bdot(al, a2_ref[...]))
    g = _bdot(jax.nn.sigmoid(gl), g2_ref[...])
    kk = k * k_k
    kk = kk * lax.rsqrt(jnp.maximum(seg_sum(kk * kk), 1e-24))
    k2 = k * (1.0 + (a - 1.0) * k_a)
    bonus_s[...] = seg_sum(r * k2 * r_k) * v
    g_s[...] = g
    r_s[...] = r
    k_s[...] = k2
    v_s[...] = v
    lw_s[...] = lw
    av_s[...] = -kk
    bv_s[...] = kk * a

    c_row = lax.broadcasted_iota(jnp.int32, (CHUNK, CHUNK), 0)
    c_col = lax.broadcasted_iota(jnp.int32, (CHUNK, CHUNK), 1)
    tri = c_col <= c_row
    tri_strict = c_col < c_row
    tri_b = tri.astype(BF16)
    eye = (c_col == c_row).astype(F32)
    lane = lax.broadcasted_iota(jnp.int32, (1, RW_PAIR), 1)
    hmask = [(lane < RW_N).astype(F32), (lane >= RW_N).astype(F32)]
    p_row = lax.broadcasted_iota(jnp.int32, (RW_PAIR, RW_PAIR), 0)
    p_col = lax.broadcasted_iota(jnp.int32, (RW_PAIR, RW_PAIR), 1)
    bd = ((p_row < RW_N) == (p_col < RW_N)).astype(F32)
    zeros_pair = jnp.zeros((CHUNK, RW_PAIR), F32)

    def chunk(c, carry):
        r0 = pl.multiple_of(c * CHUNK, CHUNK)
        for pi in range(RW_NPAIR):
            cols = slice(pi * RW_PAIR, (pi + 1) * RW_PAIR)
            rr = r_s[pl.ds(r0, CHUNK), cols]
            kc = k_s[pl.ds(r0, CHUNK), cols]
            vc = v_s[pl.ds(r0, CHUNK), cols]
            lwc = lw_s[pl.ds(r0, CHUNK), cols]
            avc = av_s[pl.ds(r0, CHUNK), cols]
            bvc = bv_s[pl.ds(r0, CHUNK), cols]
            gcum = _dot_split(tri_b, lwc)
            g_last = gcum[CHUNK - 1:CHUNK, :]
            e_pos = jnp.exp(gcum)
            e_neg = jnp.exp(-gcum)
            e_end = jnp.exp(g_last - gcum)
            r_t = rr * e_pos
            a_t = avc * jnp.exp(gcum - lwc)
            b_t = bvc * e_neg
            k_t = kc * e_neg
            b_bar = bvc * e_end
            k_bar = kc * e_end

            a_hat = zeros_pair
            u_hat = zeros_pair
            r_eff = r_t
            y0 = zeros_pair
            for h in range(2):
                m = hmask[h]
                a_m = a_t * m
                v_m = vc * m
                lhs = jnp.concatenate([a_m, r_t * m], axis=0)
                gb = _dot_nt(lhs, b_t)
                gk = _dot_nt(lhs, k_t)
                l_ab = jnp.where(tri_strict, gb[:CHUNK], 0.0)
                l_ak = jnp.where(tri_strict, gk[:CHUNK], 0.0)
                m_rb = jnp.where(tri, gb[CHUNK:], 0.0)
                m_rk = jnp.where(tri, gk[CHUNK:], 0.0)
                t_inv = _tri_inverse(l_ab, eye)
                x = _bdot(t_inv, jnp.concatenate([a_m, _bdot(l_ak, v_m)], axis=1))
                ry = _bdot(m_rb, x)
                a_hat = a_hat + x[:, :RW_PAIR]
                u_hat = u_hat + x[:, RW_PAIR:]
                r_eff = r_eff + ry[:, :RW_PAIR]
                y0 = y0 + ry[:, RW_PAIR:] + _bdot(m_rk, v_m)

            st = st_ref[pi]
            stb = st.astype(BF16)
            p_t = _dot_tn(a_hat, b_bar) * bd
            q_t = _dot_tn(jnp.concatenate([u_hat, vc], axis=0),
                          jnp.concatenate([b_bar, k_bar], axis=0)) * bd
            y_s[pl.ds(r0, CHUNK), cols] = _dot_nt(r_eff, stb) + y0
            st_ref[pi] = st * jnp.exp(g_last) + _bdot(stb, p_t) + q_t
        return carry

    lax.fori_loop(0, n_chunks, chunk, 0)

    y = y_s[...]
    inv_n = 1.0 / RW_N
    mu_y = seg_sum(y) * inv_n
    d = y - mu_y
    var = seg_sum(d * d) * inv_n
    yn = d * lax.rsqrt(var + RW_GN_EPS) * ln_g + ln_b
    o_ref[...] = ((yn + bonus_s[...]) * g_s[...]).astype(o_ref.dtype)


def _dot_split_rhs(x, m01):
    hi = x.astype(BF16)
    lo = (x - hi.astype(F32)).astype(BF16)
    return (jnp.dot(hi, m01, preferred_element_type=F32)
            + jnp.dot(lo, m01, preferred_element_type=F32))


def _rwkv(p_rw, mu, vecs, w2p, a2p, g2p, tt):
    bsz, seq, _ = p_rw.shape
    tt = min(tt, seq)
    seg = (np.arange(LANE)[:, None] // RW_N == np.arange(LANE)[None, :] // RW_N)
    seg = jnp.asarray(seg, BF16)
    row_blocks = tt // 8
    scratch = ([pltpu.VMEM((RW_NPAIR, RW_PAIR, RW_PAIR), F32)]
               + [pltpu.VMEM((tt, RW_W), F32) for _ in range(9)])
    return pl.pallas_call(
        functools.partial(_rwkv_kernel, n_chunks=tt // CHUNK),
        grid=(bsz, seq // tt),
        in_specs=[pl.BlockSpec((None, tt, RW_COLS), lambda b, t: (b, t, 0)),
                  pl.BlockSpec((None, 8, RW_COLS),
                               lambda b, t: (b, jnp.maximum(t * row_blocks - 1, 0), 0)),
                  pl.BlockSpec((1, RW_COLS), lambda b, t: (0, 0)),
                  pl.BlockSpec((8, RW_W), lambda b, t: (0, 0)),
                  pl.BlockSpec((LANE, RW_W), lambda b, t: (0, 0)),
                  pl.BlockSpec((LANE, RW_W), lambda b, t: (0, 0)),
                  pl.BlockSpec((2 * LANE, RW_W), lambda b, t: (0, 0)),
                  pl.BlockSpec((LANE, LANE), lambda b, t: (0, 0))],
        out_specs=pl.BlockSpec((None, tt, RW_W), lambda b, t: (b, t, 0)),
        out_shape=jax.ShapeDtypeStruct((bsz, seq, RW_W), BF16),
        scratch_shapes=scratch,
        compiler_params=_cparams(("parallel", "arbitrary")),
    )(p_rw, p_rw, mu, vecs, w2p, a2p, g2p, seg)


MLA_HD = LANE
MLA_ROPE_OFF = MLA_NOPE
ROLL_ROPE = LANE - MLA_ROPE


def _mla_proj_kernel(x_ref, pos_ref, freq_ref, sgn_ref, win_ref, qg_ref, wq_ref, kvg_ref, wk_ref,
                     wvt_ref, q_ref, k_ref, vt_ref):
    xb = x_ref[...].astype(BF16)
    p = jnp.dot(xb, win_ref[...], preferred_element_type=F32)
    cq = p[:, :MLA_Q_RANK]
    ckv = p[:, MLA_Q_RANK:MLA_Q_RANK + MLA_KV_RANK]
    kpe = p[:, MLA_Q_RANK + MLA_KV_RANK:]
    cqn = cq * lax.rsqrt(jnp.mean(cq * cq, axis=-1, keepdims=True) + RMS_EPS) * qg_ref[...]
    ckvn = ckv * lax.rsqrt(jnp.mean(ckv * ckv, axis=-1, keepdims=True) + RMS_EPS) * kvg_ref[...]
    cqb = cqn.astype(BF16)
    ckvb = ckvn.astype(BF16)

    ang = pos_ref[...].astype(F32) * freq_ref[...]
    lane = lax.broadcasted_iota(jnp.int32, (1, LANE), 1)
    is_rope = (lane >= MLA_ROPE_OFF) & (lane < MLA_ROPE_OFF + MLA_ROPE)
    cos_t = jnp.where(is_rope, jnp.cos(ang), 0.0)
    sin_t = jnp.sin(ang) * sgn_ref[...]
    scale = (MLA_NOPE + MLA_ROPE) ** -0.5
    q_c = jnp.where(lane < MLA_NOPE, 1.0, cos_t) * scale
    q_s = sin_t * scale

    k_rot = kpe * cos_t + pltpu.roll(kpe, ROLL_ROPE, 1) * sin_t
    for h in range(MLA_HEADS):
        cols = slice(h * MLA_HD, (h + 1) * MLA_HD)
        qh = jnp.dot(cqb, wq_ref[:, cols], preferred_element_type=F32)
        q_ref[:, cols] = (qh * q_c + pltpu.roll(qh, ROLL_ROPE, 1) * q_s).astype(q_ref.dtype)
        kh = jnp.dot(ckvb, wk_ref[:, cols], preferred_element_type=F32)
        k_ref[:, cols] = (kh + k_rot).astype(k_ref.dtype)
    vt = lax.dot_general(wvt_ref[...], ckvb, (((1,), (1,)), ((), ())),
                         preferred_element_type=F32)
    vrow = lax.broadcasted_iota(jnp.int32, (vt.shape[0], 1), 0)
    ones_rows = ((vrow % MLA_HD) >= MLA_V).astype(F32)
    vt_ref[...] = (vt + ones_rows).astype(vt_ref.dtype)


def _mla_proj(x2d, pos2d, freq, sgn, w_in, q_g, w_q, kv_g, w_k, w_vt, bsz, seq, tm):
    m, d = x2d.shape
    tm = min(tm, seq)
    nq = MLA_HEADS * MLA_HD
    spb = seq // tm
    const = lambda shape: pl.BlockSpec(shape, lambda i: (0,) * len(shape))
    return pl.pallas_call(
        _mla_proj_kernel,
        grid=(m // tm,),
        in_specs=[pl.BlockSpec((tm, d), lambda i: (i, 0)),
                  pl.BlockSpec((tm, 1), lambda i: (i, 0)),
                  const((1, LANE)), const((1, LANE)),
                  const(w_in.shape), const((1, MLA_Q_RANK)), const(w_q.shape),
                  const((1, MLA_KV_RANK)), const(w_k.shape), const(w_vt.shape)],
        out_specs=[pl.BlockSpec((tm, nq), lambda i: (i, 0)),
                   pl.BlockSpec((tm, nq), lambda i: (i, 0)),
                   pl.BlockSpec((None, nq, tm), lambda i: (i // spb, 0, i % spb))],
        out_shape=[jax.ShapeDtypeStruct((m, nq), BF16),
                   jax.ShapeDtypeStruct((m, nq), BF16),
                   jax.ShapeDtypeStruct((bsz, nq, seq), BF16)],
        compiler_params=_cparams(("parallel",)),
    )(x2d, pos2d, freq, sgn, w_in, q_g.reshape(1, -1), w_q, kv_g.reshape(1, -1), w_k, w_vt)


def _attn_kernel(q_ref, k_ref, vt_ref, o_ref, *, tq, tk):
    i = pl.program_id(2)
    kr = lax.broadcasted_iota(jnp.int32, (tk, tq), 0)
    qc = lax.broadcasted_iota(jnp.int32, (tk, tq), 1)
    diag_ok = (kr // CHUNK) <= (qc // CHUNK)
    outs = []
    for h in range(2):
        cols = slice(h * MLA_HD, (h + 1) * MLA_HD)
        qh = q_ref[:, cols]

        def step(j, carry, masked):
            m_run, acc = carry
            k0 = pl.multiple_of(j * tk, tk)
            kh = k_ref[pl.ds(k0, tk), cols]
            s = lax.dot_general(kh, qh, (((1,), (1,)), ((), ())),
                                preferred_element_type=F32)
            if masked:
                s = jnp.where(diag_ok, s, -jnp.inf)
            m_new = jnp.maximum(m_run, jnp.max(s, axis=0, keepdims=True))
            alpha = jnp.exp(m_run - m_new)
            pr = jnp.exp(s - m_new).astype(BF16)
            vth = vt_ref[h * MLA_HD:(h + 1) * MLA_HD, pl.ds(k0, tk)]
            acc = acc * alpha + jnp.dot(vth, pr, preferred_element_type=F32)
            return m_new, acc

        init = (jnp.full((1, tq), -jnp.inf, F32), jnp.zeros((MLA_HD, tq), F32))
        carry = lax.fori_loop(0, i, functools.partial(step, masked=False), init)
        _, acc = step(i, carry, True)
        o_t = acc[:MLA_V, :] / acc[MLA_V:MLA_V + 1, :]
        outs.append(o_t)
    o_ref[...] = jnp.concatenate(outs, axis=0).T.astype(o_ref.dtype)


def _attention(q, k, vt, tq):
    bsz, seq, nq = q.shape
    tq = min(tq, seq)
    pair = 2 * MLA_HD
    return pl.pallas_call(
        functools.partial(_attn_kernel, tq=tq, tk=tq),
        grid=(bsz, MLA_HEADS // 2, seq // tq),
        in_specs=[pl.BlockSpec((None, tq, pair), lambda b, h, i: (b, i, h)),
                  pl.BlockSpec((None, seq, pair), lambda b, h, i: (b, 0, h)),
                  pl.BlockSpec((None, pair, seq), lambda b, h, i: (b, h, 0))],
        out_specs=pl.BlockSpec((None, tq, 2 * MLA_V), lambda b, h, i: (b, i, h)),
        out_shape=jax.ShapeDtypeStruct((bsz, seq, MLA_HEADS * MLA_V), BF16),
        compiler_params=_cparams(("parallel", "parallel", "arbitrary")),
    )(q, k, vt)


def _pad_cols(w, width):
    return jnp.pad(w, ((0, 0), (0, width - w.shape[1])))


def _pad_rows(w, height):
    return jnp.pad(w, ((0, height - w.shape[0]), (0, 0)))


def _even_weights(w_in, mu):
    gla_in = 2 * GLA_QK + 2 * GLA_V + GLA_GATE_RANK
    w_gla, w_rw = w_in[:, :gla_in], w_in[:, gla_in:]
    o = 3 * RW_W
    rw_parts = [w_rw[:, :o],
                _pad_cols(w_rw[:, o:o + RW_DECAY_RANK], LANE),
                _pad_cols(w_rw[:, o + RW_DECAY_RANK:o + RW_DECAY_RANK + RW_A_RANK], LANE),
                _pad_cols(w_rw[:, o + RW_DECAY_RANK + RW_A_RANK:], 2 * LANE)]
    mu_parts = [mu[:o],
                jnp.pad(mu[o:o + RW_DECAY_RANK], (0, LANE - RW_DECAY_RANK)),
                jnp.pad(mu[o + RW_DECAY_RANK:o + RW_DECAY_RANK + RW_A_RANK], (0, LANE - RW_A_RANK)),
                jnp.pad(mu[o + RW_DECAY_RANK + RW_A_RANK:], (0, 2 * LANE - RW_GATE_RANK))]
    w_all = jnp.concatenate(rw_parts + [_pad_cols(w_gla, GLA_COLS)], axis=1).astype(BF16)
    return w_all, jnp.concatenate(mu_parts).reshape(1, RW_COLS)


def _mla_weights(w_in, w_q_b, w_kv_b):
    swap = np.arange(MLA_ROPE) ^ 1
    d = w_in.shape[0]
    kpe = w_in[:, MLA_Q_RANK + MLA_KV_RANK:]
    w_in_p = jnp.concatenate([w_in[:, :MLA_Q_RANK + MLA_KV_RANK],
                              jnp.zeros((d, MLA_NOPE), F32), kpe, kpe[:, swap]], axis=1).astype(BF16)
    wq = w_q_b.reshape(MLA_Q_RANK, MLA_HEADS, MLA_NOPE + MLA_ROPE)
    wq = jnp.concatenate([wq, wq[:, :, MLA_NOPE:][:, :, swap]], axis=-1)
    wq = wq.reshape(MLA_Q_RANK, MLA_HEADS * MLA_HD).astype(BF16)
    wkv = w_kv_b.reshape(MLA_KV_RANK, MLA_HEADS, MLA_NOPE + MLA_V)
    wk = jnp.pad(wkv[:, :, :MLA_NOPE], ((0, 0), (0, 0), (0, MLA_HD - MLA_NOPE)))
    wk = wk.reshape(MLA_KV_RANK, MLA_HEADS * MLA_HD).astype(BF16)
    wv = jnp.pad(wkv[:, :, MLA_NOPE:], ((0, 0), (0, 0), (0, MLA_HD - MLA_V)))
    wvt = wv.reshape(MLA_KV_RANK, MLA_HEADS * MLA_HD).T.astype(BF16)
    return w_in_p, wq, wk, wvt


def _rope_tables():
    inv_freq = ROPE_THETA ** (-jnp.arange(0, MLA_ROPE, 2, dtype=jnp.float32) / MLA_ROPE)
    d = np.arange(MLA_ROPE)
    freq = jnp.zeros((LANE,), F32).at[MLA_ROPE_OFF:MLA_ROPE_OFF + MLA_ROPE].set(inv_freq[d // 2])
    sgn = np.zeros((LANE,), np.float32)
    sgn[MLA_ROPE_OFF:MLA_ROPE_OFF + MLA_ROPE] = np.where(d % 2 == 0, -1.0, 1.0)
    return freq.reshape(1, LANE), jnp.asarray(sgn).reshape(1, LANE)


def kernel(x, positions, even_w_in, gla_gate_w2, gla_gate_b, gla_norm_g, rwkv_mu, rwkv_w0, rwkv_w2,
           rwkv_a0, rwkv_a2, rwkv_g2, rwkv_k_k, rwkv_k_a, rwkv_r_k, rwkv_ln_g, rwkv_ln_b, even_w_out,
           mla_w_in, mla_q_norm_g, mla_w_q_b, mla_kv_norm_g, mla_w_kv_b, mla_w_out,
           ffn_w_gate_up, ffn_w_down, ln_g, ln_b):
    bsz, seq, d = x.shape
    m = bsz * seq
    x2d = x.reshape(m, d)
    wgu = ffn_w_gate_up.astype(BF16)
    wdn = ffn_w_down.astype(BF16)

    w_all, mu_p = _even_weights(even_w_in[0], rwkv_mu[0])
    p_rw, p_gla = _inproj(x2d, w_all, (RW_COLS, GLA_COLS), F32, tm=256)
    o_a = _gla(p_gla.reshape(bsz, seq, GLA_COLS),
               _pad_rows(gla_gate_w2[0], LANE).astype(BF16), gla_gate_b[0], gla_norm_g[0], tt=512)
    vecs = jnp.stack([rwkv_w0[0], rwkv_a0[0], rwkv_k_k[0], rwkv_k_a[0], rwkv_r_k[0].reshape(RW_W),
                      rwkv_ln_g[0], rwkv_ln_b[0], jnp.zeros((RW_W,), F32)])
    o_b = _rwkv(p_rw.reshape(bsz, seq, RW_COLS), mu_p, vecs,
                _pad_rows(rwkv_w2[0], LANE).astype(BF16), _pad_rows(rwkv_a2[0], LANE).astype(BF16),
                _pad_rows(rwkv_g2[0], 2 * LANE).astype(BF16), tt=256)
    w_out = even_w_out[0].astype(BF16)
    x2d = _outproj_ln([o_a.reshape(m, GLA_V), o_b.reshape(m, RW_W)], [w_out[:GLA_V], w_out[GLA_V:]],
                      x2d, ln_g[0, 0], ln_b[0, 0], tm=512)
    x2d = _ffn_ln(x2d, wgu[0], wdn[0], ln_g[0, 1], ln_b[0, 1], tm=1024, th=256)

    w_in_p, wq, wk, wvt = _mla_weights(mla_w_in[0], mla_w_q_b[0], mla_w_kv_b[0])
    freq, sgn = _rope_tables()
    q, k, vt = _mla_proj(x2d, positions.reshape(m, 1), freq, sgn, w_in_p, mla_q_norm_g[0], wq,
                         mla_kv_norm_g[0], wk, wvt, bsz, seq, tm=256)
    nq = MLA_HEADS * MLA_HD
    o = _attention(q.reshape(bsz, seq, nq), k.reshape(bsz, seq, nq), vt, tq=256)
    x2d = _outproj_ln([o.reshape(m, MLA_HEADS * MLA_V)], [mla_w_out[0].astype(BF16)],
                      x2d, ln_g[1, 0], ln_b[1, 0], tm=512)
    x2d = _ffn_ln(x2d, wgu[1], wdn[1], ln_g[1, 1], ln_b[1, 1], tm=1024, th=256)
    return x2d.reshape(bsz, seq, d)
```

```python
import functools
import math

import jax
import jax.numpy as jnp
import numpy as np
from jax import lax
from jax.experimental import pallas as pl
from jax.experimental.pallas import tpu as pltpu

F32 = jnp.float32
BF16 = jnp.bfloat16

DEPTH = 2
CHUNK = 64
DN_ALPHA = (2.0 * DEPTH) ** 0.25
LN_EPS = 1e-5
RMS_EPS = 1e-6

GLA_HEADS = 4
GLA_DK = 64
GLA_DV = 128
GLA_GATE_RANK = 16
GLA_TAU = 16.0
GLA_QK = GLA_HEADS * GLA_DK
GLA_V = GLA_HEADS * GLA_DV
GLA_SUB = 16
GLA_EXP_CLAMP = 60.0

RW_HEADS = 8
RW_N = 64
RW_W = RW_HEADS * RW_N
RW_DECAY_RANK = 64
RW_A_RANK = 64
RW_GATE_RANK = 160
RW_GN_EPS = 64e-5

MLA_HEADS = 16
MLA_NOPE = 64
MLA_ROPE = 32
MLA_V = 64
MLA_Q_RANK = 768
MLA_KV_RANK = 256
ROPE_THETA = 10000.0

LANE = 128
VMEM_LIMIT = 48 * 1024 * 1024
ATT_TQ, ATT_TK, ATT_HEADS = 512, 1024, 2

RW_COLS = 3 * RW_W + LANE + LANE + 2 * LANE
GLA_COLS = 2 * GLA_QK + 2 * GLA_V + LANE


def _cparams(sem):
    return pltpu.CompilerParams(dimension_semantics=sem, vmem_limit_bytes=VMEM_LIMIT)


def _bdot(a, b):
    return jnp.dot(a.astype(BF16), b.astype(BF16), preferred_element_type=F32)


def _dot_nt(a, b):
    return lax.dot_general(a.astype(BF16), b.astype(BF16), (((1,), (1,)), ((), ())),
                           preferred_element_type=F32)


def _dot_tn(a, b):
    return lax.dot_general(a.astype(BF16), b.astype(BF16), (((0,), (0,)), ((), ())),
                           preferred_element_type=F32)


def _dot_split(m01, x):
    hi = x.astype(BF16)
    lo = (x - hi.astype(F32)).astype(BF16)
    m = m01.astype(BF16)
    return (jnp.dot(m, hi, preferred_element_type=F32)
            + jnp.dot(m, lo, preferred_element_type=F32))


def _softplus(x):
    return jnp.maximum(x, 0.0) + jnp.log(1.0 + jnp.exp(-jnp.abs(x)))


def _layer_norm(y, g, b):
    mu = jnp.mean(y, axis=-1, keepdims=True)
    d = y - mu
    var = jnp.mean(d * d, axis=-1, keepdims=True)
    return d * lax.rsqrt(var + LN_EPS) * g + b


def _inproj_kernel(x_ref, w_ref, *o_refs, widths, tn):
    xb = x_ref[...].astype(BF16)
    off = 0
    for o_ref, width in zip(o_refs, widths):
        for j in range(0, width, tn):
            w = min(tn, width - j)
            o_ref[:, j:j + w] = jnp.dot(xb, w_ref[:, off + j:off + j + w],
                                        preferred_element_type=F32).astype(o_ref.dtype)
        off += width


def _inproj(x2d, w_bf16, widths, out_dtype, tm):
    m, k = x2d.shape
    tm = min(tm, m)
    n = sum(widths)
    return pl.pallas_call(
        functools.partial(_inproj_kernel, widths=tuple(widths), tn=512),
        grid=(m // tm,),
        in_specs=[pl.BlockSpec((tm, k), lambda i: (i, 0)),
                  pl.BlockSpec((k, n), lambda i: (0, 0))],
        out_specs=[pl.BlockSpec((tm, wd), lambda i: (i, 0)) for wd in widths],
        out_shape=[jax.ShapeDtypeStruct((m, wd), out_dtype) for wd in widths],
        compiler_params=_cparams(("parallel",)),
    )(x2d, w_bf16)


def _outproj_ln_kernel(*refs, n_in):
    a_refs = refs[:n_in]
    w_refs = refs[n_in:2 * n_in]
    x_ref, g_ref, b_ref, o_ref = refs[2 * n_in:]
    y = DN_ALPHA * x_ref[...]
    for a_ref, w_ref in zip(a_refs, w_refs):
        y = y + jnp.dot(a_ref[...], w_ref[...], preferred_element_type=F32)
    o_ref[...] = _layer_norm(y, g_ref[...], b_ref[...])


def _outproj_ln(acts, weights, x2d, g, b, tm):
    m, d = x2d.shape
    tm = min(tm, m)
    n_in = len(acts)
    in_specs = ([pl.BlockSpec((tm, a.shape[1]), lambda i: (i, 0)) for a in acts]
                + [pl.BlockSpec(w.shape, lambda i: (0, 0)) for w in weights]
                + [pl.BlockSpec((tm, d), lambda i: (i, 0)),
                   pl.BlockSpec((1, d), lambda i: (0, 0)),
                   pl.BlockSpec((1, d), lambda i: (0, 0))])
    return pl.pallas_call(
        functools.partial(_outproj_ln_kernel, n_in=n_in),
        grid=(m // tm,),
        in_specs=in_specs,
        out_specs=pl.BlockSpec((tm, d), lambda i: (i, 0)),
        out_shape=jax.ShapeDtypeStruct((m, d), F32),
        compiler_params=_cparams(("parallel",)),
    )(*acts, *weights, x2d, g.reshape(1, d), b.reshape(1, d))


def _ffn_kernel(x_ref, wg_ref, wu_ref, wd_ref, g_ref, b_ref, o_ref, xb_ref, acc_ref):
    j = pl.program_id(1)

    @pl.when(j == 0)
    def _():
        xb_ref[...] = x_ref[...].astype(BF16)
        acc_ref[...] = jnp.zeros_like(acc_ref)

    xb = xb_ref[...]
    gate = jnp.dot(xb, wg_ref[...], preferred_element_type=F32)
    up = jnp.dot(xb, wu_ref[...], preferred_element_type=F32)
    h = (gate * jax.nn.sigmoid(gate) * up).astype(BF16)
    acc_ref[...] += jnp.dot(h, wd_ref[...], preferred_element_type=F32)

    @pl.when(j == pl.num_programs(1) - 1)
    def _():
        y = DN_ALPHA * x_ref[...] + acc_ref[...]
        o_ref[...] = _layer_norm(y, g_ref[...], b_ref[...])


def _ffn_ln(x2d, w_gate_up, w_down, g, b, tm, th):
    m, d = x2d.shape
    hidden = w_down.shape[0]
    tm = min(tm, m)
    nh = hidden // th
    return pl.pallas_call(
        _ffn_kernel,
        grid=(m // tm, nh),
        in_specs=[pl.BlockSpec((tm, d), lambda i, j: (i, 0)),
                  pl.BlockSpec((d, th), lambda i, j: (0, j)),
                  pl.BlockSpec((d, th), lambda i, j: (0, j + nh)),
                  pl.BlockSpec((th, d), lambda i, j: (j, 0)),
                  pl.BlockSpec((1, d), lambda i, j: (0, 0)),
                  pl.BlockSpec((1, d), lambda i, j: (0, 0))],
        out_specs=pl.BlockSpec((tm, d), lambda i, j: (i, 0)),
        out_shape=jax.ShapeDtypeStruct((m, d), F32),
        scratch_shapes=[pltpu.VMEM((tm, d), BF16), pltpu.VMEM((tm, d), F32)],
        compiler_params=_cparams(("parallel", "arbitrary")),
    )(x2d, w_gate_up, w_gate_up, w_down, g.reshape(1, d), b.reshape(1, d))


def _gla_kernel(p_ref, w2_ref, gb_ref, ng_ref, o_ref, st_ref, *, n_chunks):
    @pl.when(pl.program_id(1) == 0)
    def _():
        st_ref[...] = jnp.zeros_like(st_ref)

    c_row = lax.broadcasted_iota(jnp.int32, (CHUNK, CHUNK), 0)
    c_col = lax.broadcasted_iota(jnp.int32, (CHUNK, CHUNK), 1)
    tri = (c_col <= c_row)
    tri_loc = tri & (c_col >= (c_row // GLA_SUB) * GLA_SUB)
    cum_mat = jnp.concatenate([tri, tri_loc], axis=0).astype(BF16)
    lane = lax.broadcasted_iota(jnp.int32, (1, GLA_QK), 1)
    head_masks = [(lane // GLA_DK == h).astype(F32) for h in range(GLA_HEADS)]
    n_sub = CHUNK // GLA_SUB

    def chunk(c, carry):
        r0 = pl.multiple_of(c * CHUNK, CHUNK)
        q = p_ref[pl.ds(r0, CHUNK), 0:GLA_QK] * (GLA_DK ** -0.5)
        k = p_ref[pl.ds(r0, CHUNK), GLA_QK:2 * GLA_QK]
        v = p_ref[pl.ds(r0, CHUNK), 2 * GLA_QK:2 * GLA_QK + GLA_V]
        gg = p_ref[pl.ds(r0, CHUNK), 2 * GLA_QK + GLA_V:2 * GLA_QK + 2 * GLA_V]
        lr = p_ref[pl.ds(r0, CHUNK), 2 * GLA_QK + 2 * GLA_V:GLA_COLS]
        z = _bdot(lr, w2_ref[...]) + gb_ref[...]
        log_a = -_softplus(-z) / GLA_TAU
        cums = _dot_split(cum_mat, log_a)
        b = cums[:CHUNK]
        loc = cums[CHUNK:]
        b_last = b[CHUNK - 1:CHUNK, :]
        vb = v.astype(BF16)

        q_loc = q * jnp.exp(loc)
        a_blocks = []
        for i in range(n_sub):
            if i == 0:
                ref_i = jnp.zeros((1, GLA_QK), F32)
            else:
                ref_i = b[i * GLA_SUB - 1:i * GLA_SUB, :]
            k_i = k * jnp.exp(jnp.minimum(ref_i - b, GLA_EXP_CLAMP))
            q_i = q_loc[i * GLA_SUB:(i + 1) * GLA_SUB, :]
            lhs = jnp.concatenate([q_i * m for m in head_masks], axis=0)
            a_blocks.append(_dot_nt(lhs, k_i))

        q_b = q * jnp.exp(b)
        lhs_inter = jnp.concatenate([q_b * m for m in head_masks], axis=0)
        st = st_ref[...]
        o_inter = _dot_nt(lhs_inter, st)

        for h in range(GLA_HEADS):
            a_h = jnp.concatenate(
                [blk[h * GLA_SUB:(h + 1) * GLA_SUB, :] for blk in a_blocks], axis=0)
            a_h = jnp.where(tri, a_h, 0.0)
            v_h = vb[:, h * GLA_DV:(h + 1) * GLA_DV]
            o_h = o_inter[h * CHUNK:(h + 1) * CHUNK, :] + _bdot(a_h, v_h)
            ms = jnp.mean(o_h * o_h, axis=-1, keepdims=True)
            o_h = o_h * lax.rsqrt(ms + RMS_EPS) * ng_ref[...]
            g_h = gg[:, h * GLA_DV:(h + 1) * GLA_DV]
            o_ref[pl.ds(r0, CHUNK), h * GLA_DV:(h + 1) * GLA_DV] = (
                o_h * (g_h * jax.nn.sigmoid(g_h))).astype(o_ref.dtype)

        k_bar = k * jnp.exp(b_last - b)
        kv = _dot_tn(vb, k_bar)
        new_st = st * jnp.exp(b_last)
        for h in range(GLA_HEADS):
            new_st = new_st + kv[h * GLA_DV:(h + 1) * GLA_DV, :] * head_masks[h]
        st_ref[...] = new_st
        return carry

    lax.fori_loop(0, n_chunks, chunk, 0)


def _gla(p_gla, w2p, gate_b, norm_g, tt):
    bsz, seq, _ = p_gla.shape
    tt = min(tt, seq)
    return pl.pallas_call(
        functools.partial(_gla_kernel, n_chunks=tt // CHUNK),
        grid=(bsz, seq // tt),
        in_specs=[pl.BlockSpec((None, tt, GLA_COLS), lambda b, t: (b, t, 0)),
                  pl.BlockSpec((LANE, GLA_QK), lambda b, t: (0, 0)),
                  pl.BlockSpec((1, GLA_QK), lambda b, t: (0, 0)),
                  pl.BlockSpec((1, GLA_DV), lambda b, t: (0, 0))],
        out_specs=pl.BlockSpec((None, tt, GLA_V), lambda b, t: (b, t, 0)),
        out_shape=jax.ShapeDtypeStruct((bsz, seq, GLA_V), BF16),
        scratch_shapes=[pltpu.VMEM((GLA_DV, GLA_QK), F32)],
        compiler_params=_cparams(("parallel", "arbitrary")),
    )(p_gla, w2p, gate_b.reshape(1, GLA_QK), norm_g.reshape(1, GLA_DV))


RW_PAIR = 2 * RW_N
RW_NPAIR = RW_HEADS // 2


def _tri_inverse(l_strict, eye):
    acc = eye + l_strict
    pw = l_strict
    n = 2
    while n < CHUNK:
        pw = _bdot(pw, pw)
        acc = acc + _bdot(acc, pw)
        n *= 2
    return acc


def _rwkv_kernel(p_ref, prev_ref, mu_ref, vec_ref, w2_ref, a2_ref, g2_ref, seg_ref, o_ref,
                 st_ref, r_s, k_s, v_s, lw_s, av_s, bv_s, y_s, bonus_s, g_s, *, n_chunks):
    t_idx = pl.program_id(1)

    @pl.when(t_idx == 0)
    def _():
        st_ref[...] = jnp.zeros_like(st_ref)

    tt = p_ref.shape[0]
    w0, a0, k_k, k_a, r_k, ln_g, ln_b = [vec_ref[i:i + 1, :] for i in range(7)]
    seg = seg_ref[...]

    def seg_sum(x):
        return jnp.concatenate(
            [_dot_split_rhs(x[:, i * LANE:(i + 1) * LANE], seg) for i in range(RW_W // LANE)], axis=1)

    p = p_ref[...]
    rows = lax.broadcasted_iota(jnp.int32, (tt, 1), 0)
    first = jnp.where(t_idx > 0, prev_ref[7:8, :], 0.0)
    shifted = jnp.where(rows == 0, first, pltpu.roll(p, 1, 0))
    p = p + mu_ref[...] * (shifted - p)
    r = p[:, 0:RW_W]
    k = p[:, RW_W:2 * RW_W]
    v = p[:, 2 * RW_W:3 * RW_W]
    wl = p[:, 3 * RW_W:3 * RW_W + LANE]
    al = p[:, 3 * RW_W + LANE:3 * RW_W + 2 * LANE]
    gl = p[:, 3 * RW_W + 2 * LANE:RW_COLS]
    w_raw = -_softplus(-(w0 + _bdot(jnp.tanh(wl), w2_ref[...]))) - 0.5
    lw = -jnp.exp(w_raw)
    a = jax.nn.sigmoid(a0 + _bdot(al, a2_ref[...]))
    g = _bdot(jax.nn.sigmoid(gl), g2_ref[...])
    kk = k * k_k
    kk = kk * lax.rsqrt(jnp.maximum(seg_sum(kk * kk), 1e-24))
    k2 = k * (1.0 + (a - 1.0) * k_a)
    bonus_s[...] = seg_sum(r * k2 * r_k) * v
    g_s[...] = g
    r_s[...] = r
    k_s[...] = k2
    v_s[...] = v
    lw_s[...] = lw
    av_s[...] = -kk
    bv_s[...] = kk * a

    c_row = lax.broadcasted_iota(jnp.int32, (CHUNK, CHUNK), 0)
    c_col = lax.broadcasted_iota(jnp.int32, (CHUNK, CHUNK), 1)
    tri = c_col <= c_row
    tri_strict = c_col < c_row
    tri_b = tri.astype(BF16)
    eye = (c_col == c_row).astype(F32)
    lane = lax.broadcasted_iota(jnp.int32, (1, RW_PAIR), 1)
    hmask = [(lane < RW_N).astype(F32), (lane >= RW_N).astype(F32)]
    p_row = lax.broadcasted_iota(jnp.int32, (RW_PAIR, RW_PAIR), 0)
    p_col = lax.broadcasted_iota(jnp.int32, (RW_PAIR, RW_PAIR), 1)
    bd = ((p_row < RW_N) == (p_col < RW_N)).astype(F32)
    zeros_pair = jnp.zeros((CHUNK, RW_PAIR), F32)

    def chunk(c, carry):
        r0 = pl.multiple_of(c * CHUNK, CHUNK)
        for pi in range(RW_NPAIR):
            cols = slice(pi * RW_PAIR, (pi + 1) * RW_PAIR)
            rr = r_s[pl.ds(r0, CHUNK), cols]
            kc = k_s[pl.ds(r0, CHUNK), cols]
            vc = v_s[pl.ds(r0, CHUNK), cols]
            lwc = lw_s[pl.ds(r0, CHUNK), cols]
            avc = av_s[pl.ds(r0, CHUNK), cols]
            bvc = bv_s[pl.ds(r0, CHUNK), cols]
            gcum = _dot_split(tri_b, lwc)
            g_last = gcum[CHUNK - 1:CHUNK, :]
            e_pos = jnp.exp(gcum)
            e_neg = jnp.exp(-gcum)
            e_end = jnp.exp(g_last - gcum)
            r_t = rr * e_pos
            a_t = avc * jnp.exp(gcum - lwc)
            b_t = bvc * e_neg
            k_t = kc * e_neg
            b_bar = bvc * e_end
            k_bar = kc * e_end

            a_hat = zeros_pair
            u_hat = zeros_pair
            r_eff = r_t
            y0 = zeros_pair
            for h in range(2):
                m = hmask[h]
                a_m = a_t * m
                v_m = vc * m
                lhs = jnp.concatenate([a_m, r_t * m], axis=0)
                gb = _dot_nt(lhs, b_t)
                gk = _dot_nt(lhs, k_t)
                l_ab = jnp.where(tri_strict, gb[:CHUNK], 0.0)
                l_ak = jnp.where(tri_strict, gk[:CHUNK], 0.0)
                m_rb = jnp.where(tri, gb[CHUNK:], 0.0)
                m_rk = jnp.where(tri, gk[CHUNK:], 0.0)
                t_inv = _tri_inverse(l_ab, eye)
                x = _bdot(t_inv, jnp.concatenate([a_m, _bdot(l_ak, v_m)], axis=1))
                ry = _bdot(m_rb, x)
                a_hat = a_hat + x[:, :RW_PAIR]
                u_hat = u_hat + x[:, RW_PAIR:]
                r_eff = r_eff + ry[:, :RW_PAIR]
                y0 = y0 + ry[:, RW_PAIR:] + _bdot(m_rk, v_m)

            st = st_ref[pi]
            stb = st.astype(BF16)
            p_t = _dot_tn(a_hat, b_bar) * bd
            q_t = _dot_tn(jnp.concatenate([u_hat, vc], axis=0),
                          jnp.concatenate([b_bar, k_bar], axis=0)) * bd
            y_s[pl.ds(r0, CHUNK), cols] = _dot_nt(r_eff, stb) + y0
            st_ref[pi] = st * jnp.exp(g_last) + _bdot(stb, p_t) + q_t
        return carry

    lax.fori_loop(0, n_chunks, chunk, 0)

    y = y_s[...]
    inv_n = 1.0 / RW_N
    mu_y = seg_sum(y) * inv_n
    d = y - mu_y
    var = seg_sum(d * d) * inv_n
    yn = d * lax.rsqrt(var + RW_GN_EPS) * ln_g + ln_b
    o_ref[...] = ((yn + bonus_s[...]) * g_s[...]).astype(o_ref.dtype)


def _dot_split_rhs(x, m01):
    hi = x.astype(BF16)
    lo = (x - hi.astype(F32)).astype(BF16)
    return (jnp.dot(hi, m01, preferred_element_type=F32)
            + jnp.dot(lo, m01, preferred_element_type=F32))


def _rwkv(p_rw, mu, vecs, w2p, a2p, g2p, tt):
    bsz, seq, _ = p_rw.shape
    tt = min(tt, seq)
    seg = (np.arange(LANE)[:, None] // RW_N == np.arange(LANE)[None, :] // RW_N)
    seg = jnp.asarray(seg, BF16)
    row_blocks = tt // 8
    scratch = ([pltpu.VMEM((RW_NPAIR, RW_PAIR, RW_PAIR), F32)]
               + [pltpu.VMEM((tt, RW_W), F32) for _ in range(9)])
    return pl.pallas_call(
        functools.partial(_rwkv_kernel, n_chunks=tt // CHUNK),
        grid=(bsz, seq // tt),
        in_specs=[pl.BlockSpec((None, tt, RW_COLS), lambda b, t: (b, t, 0)),
                  pl.BlockSpec((None, 8, RW_COLS),
                               lambda b, t: (b, jnp.maximum(t * row_blocks - 1, 0), 0)),
                  pl.BlockSpec((1, RW_COLS), lambda b, t: (0, 0)),
                  pl.BlockSpec((8, RW_W), lambda b, t: (0, 0)),
                  pl.BlockSpec((LANE, RW_W), lambda b, t: (0, 0)),
                  pl.BlockSpec((LANE, RW_W), lambda b, t: (0, 0)),
                  pl.BlockSpec((2 * LANE, RW_W), lambda b, t: (0, 0)),
                  pl.BlockSpec((LANE, LANE), lambda b, t: (0, 0))],
        out_specs=pl.BlockSpec((None, tt, RW_W), lambda b, t: (b, t, 0)),
        out_shape=jax.ShapeDtypeStruct((bsz, seq, RW_W), BF16),
        scratch_shapes=scratch,
        compiler_params=_cparams(("parallel", "arbitrary")),
    )(p_rw, p_rw, mu, vecs, w2p, a2p, g2p, seg)


MLA_HD = LANE
MLA_ROPE_OFF = MLA_NOPE
ROLL_ROPE = LANE - MLA_ROPE


def _mla_proj_kernel(x_ref, pos_ref, freq_ref, sgn_ref, win_ref, qg_ref, wq_ref, kvg_ref, wk_ref,
                     wv_ref, q_ref, k_ref, v_ref):
    xb = x_ref[...].astype(BF16)
    p = jnp.dot(xb, win_ref[...], preferred_element_type=F32)
    cq = p[:, :MLA_Q_RANK]
    ckv = p[:, MLA_Q_RANK:MLA_Q_RANK + MLA_KV_RANK]
    kpe = p[:, MLA_Q_RANK + MLA_KV_RANK:]
    cqn = cq * lax.rsqrt(jnp.mean(cq * cq, axis=-1, keepdims=True) + RMS_EPS) * qg_ref[...]
    ckvn = ckv * lax.rsqrt(jnp.mean(ckv * ckv, axis=-1, keepdims=True) + RMS_EPS) * kvg_ref[...]
    cqb = cqn.astype(BF16)
    ckvb = ckvn.astype(BF16)

    ang = pos_ref[...].astype(F32) * freq_ref[...]
    lane = lax.broadcasted_iota(jnp.int32, (1, LANE), 1)
    is_rope = (lane >= MLA_ROPE_OFF) & (lane < MLA_ROPE_OFF + MLA_ROPE)
    cos_t = jnp.where(is_rope, jnp.cos(ang), 0.0)
    sin_t = jnp.sin(ang) * sgn_ref[...]
    scale = (MLA_NOPE + MLA_ROPE) ** -0.5
    q_c = jnp.where(lane < MLA_NOPE, 1.0, cos_t) * scale
    q_s = sin_t * scale

    k_rot = kpe * cos_t + pltpu.roll(kpe, ROLL_ROPE, 1) * sin_t
    ones_cols = (lane >= MLA_V).astype(F32)
    for h in range(MLA_HEADS):
        cols = slice(h * MLA_HD, (h + 1) * MLA_HD)
        qh = jnp.dot(cqb, wq_ref[:, cols], preferred_element_type=F32)
        q_ref[:, cols] = (qh * q_c + pltpu.roll(qh, ROLL_ROPE, 1) * q_s).astype(q_ref.dtype)
        kh = jnp.dot(ckvb, wk_ref[:, cols], preferred_element_type=F32)
        k_ref[:, cols] = (kh + k_rot).astype(k_ref.dtype)
        vh = jnp.dot(ckvb, wv_ref[:, cols], preferred_element_type=F32)
        v_ref[:, cols] = (vh + ones_cols).astype(v_ref.dtype)


def _mla_proj(x2d, pos2d, freq, sgn, w_in, q_g, w_q, kv_g, w_k, w_v, tm):
    m, d = x2d.shape
    tm = min(tm, m)
    nq = MLA_HEADS * MLA_HD
    const = lambda shape: pl.BlockSpec(shape, lambda i: (0,) * len(shape))
    return pl.pallas_call(
        _mla_proj_kernel,
        grid=(m // tm,),
        in_specs=[pl.BlockSpec((tm, d), lambda i: (i, 0)),
                  pl.BlockSpec((tm, 1), lambda i: (i, 0)),
                  const((1, LANE)), const((1, LANE)),
                  const(w_in.shape), const((1, MLA_Q_RANK)), const(w_q.shape),
                  const((1, MLA_KV_RANK)), const(w_k.shape), const(w_v.shape)],
        out_specs=[pl.BlockSpec((tm, nq), lambda i: (i, 0)) for _ in range(3)],
        out_shape=[jax.ShapeDtypeStruct((m, nq), BF16) for _ in range(3)],
        name="mla_proj",
        compiler_params=_cparams(("parallel",)),
    )(x2d, pos2d, freq, sgn, w_in, q_g.reshape(1, -1), w_q, kv_g.reshape(1, -1), w_k, w_v)


def _attn_kernel(q_ref, k_ref, v_ref, o_ref, *, tq, tk, n_heads):
    i = pl.program_id(2)
    q0 = i * tq
    n_full = q0 // tk
    q_chunk = (lax.broadcasted_iota(jnp.int32, (tq, tk), 0) + q0) // CHUNK
    k_col = lax.broadcasted_iota(jnp.int32, (tq, tk), 1)
    qhs = [q_ref[:, h * MLA_HD:(h + 1) * MLA_HD] for h in range(n_heads)]

    def step(j, carry, masked):
        k0 = pl.multiple_of(j * tk, tk)
        if masked:
            ok = ((k_col + k0) // CHUNK) <= q_chunk
        new = []
        for h in range(n_heads):
            m_run, acc = carry[h]
            cols = slice(h * MLA_HD, (h + 1) * MLA_HD)
            s = lax.dot_general(qhs[h], k_ref[pl.ds(k0, tk), cols], (((1,), (1,)), ((), ())),
                                preferred_element_type=F32)
            if masked:
                s = jnp.where(ok, s, -jnp.inf)
            m_new = jnp.maximum(m_run, jnp.max(s, axis=1, keepdims=True))
            alpha = jnp.exp(m_run - m_new)
            pr = jnp.exp(s - m_new).astype(BF16)
            acc = acc * alpha + jnp.dot(pr, v_ref[pl.ds(k0, tk), cols],
                                        preferred_element_type=F32)
            new.append((m_new, acc))
        return tuple(new)

    init = tuple((jnp.full((tq, 1), -jnp.inf, F32), jnp.zeros((tq, MLA_HD), F32))
                 for _ in range(n_heads))
    carry = lax.fori_loop(0, n_full, functools.partial(step, masked=False), init)
    carry = step(n_full, carry, True)
    lane = lax.broadcasted_iota(jnp.int32, (1, MLA_HD), 1)
    for hp in range(n_heads // 2):
        acc0, acc1 = carry[2 * hp][1], carry[2 * hp + 1][1]
        o0 = acc0 / pltpu.roll(acc0, MLA_V, 1)
        o1 = pltpu.roll(acc1, MLA_V, 1) / acc1
        o_ref[:, hp * MLA_HD:(hp + 1) * MLA_HD] = jnp.where(lane < MLA_V, o0, o1).astype(o_ref.dtype)


def _attention(q, k, v, tq, tk, n_heads):
    bsz, seq, nq = q.shape
    tq = min(tq, seq)
    tk = min(tk, seq)
    width = n_heads * MLA_HD
    return pl.pallas_call(
        functools.partial(_attn_kernel, tq=tq, tk=tk, n_heads=n_heads),
        grid=(bsz, MLA_HEADS // n_heads, seq // tq),
        in_specs=[pl.BlockSpec((None, tq, width), lambda b, h, i: (b, i, h)),
                  pl.BlockSpec((None, seq, width), lambda b, h, i: (b, 0, h)),
                  pl.BlockSpec((None, seq, width), lambda b, h, i: (b, 0, h))],
        out_specs=pl.BlockSpec((None, tq, n_heads * MLA_V), lambda b, h, i: (b, i, h)),
        out_shape=jax.ShapeDtypeStruct((bsz, seq, MLA_HEADS * MLA_V), BF16),
        name="mla_attention",
        compiler_params=_cparams(("parallel", "parallel", "arbitrary")),
    )(q, k, v)


def _pad_cols(w, width):
    return jnp.pad(w, ((0, 0), (0, width - w.shape[1])))


def _pad_rows(w, height):
    return jnp.pad(w, ((0, height - w.shape[0]), (0, 0)))


def _even_weights(w_in, mu):
    gla_in = 2 * GLA_QK + 2 * GLA_V + GLA_GATE_RANK
    w_gla, w_rw = w_in[:, :gla_in], w_in[:, gla_in:]
    o = 3 * RW_W
    rw_parts = [w_rw[:, :o],
                _pad_cols(w_rw[:, o:o + RW_DECAY_RANK], LANE),
                _pad_cols(w_rw[:, o + RW_DECAY_RANK:o + RW_DECAY_RANK + RW_A_RANK], LANE),
                _pad_cols(w_rw[:, o + RW_DECAY_RANK + RW_A_RANK:], 2 * LANE)]
    mu_parts = [mu[:o],
                jnp.pad(mu[o:o + RW_DECAY_RANK], (0, LANE - RW_DECAY_RANK)),
                jnp.pad(mu[o + RW_DECAY_RANK:o + RW_DECAY_RANK + RW_A_RANK], (0, LANE - RW_A_RANK)),
                jnp.pad(mu[o + RW_DECAY_RANK + RW_A_RANK:], (0, 2 * LANE - RW_GATE_RANK))]
    w_all = jnp.concatenate(rw_parts + [_pad_cols(w_gla, GLA_COLS)], axis=1).astype(BF16)
    return w_all, jnp.concatenate(mu_parts).reshape(1, RW_COLS)


def _mla_weights(w_in, w_q_b, w_kv_b):
    swap = np.arange(MLA_ROPE) ^ 1
    d = w_in.shape[0]
    kpe = w_in[:, MLA_Q_RANK + MLA_KV_RANK:]
    w_in_p = jnp.concatenate([w_in[:, :MLA_Q_RANK + MLA_KV_RANK],
                              jnp.zeros((d, MLA_NOPE), F32), kpe, kpe[:, swap]], axis=1).astype(BF16)
    wq = w_q_b.reshape(MLA_Q_RANK, MLA_HEADS, MLA_NOPE + MLA_ROPE)
    wq = jnp.concatenate([wq, wq[:, :, MLA_NOPE:][:, :, swap]], axis=-1)
    wq = wq.reshape(MLA_Q_RANK, MLA_HEADS * MLA_HD).astype(BF16)
    wkv = w_kv_b.reshape(MLA_KV_RANK, MLA_HEADS, MLA_NOPE + MLA_V)
    wk = jnp.pad(wkv[:, :, :MLA_NOPE], ((0, 0), (0, 0), (0, MLA_HD - MLA_NOPE)))
    wk = wk.reshape(MLA_KV_RANK, MLA_HEADS * MLA_HD).astype(BF16)
    wv = jnp.pad(wkv[:, :, MLA_NOPE:], ((0, 0), (0, 0), (0, MLA_HD - MLA_V)))
    wv = wv.reshape(MLA_KV_RANK, MLA_HEADS * MLA_HD).astype(BF16)
    return w_in_p, wq, wk, wv


def _rope_tables():
    inv_freq = ROPE_THETA ** (-jnp.arange(0, MLA_ROPE, 2, dtype=jnp.float32) / MLA_ROPE)
    d = np.arange(MLA_ROPE)
    freq = jnp.zeros((LANE,), F32).at[MLA_ROPE_OFF:MLA_ROPE_OFF + MLA_ROPE].set(inv_freq[d // 2])
    sgn = np.zeros((LANE,), np.float32)
    sgn[MLA_ROPE_OFF:MLA_ROPE_OFF + MLA_ROPE] = np.where(d % 2 == 0, -1.0, 1.0)
    return freq.reshape(1, LANE), jnp.asarray(sgn).reshape(1, LANE)


def kernel(x, positions, even_w_in, gla_gate_w2, gla_gate_b, gla_norm_g, rwkv_mu, rwkv_w0, rwkv_w2,
           rwkv_a0, rwkv_a2, rwkv_g2, rwkv_k_k, rwkv_k_a, rwkv_r_k, rwkv_ln_g, rwkv_ln_b, even_w_out,
           mla_w_in, mla_q_norm_g, mla_w_q_b, mla_kv_norm_g, mla_w_kv_b, mla_w_out,
           ffn_w_gate_up, ffn_w_down, ln_g, ln_b):
    bsz, seq, d = x.shape
    m = bsz * seq
    x2d = x.reshape(m, d)
    wgu = ffn_w_gate_up.astype(BF16)
    wdn = ffn_w_down.astype(BF16)

    w_all, mu_p = _even_weights(even_w_in[0], rwkv_mu[0])
    p_rw, p_gla = _inproj(x2d, w_all, (RW_COLS, GLA_COLS), F32, tm=256)
    o_a = _gla(p_gla.reshape(bsz, seq, GLA_COLS),
               _pad_rows(gla_gate_w2[0], LANE).astype(BF16), gla_gate_b[0], gla_norm_g[0], tt=512)
    vecs = jnp.stack([rwkv_w0[0], rwkv_a0[0], rwkv_k_k[0], rwkv_k_a[0], rwkv_r_k[0].reshape(RW_W),
                      rwkv_ln_g[0], rwkv_ln_b[0], jnp.zeros((RW_W,), F32)])
    o_b = _rwkv(p_rw.reshape(bsz, seq, RW_COLS), mu_p, vecs,
                _pad_rows(rwkv_w2[0], LANE).astype(BF16), _pad_rows(rwkv_a2[0], LANE).astype(BF16),
                _pad_rows(rwkv_g2[0], 2 * LANE).astype(BF16), tt=256)
    w_out = even_w_out[0].astype(BF16)
    x2d = _outproj_ln([o_a.reshape(m, GLA_V), o_b.reshape(m, RW_W)], [w_out[:GLA_V], w_out[GLA_V:]],
                      x2d, ln_g[0, 0], ln_b[0, 0], tm=512)
    x2d = _ffn_ln(x2d, wgu[0], wdn[0], ln_g[0, 1], ln_b[0, 1], tm=1024, th=256)

    w_in_p, wq, wk, wv = _mla_weights(mla_w_in[0], mla_w_q_b[0], mla_w_kv_b[0])
    freq, sgn = _rope_tables()
    q, k, v = _mla_proj(x2d, positions.reshape(m, 1), freq, sgn, w_in_p, mla_q_norm_g[0], wq,
                        mla_kv_norm_g[0], wk, wv, tm=256)
    nq = MLA_HEADS * MLA_HD
    o = _attention(q.reshape(bsz, seq, nq), k.reshape(bsz, seq, nq), v.reshape(bsz, seq, nq),
                   tq=ATT_TQ, tk=ATT_TK, n_heads=ATT_HEADS)
    x2d = _outproj_ln([o.reshape(m, MLA_HEADS * MLA_V)], [mla_w_out[0].astype(BF16)],
                      x2d, ln_g[1, 0], ln_b[1, 0], tm=512)
    x2d = _ffn_ln(x2d, wgu[1], wdn[1], ln_g[1, 1], ln_b[1, 1], tm=1024, th=256)
    return x2d.reshape(bsz, seq, d)
```

```python
import functools
import math

import jax
import jax.numpy as jnp
import numpy as np
from jax import lax
from jax.experimental import pallas as pl
from jax.experimental.pallas import tpu as pltpu

F32 = jnp.float32
BF16 = jnp.bfloat16

DEPTH = 2
CHUNK = 64
DN_ALPHA = (2.0 * DEPTH) ** 0.25
LN_EPS = 1e-5
RMS_EPS = 1e-6

GLA_HEADS = 4
GLA_DK = 64
GLA_DV = 128
GLA_GATE_RANK = 16
GLA_TAU = 16.0
GLA_QK = GLA_HEADS * GLA_DK
GLA_V = GLA_HEADS * GLA_DV
GLA_SUB = 16
GLA_EXP_CLAMP = 60.0

RW_HEADS = 8
RW_N = 64
RW_W = RW_HEADS * RW_N
RW_DECAY_RANK = 64
RW_A_RANK = 64
RW_GATE_RANK = 160
RW_GN_EPS = 64e-5

MLA_HEADS = 16
MLA_NOPE = 64
MLA_ROPE = 32
MLA_V = 64
MLA_Q_RANK = 768
MLA_KV_RANK = 256
ROPE_THETA = 10000.0

LANE = 128
VMEM_LIMIT = 48 * 1024 * 1024
ATT_TQ, ATT_TK, ATT_HEADS = 512, 1024, 2

RW_COLS = 3 * RW_W + LANE + LANE + 2 * LANE
GLA_COLS = 2 * GLA_QK + 2 * GLA_V + LANE


def _cparams(sem):
    return pltpu.CompilerParams(dimension_semantics=sem, vmem_limit_bytes=VMEM_LIMIT)


def _bdot(a, b):
    return jnp.dot(a.astype(BF16), b.astype(BF16), preferred_element_type=F32)


def _dot_nt(a, b):
    return lax.dot_general(a.astype(BF16), b.astype(BF16), (((1,), (1,)), ((), ())),
                           preferred_element_type=F32)


def _dot_tn(a, b):
    return lax.dot_general(a.astype(BF16), b.astype(BF16), (((0,), (0,)), ((), ())),
                           preferred_element_type=F32)


def _dot_split(m01, x):
    hi = x.astype(BF16)
    lo = (x - hi.astype(F32)).astype(BF16)
    m = m01.astype(BF16)
    return (jnp.dot(m, hi, preferred_element_type=F32)
            + jnp.dot(m, lo, preferred_element_type=F32))


def _softplus(x):
    return jnp.maximum(x, 0.0) + jnp.log(1.0 + jnp.exp(-jnp.abs(x)))


def _layer_norm(y, g, b):
    mu = jnp.mean(y, axis=-1, keepdims=True)
    d = y - mu
    var = jnp.mean(d * d, axis=-1, keepdims=True)
    return d * lax.rsqrt(var + LN_EPS) * g + b


def _inproj_kernel(x_ref, w_ref, *o_refs, widths, tn):
    xb = x_ref[...].astype(BF16)
    off = 0
    for o_ref, width in zip(o_refs, widths):
        for j in range(0, width, tn):
            w = min(tn, width - j)
            o_ref[:, j:j + w] = jnp.dot(xb, w_ref[:, off + j:off + j + w],
                                        preferred_element_type=F32).astype(o_ref.dtype)
        off += width


def _inproj(x2d, w_bf16, widths, out_dtype, tm):
    m, k = x2d.shape
    tm = min(tm, m)
    n = sum(widths)
    return pl.pallas_call(
        functools.partial(_inproj_kernel, widths=tuple(widths), tn=512),
        grid=(m // tm,),
        in_specs=[pl.BlockSpec((tm, k), lambda i: (i, 0)),
                  pl.BlockSpec((k, n), lambda i: (0, 0))],
        out_specs=[pl.BlockSpec((tm, wd), lambda i: (i, 0)) for wd in widths],
        out_shape=[jax.ShapeDtypeStruct((m, wd), out_dtype) for wd in widths],
        compiler_params=_cparams(("parallel",)),
    )(x2d, w_bf16)


def _outproj_ln_kernel(*refs, n_in):
    a_refs = refs[:n_in]
    w_refs = refs[n_in:2 * n_in]
    x_ref, g_ref, b_ref, o_ref = refs[2 * n_in:]
    y = DN_ALPHA * x_ref[...]
    for a_ref, w_ref in zip(a_refs, w_refs):
        y = y + jnp.dot(a_ref[...], w_ref[...], preferred_element_type=F32)
    o_ref[...] = _layer_norm(y, g_ref[...], b_ref[...])


def _outproj_ln(acts, weights, x2d, g, b, tm):
    m, d = x2d.shape
    tm = min(tm, m)
    n_in = len(acts)
    in_specs = ([pl.BlockSpec((tm, a.shape[1]), lambda i: (i, 0)) for a in acts]
                + [pl.BlockSpec(w.shape, lambda i: (0, 0)) for w in weights]
                + [pl.BlockSpec((tm, d), lambda i: (i, 0)),
                   pl.BlockSpec((1, d), lambda i: (0, 0)),
                   pl.BlockSpec((1, d), lambda i: (0, 0))])
    return pl.pallas_call(
        functools.partial(_outproj_ln_kernel, n_in=n_in),
        grid=(m // tm,),
        in_specs=in_specs,
        out_specs=pl.BlockSpec((tm, d), lambda i: (i, 0)),
        out_shape=jax.ShapeDtypeStruct((m, d), F32),
        compiler_params=_cparams(("parallel",)),
    )(*acts, *weights, x2d, g.reshape(1, d), b.reshape(1, d))


def _ffn_kernel(x_ref, wg_ref, wu_ref, wd_ref, g_ref, b_ref, o_ref, xb_ref, acc_ref):
    j = pl.program_id(1)

    @pl.when(j == 0)
    def _():
        xb_ref[...] = x_ref[...].astype(BF16)
        acc_ref[...] = jnp.zeros_like(acc_ref)

    xb = xb_ref[...]
    gate = jnp.dot(xb, wg_ref[...], preferred_element_type=F32)
    up = jnp.dot(xb, wu_ref[...], preferred_element_type=F32)
    h = (gate * jax.nn.sigmoid(gate) * up).astype(BF16)
    acc_ref[...] += jnp.dot(h, wd_ref[...], preferred_element_type=F32)

    @pl.when(j == pl.num_programs(1) - 1)
    def _():
        y = DN_ALPHA * x_ref[...] + acc_ref[...]
        o_ref[...] = _layer_norm(y, g_ref[...], b_ref[...])


def _ffn_ln(x2d, w_gate_up, w_down, g, b, tm, th):
    m, d = x2d.shape
    hidden = w_down.shape[0]
    tm = min(tm, m)
    nh = hidden // th
    return pl.pallas_call(
        _ffn_kernel,
        grid=(m // tm, nh),
        in_specs=[pl.BlockSpec((tm, d), lambda i, j: (i, 0)),
                  pl.BlockSpec((d, th), lambda i, j: (0, j)),
                  pl.BlockSpec((d, th), lambda i, j: (0, j + nh)),
                  pl.BlockSpec((th, d), lambda i, j: (j, 0)),
                  pl.BlockSpec((1, d), lambda i, j: (0, 0)),
                  pl.BlockSpec((1, d), lambda i, j: (0, 0))],
        out_specs=pl.BlockSpec((tm, d), lambda i, j: (i, 0)),
        out_shape=jax.ShapeDtypeStruct((m, d), F32),
        scratch_shapes=[pltpu.VMEM((tm, d), BF16), pltpu.VMEM((tm, d), F32)],
        compiler_params=_cparams(("parallel", "arbitrary")),
    )(x2d, w_gate_up, w_gate_up, w_down, g.reshape(1, d), b.reshape(1, d))


def _gla_kernel(p_ref, w2_ref, gb_ref, ng_ref, o_ref, st_ref, *, n_chunks):
    @pl.when(pl.program_id(1) == 0)
    def _():
        st_ref[...] = jnp.zeros_like(st_ref)

    c_row = lax.broadcasted_iota(jnp.int32, (CHUNK, CHUNK), 0)
    c_col = lax.broadcasted_iota(jnp.int32, (CHUNK, CHUNK), 1)
    tri = (c_col <= c_row)
    tri_loc = tri & (c_col >= (c_row // GLA_SUB) * GLA_SUB)
    cum_mat = jnp.concatenate([tri, tri_loc], axis=0).astype(BF16)
    lane = lax.broadcasted_iota(jnp.int32, (1, GLA_QK), 1)
    head_masks = [(lane // GLA_DK == h).astype(F32) for h in range(GLA_HEADS)]
    n_sub = CHUNK // GLA_SUB

    def chunk(c, carry):
        r0 = pl.multiple_of(c * CHUNK, CHUNK)
        q = p_ref[pl.ds(r0, CHUNK), 0:GLA_QK] * (GLA_DK ** -0.5)
        k = p_ref[pl.ds(r0, CHUNK), GLA_QK:2 * GLA_QK]
        v = p_ref[pl.ds(r0, CHUNK), 2 * GLA_QK:2 * GLA_QK + GLA_V]
        gg = p_ref[pl.ds(r0, CHUNK), 2 * GLA_QK + GLA_V:2 * GLA_QK + 2 * GLA_V]
        lr = p_ref[pl.ds(r0, CHUNK), 2 * GLA_QK + 2 * GLA_V:GLA_COLS]
        z = _bdot(lr, w2_ref[...]) + gb_ref[...]
        log_a = -_softplus(-z) / GLA_TAU
        cums = _dot_split(cum_mat, log_a)
        b = cums[:CHUNK]
        loc = cums[CHUNK:]
        b_last = b[CHUNK - 1:CHUNK, :]
        vb = v.astype(BF16)

        q_loc = q * jnp.exp(loc)
        a_blocks = []
        for i in range(n_sub):
            if i == 0:
                ref_i = jnp.zeros((1, GLA_QK), F32)
            else:
                ref_i = b[i * GLA_SUB - 1:i * GLA_SUB, :]
            k_i = k * jnp.exp(jnp.minimum(ref_i - b, GLA_EXP_CLAMP))
            q_i = q_loc[i * GLA_SUB:(i + 1) * GLA_SUB, :]
            lhs = jnp.concatenate([q_i * m for m in head_masks], axis=0)
            a_blocks.append(_dot_nt(lhs, k_i))

        q_b = q * jnp.exp(b)
        lhs_inter = jnp.concatenate([q_b * m for m in head_masks], axis=0)
        st = st_ref[...]
        o_inter = _dot_nt(lhs_inter, st)

        for h in range(GLA_HEADS):
            a_h = jnp.concatenate(
                [blk[h * GLA_SUB:(h + 1) * GLA_SUB, :] for blk in a_blocks], axis=0)
            a_h = jnp.where(tri, a_h, 0.0)
            v_h = vb[:, h * GLA_DV:(h + 1) * GLA_DV]
            o_h = o_inter[h * CHUNK:(h + 1) * CHUNK, :] + _bdot(a_h, v_h)
            ms = jnp.mean(o_h * o_h, axis=-1, keepdims=True)
            o_h = o_h * lax.rsqrt(ms + RMS_EPS) * ng_ref[...]
            g_h = gg[:, h * GLA_DV:(h + 1) * GLA_DV]
            o_ref[pl.ds(r0, CHUNK), h * GLA_DV:(h + 1) * GLA_DV] = (
                o_h * (g_h * jax.nn.sigmoid(g_h))).astype(o_ref.dtype)

        k_bar = k * jnp.exp(b_last - b)
        kv = _dot_tn(vb, k_bar)
        new_st = st * jnp.exp(b_last)
        for h in range(GLA_HEADS):
            new_st = new_st + kv[h * GLA_DV:(h + 1) * GLA_DV, :] * head_masks[h]
        st_ref[...] = new_st
        return carry

    lax.fori_loop(0, n_chunks, chunk, 0)


def _gla(p_gla, w2p, gate_b, norm_g, tt):
    bsz, seq, _ = p_gla.shape
    tt = min(tt, seq)
    return pl.pallas_call(
        functools.partial(_gla_kernel, n_chunks=tt // CHUNK),
        grid=(bsz, seq // tt),
        in_specs=[pl.BlockSpec((None, tt, GLA_COLS), lambda b, t: (b, t, 0)),
                  pl.BlockSpec((LANE, GLA_QK), lambda b, t: (0, 0)),
                  pl.BlockSpec((1, GLA_QK), lambda b, t: (0, 0)),
                  pl.BlockSpec((1, GLA_DV), lambda b, t: (0, 0))],
        out_specs=pl.BlockSpec((None, tt, GLA_V), lambda b, t: (b, t, 0)),
        out_shape=jax.ShapeDtypeStruct((bsz, seq, GLA_V), BF16),
        scratch_shapes=[pltpu.VMEM((GLA_DV, GLA_QK), F32)],
        compiler_params=_cparams(("parallel", "arbitrary")),
    )(p_gla, w2p, gate_b.reshape(1, GLA_QK), norm_g.reshape(1, GLA_DV))


RW_PAIR = 2 * RW_N
RW_NPAIR = RW_HEADS // 2
RW_PAR_CHUNKS = 4
RW_TT = 256
GLA_TT = 512


def _rwkv_kernel(p_ref, prev_ref, mu_ref, vec_ref, w2_ref, a2_ref, g2_ref, seg_ref, o_ref,
                 st_ref, r_s, k_s, v_s, lw_s, av_s, bv_s, y_s, bonus_s, g_s, *, n_chunks):
    t_idx = pl.program_id(1)

    @pl.when(t_idx == 0)
    def _():
        st_ref[...] = jnp.zeros_like(st_ref)

    tt = p_ref.shape[0]
    w0, a0, k_k, k_a, r_k, ln_g, ln_b = [vec_ref[i:i + 1, :] for i in range(7)]
    seg = seg_ref[...]

    def seg_sum(x):
        return jnp.concatenate(
            [_dot_split_rhs(x[:, i * LANE:(i + 1) * LANE], seg) for i in range(RW_W // LANE)], axis=1)

    p = p_ref[...]
    rows = lax.broadcasted_iota(jnp.int32, (tt, 1), 0)
    first = jnp.where(t_idx > 0, prev_ref[7:8, :], 0.0)
    shifted = jnp.where(rows == 0, first, pltpu.roll(p, 1, 0))
    p = p + mu_ref[...] * (shifted - p)
    r = p[:, 0:RW_W]
    k = p[:, RW_W:2 * RW_W]
    v = p[:, 2 * RW_W:3 * RW_W]
    wl = p[:, 3 * RW_W:3 * RW_W + LANE]
    al = p[:, 3 * RW_W + LANE:3 * RW_W + 2 * LANE]
    gl = p[:, 3 * RW_W + 2 * LANE:RW_COLS]
    w_raw = -_softplus(-(w0 + _bdot(jnp.tanh(wl), w2_ref[...]))) - 0.5
    lw = -jnp.exp(w_raw)
    a = jax.nn.sigmoid(a0 + _bdot(al, a2_ref[...]))
    g = _bdot(jax.nn.sigmoid(gl), g2_ref[...])
    kk = k * k_k
    kk = kk * lax.rsqrt(jnp.maximum(seg_sum(kk * kk), 1e-24))
    k2 = k * (1.0 + (a - 1.0) * k_a)
    bonus_s[...] = seg_sum(r * k2 * r_k) * v
    g_s[...] = g
    r_s[...] = r
    k_s[...] = k2
    v_s[...] = v
    lw_s[...] = lw
    av_s[...] = -kk
    bv_s[...] = kk * a

    c_row = lax.broadcasted_iota(jnp.int32, (CHUNK, CHUNK), 0)
    c_col = lax.broadcasted_iota(jnp.int32, (CHUNK, CHUNK), 1)
    tri = c_col <= c_row
    tri_strict = c_col < c_row
    tri_b = tri.astype(BF16)
    eye = (c_col == c_row).astype(F32)
    lane = lax.broadcasted_iota(jnp.int32, (1, RW_PAIR), 1)
    hmask = [(lane < RW_N).astype(F32), (lane >= RW_N).astype(F32)]
    p_row = lax.broadcasted_iota(jnp.int32, (RW_PAIR, RW_PAIR), 0)
    p_col = lax.broadcasted_iota(jnp.int32, (RW_PAIR, RW_PAIR), 1)
    bd = ((p_row < RW_N) == (p_col < RW_N)).astype(F32)

    n_par = min(RW_PAR_CHUNKS, n_chunks)
    items = [(cc, pi) for cc in range(n_par) for pi in range(RW_NPAIR)]
    heads = [(it, h) for it in range(len(items)) for h in range(2)]

    def group(gi, carry):
        base = gi * (n_par * CHUNK)

        def ld(ref):
            return [ref[pl.ds(pl.multiple_of(base + cc * CHUNK, CHUNK), CHUNK),
                        pi * RW_PAIR:(pi + 1) * RW_PAIR] for cc, pi in items]

        rr, kc, vc, lwc, avc, bvc = ld(r_s), ld(k_s), ld(v_s), ld(lw_s), ld(av_s), ld(bv_s)
        st0 = [st_ref[pi] for pi in range(RW_NPAIR)]
        gcum = [_dot_split(tri_b, x) for x in lwc]
        g_last = [g[CHUNK - 1:CHUNK, :] for g in gcum]
        e_neg = [jnp.exp(-g) for g in gcum]
        e_end = [jnp.exp(gl - g) for gl, g in zip(g_last, gcum)]
        r_t = [x * jnp.exp(g) for x, g in zip(rr, gcum)]
        a_t = [x * jnp.exp(g - lw) for x, g, lw in zip(avc, gcum, lwc)]
        b_t = [x * e for x, e in zip(bvc, e_neg)]
        k_t = [x * e for x, e in zip(kc, e_neg)]
        b_bar = [x * e for x, e in zip(bvc, e_end)]
        k_bar = [x * e for x, e in zip(kc, e_end)]

        a_m = [a_t[it] * hmask[h] for it, h in heads]
        v_m = [vc[it] * hmask[h] for it, h in heads]
        lhs = [jnp.concatenate([a_m[n], r_t[it] * hmask[h]], axis=0) for n, (it, h) in enumerate(heads)]
        gb = [_dot_nt(lhs[n], b_t[it]) for n, (it, h) in enumerate(heads)]
        gk = [_dot_nt(lhs[n], k_t[it]) for n, (it, h) in enumerate(heads)]
        l_ab = [jnp.where(tri_strict, x[:CHUNK], 0.0) for x in gb]
        l_ak = [jnp.where(tri_strict, x[:CHUNK], 0.0) for x in gk]
        m_rb = [jnp.where(tri, x[CHUNK:], 0.0) for x in gb]
        m_rk = [jnp.where(tri, x[CHUNK:], 0.0) for x in gk]
        t_inv = [eye + x for x in l_ab]
        pw = l_ab
        n = 2
        while n < CHUNK:
            pw = [_bdot(x, x) for x in pw]
            t_inv = [t + _bdot(t, x) for t, x in zip(t_inv, pw)]
            n *= 2
        lv = [_bdot(x, v) for x, v in zip(l_ak, v_m)]
        x_h = [_bdot(t, jnp.concatenate([am, l], axis=1)) for t, am, l in zip(t_inv, a_m, lv)]
        ry = [_bdot(mr, x) for mr, x in zip(m_rb, x_h)]
        mv = [_bdot(mr, v) for mr, v in zip(m_rk, v_m)]

        a_hat, u_hat, r_eff, y0 = [], [], [], []
        for it in range(len(items)):
            x0, x1, ry0, ry1 = x_h[2 * it], x_h[2 * it + 1], ry[2 * it], ry[2 * it + 1]
            a_hat.append(x0[:, :RW_PAIR] + x1[:, :RW_PAIR])
            u_hat.append(x0[:, RW_PAIR:] + x1[:, RW_PAIR:])
            r_eff.append(r_t[it] + ry0[:, :RW_PAIR] + ry1[:, :RW_PAIR])
            y0.append(ry0[:, RW_PAIR:] + ry1[:, RW_PAIR:] + mv[2 * it] + mv[2 * it + 1])
        p_t = [_dot_tn(a, b) * bd for a, b in zip(a_hat, b_bar)]
        q_t = [_dot_tn(jnp.concatenate([u, v], axis=0), jnp.concatenate([b, k], axis=0)) * bd
               for u, v, b, k in zip(u_hat, vc, b_bar, k_bar)]
        w_end = [jnp.exp(g) for g in g_last]

        st = list(st0)
        ys = []
        for it, (cc, pi) in enumerate(items):
            stb = st[pi].astype(BF16)
            ys.append(_dot_nt(r_eff[it], stb) + y0[it])
            st[pi] = st[pi] * w_end[it] + _bdot(stb, p_t[it]) + q_t[it]
        for it, (cc, pi) in enumerate(items):
            y_s[pl.ds(pl.multiple_of(base + cc * CHUNK, CHUNK), CHUNK),
                pi * RW_PAIR:(pi + 1) * RW_PAIR] = ys[it]
        for pi in range(RW_NPAIR):
            st_ref[pi] = st[pi]
        return carry

    lax.fori_loop(0, n_chunks // n_par, group, 0)

    y = y_s[...]
    inv_n = 1.0 / RW_N
    mu_y = seg_sum(y) * inv_n
    d = y - mu_y
    var = seg_sum(d * d) * inv_n
    yn = d * lax.rsqrt(var + RW_GN_EPS) * ln_g + ln_b
    o_ref[...] = ((yn + bonus_s[...]) * g_s[...]).astype(o_ref.dtype)


def _dot_split_rhs(x, m01):
    hi = x.astype(BF16)
    lo = (x - hi.astype(F32)).astype(BF16)
    return (jnp.dot(hi, m01, preferred_element_type=F32)
            + jnp.dot(lo, m01, preferred_element_type=F32))


def _rwkv(p_rw, mu, vecs, w2p, a2p, g2p, tt):
    bsz, seq, _ = p_rw.shape
    tt = min(tt, seq)
    seg = (np.arange(LANE)[:, None] // RW_N == np.arange(LANE)[None, :] // RW_N)
    seg = jnp.asarray(seg, BF16)
    row_blocks = tt // 8
    scratch = ([pltpu.VMEM((RW_NPAIR, RW_PAIR, RW_PAIR), F32)]
               + [pltpu.VMEM((tt, RW_W), F32) for _ in range(9)])
    return pl.pallas_call(
        functools.partial(_rwkv_kernel, n_chunks=tt // CHUNK),
        grid=(bsz, seq // tt),
        in_specs=[pl.BlockSpec((None, tt, RW_COLS), lambda b, t: (b, t, 0)),
                  pl.BlockSpec((None, 8, RW_COLS),
                               lambda b, t: (b, jnp.maximum(t * row_blocks - 1, 0), 0)),
                  pl.BlockSpec((1, RW_COLS), lambda b, t: (0, 0)),
                  pl.BlockSpec((8, RW_W), lambda b, t: (0, 0)),
                  pl.BlockSpec((LANE, RW_W), lambda b, t: (0, 0)),
                  pl.BlockSpec((LANE, RW_W), lambda b, t: (0, 0)),
                  pl.BlockSpec((2 * LANE, RW_W), lambda b, t: (0, 0)),
                  pl.BlockSpec((LANE, LANE), lambda b, t: (0, 0))],
        out_specs=pl.BlockSpec((None, tt, RW_W), lambda b, t: (b, t, 0)),
        out_shape=jax.ShapeDtypeStruct((bsz, seq, RW_W), BF16),
        scratch_shapes=scratch,
        compiler_params=_cparams(("parallel", "arbitrary")),
    )(p_rw, p_rw, mu, vecs, w2p, a2p, g2p, seg)


MLA_HD = LANE
MLA_ROPE_OFF = MLA_NOPE
ROLL_ROPE = LANE - MLA_ROPE


def _mla_proj_kernel(x_ref, pos_ref, freq_ref, sgn_ref, win_ref, qg_ref, wq_ref, kvg_ref, wk_ref,
                     wv_ref, q_ref, k_ref, v_ref):
    xb = x_ref[...].astype(BF16)
    p = jnp.dot(xb, win_ref[...], preferred_element_type=F32)
    cq = p[:, :MLA_Q_RANK]
    ckv = p[:, MLA_Q_RANK:MLA_Q_RANK + MLA_KV_RANK]
    kpe = p[:, MLA_Q_RANK + MLA_KV_RANK:]
    cqn = cq * lax.rsqrt(jnp.mean(cq * cq, axis=-1, keepdims=True) + RMS_EPS) * qg_ref[...]
    ckvn = ckv * lax.rsqrt(jnp.mean(ckv * ckv, axis=-1, keepdims=True) + RMS_EPS) * kvg_ref[...]
    cqb = cqn.astype(BF16)
    ckvb = ckvn.astype(BF16)

    ang = pos_ref[...].astype(F32) * freq_ref[...]
    lane = lax.broadcasted_iota(jnp.int32, (1, LANE), 1)
    is_rope = (lane >= MLA_ROPE_OFF) & (lane < MLA_ROPE_OFF + MLA_ROPE)
    cos_t = jnp.where(is_rope, jnp.cos(ang), 0.0)
    sin_t = jnp.sin(ang) * sgn_ref[...]
    scale = (MLA_NOPE + MLA_ROPE) ** -0.5
    q_c = jnp.where(lane < MLA_NOPE, 1.0, cos_t) * scale
    q_s = sin_t * scale

    k_rot = kpe * cos_t + pltpu.roll(kpe, ROLL_ROPE, 1) * sin_t
    ones_cols = (lane >= MLA_V).astype(F32)
    pair = lambda t: jnp.concatenate([t, t], axis=1)
    q_c2, q_s2, k_rot2, ones2 = pair(q_c), pair(q_s), pair(k_rot), pair(ones_cols)
    for hp in range(MLA_HEADS // 2):
        cols = slice(2 * hp * MLA_HD, 2 * (hp + 1) * MLA_HD)
        qh = jnp.dot(cqb, wq_ref[:, cols], preferred_element_type=F32)
        q_ref[:, cols] = (qh * q_c2 + pltpu.roll(qh, 2 * MLA_HD - MLA_ROPE, 1) * q_s2).astype(q_ref.dtype)
        kh = jnp.dot(ckvb, wk_ref[:, cols], preferred_element_type=F32)
        k_ref[:, cols] = (kh + k_rot2).astype(k_ref.dtype)
        vh = jnp.dot(ckvb, wv_ref[:, cols], preferred_element_type=F32)
        v_ref[:, cols] = (vh + ones2).astype(v_ref.dtype)


def _mla_proj(x2d, pos2d, freq, sgn, w_in, q_g, w_q, kv_g, w_k, w_v, tm):
    m, d = x2d.shape
    tm = min(tm, m)
    nq = MLA_HEADS * MLA_HD
    const = lambda shape: pl.BlockSpec(shape, lambda i: (0,) * len(shape))
    return pl.pallas_call(
        _mla_proj_kernel,
        grid=(m // tm,),
        in_specs=[pl.BlockSpec((tm, d), lambda i: (i, 0)),
                  pl.BlockSpec((tm, 1), lambda i: (i, 0)),
                  const((1, LANE)), const((1, LANE)),
                  const(w_in.shape), const((1, MLA_Q_RANK)), const(w_q.shape),
                  const((1, MLA_KV_RANK)), const(w_k.shape), const(w_v.shape)],
        out_specs=[pl.BlockSpec((tm, nq), lambda i: (i, 0)) for _ in range(3)],
        out_shape=[jax.ShapeDtypeStruct((m, nq), BF16) for _ in range(3)],
        name="mla_proj",
        compiler_params=_cparams(("parallel",)),
    )(x2d, pos2d, freq, sgn, w_in, q_g.reshape(1, -1), w_q, kv_g.reshape(1, -1), w_k, w_v)


def _attn_kernel(q_ref, k_ref, v_ref, o_ref, *, tq, tk, n_heads):
    i = pl.program_id(2)
    q0 = i * tq
    n_full = q0 // tk
    q_chunk = (lax.broadcasted_iota(jnp.int32, (tq, tk), 0) + q0) // CHUNK
    k_col = lax.broadcasted_iota(jnp.int32, (tq, tk), 1)
    qhs = [q_ref[:, h * MLA_HD:(h + 1) * MLA_HD] for h in range(n_heads)]

    def step(j, carry, masked):
        k0 = pl.multiple_of(j * tk, tk)
        if masked:
            ok = ((k_col + k0) // CHUNK) <= q_chunk
        new = []
        for h in range(n_heads):
            m_run, acc = carry[h]
            cols = slice(h * MLA_HD, (h + 1) * MLA_HD)
            s = lax.dot_general(qhs[h], k_ref[pl.ds(k0, tk), cols], (((1,), (1,)), ((), ())),
                                preferred_element_type=F32)
            if masked:
                s = jnp.where(ok, s, -jnp.inf)
            m_new = jnp.maximum(m_run, jnp.max(s, axis=1, keepdims=True))
            alpha = jnp.exp(m_run - m_new)
            pr = jnp.exp(s - m_new).astype(BF16)
            acc = acc * alpha + jnp.dot(pr, v_ref[pl.ds(k0, tk), cols],
                                        preferred_element_type=F32)
            new.append((m_new, acc))
        return tuple(new)

    init = tuple((jnp.full((tq, 1), -jnp.inf, F32), jnp.zeros((tq, MLA_HD), F32))
                 for _ in range(n_heads))
    carry = lax.fori_loop(0, n_full, functools.partial(step, masked=False), init)
    carry = step(n_full, carry, True)
    lane = lax.broadcasted_iota(jnp.int32, (1, MLA_HD), 1)
    for hp in range(n_heads // 2):
        acc0, acc1 = carry[2 * hp][1], carry[2 * hp + 1][1]
        o0 = acc0 / pltpu.roll(acc0, MLA_V, 1)
        o1 = pltpu.roll(acc1, MLA_V, 1) / acc1
        o_ref[:, hp * MLA_HD:(hp + 1) * MLA_HD] = jnp.where(lane < MLA_V, o0, o1).astype(o_ref.dtype)


def _attention(q, k, v, tq, tk, n_heads):
    bsz, seq, nq = q.shape
    tq = min(tq, seq)
    tk = min(tk, seq)
    width = n_heads * MLA_HD
    return pl.pallas_call(
        functools.partial(_attn_kernel, tq=tq, tk=tk, n_heads=n_heads),
        grid=(bsz, MLA_HEADS // n_heads, seq // tq),
        in_specs=[pl.BlockSpec((None, tq, width), lambda b, h, i: (b, i, h)),
                  pl.BlockSpec((None, seq, width), lambda b, h, i: (b, 0, h)),
                  pl.BlockSpec((None, seq, width), lambda b, h, i: (b, 0, h))],
        out_specs=pl.BlockSpec((None, tq, n_heads * MLA_V), lambda b, h, i: (b, i, h)),
        out_shape=jax.ShapeDtypeStruct((bsz, seq, MLA_HEADS * MLA_V), BF16),
        name="mla_attention",
        compiler_params=_cparams(("parallel", "parallel", "arbitrary")),
    )(q, k, v)


def _pad_cols(w, width):
    return jnp.pad(w, ((0, 0), (0, width - w.shape[1])))


def _pad_rows(w, height):
    return jnp.pad(w, ((0, height - w.shape[0]), (0, 0)))


def _even_weights(w_in, mu):
    gla_in = 2 * GLA_QK + 2 * GLA_V + GLA_GATE_RANK
    w_gla, w_rw = w_in[:, :gla_in], w_in[:, gla_in:]
    o = 3 * RW_W
    rw_parts = [w_rw[:, :o],
                _pad_cols(w_rw[:, o:o + RW_DECAY_RANK], LANE),
                _pad_cols(w_rw[:, o + RW_DECAY_RANK:o + RW_DECAY_RANK + RW_A_RANK], LANE),
                _pad_cols(w_rw[:, o + RW_DECAY_RANK + RW_A_RANK:], 2 * LANE)]
    mu_parts = [mu[:o],
                jnp.pad(mu[o:o + RW_DECAY_RANK], (0, LANE - RW_DECAY_RANK)),
                jnp.pad(mu[o + RW_DECAY_RANK:o + RW_DECAY_RANK + RW_A_RANK], (0, LANE - RW_A_RANK)),
                jnp.pad(mu[o + RW_DECAY_RANK + RW_A_RANK:], (0, 2 * LANE - RW_GATE_RANK))]
    w_all = jnp.concatenate(rw_parts + [_pad_cols(w_gla, GLA_COLS)], axis=1).astype(BF16)
    return w_all, jnp.concatenate(mu_parts).reshape(1, RW_COLS)


def _mla_weights(w_in, w_q_b, w_kv_b):
    swap = np.arange(MLA_ROPE) ^ 1
    d = w_in.shape[0]
    kpe = w_in[:, MLA_Q_RANK + MLA_KV_RANK:]
    w_in_p = jnp.concatenate([w_in[:, :MLA_Q_RANK + MLA_KV_RANK],
                              jnp.zeros((d, MLA_NOPE), F32), kpe, kpe[:, swap]], axis=1).astype(BF16)
    wq = w_q_b.reshape(MLA_Q_RANK, MLA_HEADS, MLA_NOPE + MLA_ROPE)
    wq = jnp.concatenate([wq, wq[:, :, MLA_NOPE:][:, :, swap]], axis=-1)
    wq = wq.reshape(MLA_Q_RANK, MLA_HEADS * MLA_HD).astype(BF16)
    wkv = w_kv_b.reshape(MLA_KV_RANK, MLA_HEADS, MLA_NOPE + MLA_V)
    wk = jnp.pad(wkv[:, :, :MLA_NOPE], ((0, 0), (0, 0), (0, MLA_HD - MLA_NOPE)))
    wk = wk.reshape(MLA_KV_RANK, MLA_HEADS * MLA_HD).astype(BF16)
    wv = jnp.pad(wkv[:, :, MLA_NOPE:], ((0, 0), (0, 0), (0, MLA_HD - MLA_V)))
    wv = wv.reshape(MLA_KV_RANK, MLA_HEADS * MLA_HD).astype(BF16)
    return w_in_p, wq, wk, wv


def _rope_tables():
    inv_freq = ROPE_THETA ** (-jnp.arange(0, MLA_ROPE, 2, dtype=jnp.float32) / MLA_ROPE)
    d = np.arange(MLA_ROPE)
    freq = jnp.zeros((LANE,), F32).at[MLA_ROPE_OFF:MLA_ROPE_OFF + MLA_ROPE].set(inv_freq[d // 2])
    sgn = np.zeros((LANE,), np.float32)
    sgn[MLA_ROPE_OFF:MLA_ROPE_OFF + MLA_ROPE] = np.where(d % 2 == 0, -1.0, 1.0)
    return freq.reshape(1, LANE), jnp.asarray(sgn).reshape(1, LANE)


def kernel(x, positions, even_w_in, gla_gate_w2, gla_gate_b, gla_norm_g, rwkv_mu, rwkv_w0, rwkv_w2,
           rwkv_a0, rwkv_a2, rwkv_g2, rwkv_k_k, rwkv_k_a, rwkv_r_k, rwkv_ln_g, rwkv_ln_b, even_w_out,
           mla_w_in, mla_q_norm_g, mla_w_q_b, mla_kv_norm_g, mla_w_kv_b, mla_w_out,
           ffn_w_gate_up, ffn_w_down, ln_g, ln_b):
    bsz, seq, d = x.shape
    m = bsz * seq
    x2d = x.reshape(m, d)
    wgu = ffn_w_gate_up.astype(BF16)
    wdn = ffn_w_down.astype(BF16)

    w_all, mu_p = _even_weights(even_w_in[0], rwkv_mu[0])
    p_rw, p_gla = _inproj(x2d, w_all, (RW_COLS, GLA_COLS), F32, tm=256)
    o_a = _gla(p_gla.reshape(bsz, seq, GLA_COLS),
               _pad_rows(gla_gate_w2[0], LANE).astype(BF16), gla_gate_b[0], gla_norm_g[0], tt=GLA_TT)
    vecs = jnp.stack([rwkv_w0[0], rwkv_a0[0], rwkv_k_k[0], rwkv_k_a[0], rwkv_r_k[0].reshape(RW_W),
                      rwkv_ln_g[0], rwkv_ln_b[0], jnp.zeros((RW_W,), F32)])
    o_b = _rwkv(p_rw.reshape(bsz, seq, RW_COLS), mu_p, vecs,
                _pad_rows(rwkv_w2[0], LANE).astype(BF16), _pad_rows(rwkv_a2[0], LANE).astype(BF16),
                _pad_rows(rwkv_g2[0], 2 * LANE).astype(BF16), tt=RW_TT)
    w_out = even_w_out[0].astype(BF16)
    x2d = _outproj_ln([o_a.reshape(m, GLA_V), o_b.reshape(m, RW_W)], [w_out[:GLA_V], w_out[GLA_V:]],
                      x2d, ln_g[0, 0], ln_b[0, 0], tm=512)
    x2d = _ffn_ln(x2d, wgu[0], wdn[0], ln_g[0, 1], ln_b[0, 1], tm=1024, th=256)

    w_in_p, wq, wk, wv = _mla_weights(mla_w_in[0], mla_w_q_b[0], mla_w_kv_b[0])
    freq, sgn = _rope_tables()
    q, k, v = _mla_proj(x2d, positions.reshape(m, 1), freq, sgn, w_in_p, mla_q_norm_g[0], wq,
                        mla_kv_norm_g[0], wk, wv, tm=256)
    nq = MLA_HEADS * MLA_HD
    o = _attention(q.reshape(bsz, seq, nq), k.reshape(bsz, seq, nq), v.reshape(bsz, seq, nq),
                   tq=ATT_TQ, tk=ATT_TK, n_heads=ATT_HEADS)
    x2d = _outproj_ln([o.reshape(m, MLA_HEADS * MLA_V)], [mla_w_out[0].astype(BF16)],
                      x2d, ln_g[1, 0], ln_b[1, 0], tm=512)
    x2d = _ffn_ln(x2d, wgu[1], wdn[1], ln_g[1, 1], ln_b[1, 1], tm=1024, th=256)
    return x2d.reshape(bsz, seq, d)
```

```python
import functools
import math

import jax
import jax.numpy as jnp
import numpy as np
from jax import lax
from jax.experimental import pallas as pl
from jax.experimental.pallas import tpu as pltpu

F32 = jnp.float32
BF16 = jnp.bfloat16

DEPTH = 2
CHUNK = 64
DN_ALPHA = (2.0 * DEPTH) ** 0.25
LN_EPS = 1e-5
RMS_EPS = 1e-6

GLA_HEADS = 4
GLA_DK = 64
GLA_DV = 128
GLA_GATE_RANK = 16
GLA_TAU = 16.0
GLA_QK = GLA_HEADS * GLA_DK
GLA_V = GLA_HEADS * GLA_DV
GLA_SUB = 16
GLA_EXP_CLAMP = 60.0

RW_HEADS = 8
RW_N = 64
RW_W = RW_HEADS * RW_N
RW_DECAY_RANK = 64
RW_A_RANK = 64
RW_GATE_RANK = 160
RW_GN_EPS = 64e-5

MLA_HEADS = 16
MLA_NOPE = 64
MLA_ROPE = 32
MLA_V = 64
MLA_Q_RANK = 768
MLA_KV_RANK = 256
ROPE_THETA = 10000.0

LANE = 128
VMEM_LIMIT = 48 * 1024 * 1024
ATT_TQ, ATT_HEADS = 1024, 2
ATT_SUB = 256
ATT_VROWS = 80

RW_COLS = 3 * RW_W + LANE + LANE + 2 * LANE
GLA_COLS = 2 * GLA_QK + 2 * GLA_V + LANE


def _cparams(sem):
    return pltpu.CompilerParams(dimension_semantics=sem, vmem_limit_bytes=VMEM_LIMIT)


def _bdot(a, b):
    return jnp.dot(a.astype(BF16), b.astype(BF16), preferred_element_type=F32)


def _dot_nt(a, b):
    return lax.dot_general(a.astype(BF16), b.astype(BF16), (((1,), (1,)), ((), ())),
                           preferred_element_type=F32)


def _dot_tn(a, b):
    return lax.dot_general(a.astype(BF16), b.astype(BF16), (((0,), (0,)), ((), ())),
                           preferred_element_type=F32)


def _dot_split(m01, x):
    hi = x.astype(BF16)
    lo = (x - hi.astype(F32)).astype(BF16)
    m = m01.astype(BF16)
    return (jnp.dot(m, hi, preferred_element_type=F32)
            + jnp.dot(m, lo, preferred_element_type=F32))


def _softplus(x):
    return jnp.maximum(x, 0.0) + jnp.log(1.0 + jnp.exp(-jnp.abs(x)))


def _layer_norm(y, g, b):
    mu = jnp.mean(y, axis=-1, keepdims=True)
    d = y - mu
    var = jnp.mean(d * d, axis=-1, keepdims=True)
    return d * lax.rsqrt(var + LN_EPS) * g + b


def _inproj_kernel(x_ref, w_ref, *o_refs, widths, tn):
    xb = x_ref[...].astype(BF16)
    off = 0
    for o_ref, width in zip(o_refs, widths):
        for j in range(0, width, tn):
            w = min(tn, width - j)
            o_ref[:, j:j + w] = jnp.dot(xb, w_ref[:, off + j:off + j + w],
                                        preferred_element_type=F32).astype(o_ref.dtype)
        off += width


def _inproj(x2d, w_bf16, widths, out_dtype, tm):
    m, k = x2d.shape
    tm = min(tm, m)
    n = sum(widths)
    return pl.pallas_call(
        functools.partial(_inproj_kernel, widths=tuple(widths), tn=512),
        grid=(m // tm,),
        in_specs=[pl.BlockSpec((tm, k), lambda i: (i, 0)),
                  pl.BlockSpec((k, n), lambda i: (0, 0))],
        out_specs=[pl.BlockSpec((tm, wd), lambda i: (i, 0)) for wd in widths],
        out_shape=[jax.ShapeDtypeStruct((m, wd), out_dtype) for wd in widths],
        compiler_params=_cparams(("parallel",)),
    )(x2d, w_bf16)


def _outproj_ln_kernel(*refs, n_in):
    a_refs = refs[:n_in]
    w_refs = refs[n_in:2 * n_in]
    x_ref, g_ref, b_ref, o_ref = refs[2 * n_in:]
    y = DN_ALPHA * x_ref[...]
    for a_ref, w_ref in zip(a_refs, w_refs):
        y = y + jnp.dot(a_ref[...], w_ref[...], preferred_element_type=F32)
    o_ref[...] = _layer_norm(y, g_ref[...], b_ref[...])


def _outproj_ln(acts, weights, x2d, g, b, tm):
    m, d = x2d.shape
    tm = min(tm, m)
    n_in = len(acts)
    in_specs = ([pl.BlockSpec((tm, a.shape[1]), lambda i: (i, 0)) for a in acts]
                + [pl.BlockSpec(w.shape, lambda i: (0, 0)) for w in weights]
                + [pl.BlockSpec((tm, d), lambda i: (i, 0)),
                   pl.BlockSpec((1, d), lambda i: (0, 0)),
                   pl.BlockSpec((1, d), lambda i: (0, 0))])
    return pl.pallas_call(
        functools.partial(_outproj_ln_kernel, n_in=n_in),
        grid=(m // tm,),
        in_specs=in_specs,
        out_specs=pl.BlockSpec((tm, d), lambda i: (i, 0)),
        out_shape=jax.ShapeDtypeStruct((m, d), F32),
        compiler_params=_cparams(("parallel",)),
    )(*acts, *weights, x2d, g.reshape(1, d), b.reshape(1, d))


def _ffn_kernel(x_ref, wg_ref, wu_ref, wd_ref, g_ref, b_ref, o_ref, xb_ref, acc_ref):
    j = pl.program_id(1)

    @pl.when(j == 0)
    def _():
        xb_ref[...] = x_ref[...].astype(BF16)
        acc_ref[...] = jnp.zeros_like(acc_ref)

    xb = xb_ref[...]
    gate = jnp.dot(xb, wg_ref[...], preferred_element_type=F32)
    up = jnp.dot(xb, wu_ref[...], preferred_element_type=F32)
    h = (gate * jax.nn.sigmoid(gate) * up).astype(BF16)
    acc_ref[...] += jnp.dot(h, wd_ref[...], preferred_element_type=F32)

    @pl.when(j == pl.num_programs(1) - 1)
    def _():
        y = DN_ALPHA * x_ref[...] + acc_ref[...]
        o_ref[...] = _layer_norm(y, g_ref[...], b_ref[...])


def _ffn_ln(x2d, w_gate_up, w_down, g, b, tm, th):
    m, d = x2d.shape
    hidden = w_down.shape[0]
    tm = min(tm, m)
    nh = hidden // th
    return pl.pallas_call(
        _ffn_kernel,
        grid=(m // tm, nh),
        in_specs=[pl.BlockSpec((tm, d), lambda i, j: (i, 0)),
                  pl.BlockSpec((d, th), lambda i, j: (0, j)),
                  pl.BlockSpec((d, th), lambda i, j: (0, j + nh)),
                  pl.BlockSpec((th, d), lambda i, j: (j, 0)),
                  pl.BlockSpec((1, d), lambda i, j: (0, 0)),
                  pl.BlockSpec((1, d), lambda i, j: (0, 0))],
        out_specs=pl.BlockSpec((tm, d), lambda i, j: (i, 0)),
        out_shape=jax.ShapeDtypeStruct((m, d), F32),
        scratch_shapes=[pltpu.VMEM((tm, d), BF16), pltpu.VMEM((tm, d), F32)],
        compiler_params=_cparams(("parallel", "arbitrary")),
    )(x2d, w_gate_up, w_gate_up, w_down, g.reshape(1, d), b.reshape(1, d))


def _gla_kernel(p_ref, w2_ref, gb_ref, ng_ref, o_ref, st_ref, *, n_chunks):
    @pl.when(pl.program_id(1) == 0)
    def _():
        st_ref[...] = jnp.zeros_like(st_ref)

    c_row = lax.broadcasted_iota(jnp.int32, (CHUNK, CHUNK), 0)
    c_col = lax.broadcasted_iota(jnp.int32, (CHUNK, CHUNK), 1)
    tri = (c_col <= c_row)
    tri_loc = tri & (c_col >= (c_row // GLA_SUB) * GLA_SUB)
    cum_mat = jnp.concatenate([tri, tri_loc], axis=0).astype(BF16)
    lane = lax.broadcasted_iota(jnp.int32, (1, GLA_QK), 1)
    head_masks = [(lane // GLA_DK == h).astype(F32) for h in range(GLA_HEADS)]
    n_sub = CHUNK // GLA_SUB

    n_par = min(GLA_PAR_CHUNKS, n_chunks)
    o_v = 2 * GLA_QK
    o_g = o_v + GLA_V
    o_l = o_g + GLA_V

    def group(gi, carry):
        base = gi * (n_par * CHUNK)
        rows = [pl.ds(pl.multiple_of(base + cc * CHUNK, CHUNK), CHUNK) for cc in range(n_par)]
        q = [p_ref[r, 0:GLA_QK] * (GLA_DK ** -0.5) for r in rows]
        k = [p_ref[r, GLA_QK:o_v] for r in rows]
        vb = [p_ref[r, o_v:o_g].astype(BF16) for r in rows]
        gg = [p_ref[r, o_g:o_l] for r in rows]
        z = [_bdot(p_ref[r, o_l:GLA_COLS], w2_ref[...]) + gb_ref[...] for r in rows]
        log_a = [-_softplus(-x) / GLA_TAU for x in z]
        cums = [_dot_split(cum_mat, x) for x in log_a]
        b = [x[:CHUNK] for x in cums]
        b_last = [x[CHUNK - 1:CHUNK, :] for x in b]
        q_loc = [x * jnp.exp(c[CHUNK:]) for x, c in zip(q, cums)]
        a_blocks = []
        for cc in range(n_par):
            blocks = []
            for i in range(n_sub):
                ref_i = (jnp.zeros((1, GLA_QK), F32) if i == 0
                         else b[cc][i * GLA_SUB - 1:i * GLA_SUB, :])
                k_i = k[cc] * jnp.exp(jnp.minimum(ref_i - b[cc], GLA_EXP_CLAMP))
                q_i = q_loc[cc][i * GLA_SUB:(i + 1) * GLA_SUB, :]
                lhs = jnp.concatenate([q_i * m for m in head_masks], axis=0)
                blocks.append(_dot_nt(lhs, k_i))
            a_blocks.append(blocks)
        o_intra = []
        for cc in range(n_par):
            per_head = []
            for h in range(GLA_HEADS):
                a_h = jnp.concatenate(
                    [blk[h * GLA_SUB:(h + 1) * GLA_SUB, :] for blk in a_blocks[cc]], axis=0)
                a_h = jnp.where(tri, a_h, 0.0)
                per_head.append(_bdot(a_h, vb[cc][:, h * GLA_DV:(h + 1) * GLA_DV]))
            o_intra.append(per_head)
        lhs_inter = [jnp.concatenate([x * jnp.exp(bb) * m for m in head_masks], axis=0)
                     for x, bb in zip(q, b)]
        kv = [_dot_tn(v, x * jnp.exp(bl - bb)) for v, x, bl, bb in zip(vb, k, b_last, b)]
        kv = [sum(x[h * GLA_DV:(h + 1) * GLA_DV, :] * head_masks[h] for h in range(GLA_HEADS)) for x in kv]
        decay = [jnp.exp(x) for x in b_last]

        st = st_ref[...]
        outs = []
        for cc in range(n_par):
            o_inter = _dot_nt(lhs_inter[cc], st)
            st = st * decay[cc] + kv[cc]
            for h in range(GLA_HEADS):
                o_h = o_inter[h * CHUNK:(h + 1) * CHUNK, :] + o_intra[cc][h]
                ms = jnp.mean(o_h * o_h, axis=-1, keepdims=True)
                o_h = o_h * lax.rsqrt(ms + RMS_EPS) * ng_ref[...]
                g_h = gg[cc][:, h * GLA_DV:(h + 1) * GLA_DV]
                outs.append((o_h * (g_h * jax.nn.sigmoid(g_h))).astype(o_ref.dtype))
        for cc in range(n_par):
            for h in range(GLA_HEADS):
                o_ref[rows[cc], h * GLA_DV:(h + 1) * GLA_DV] = outs[cc * GLA_HEADS + h]
        st_ref[...] = st
        return carry

    lax.fori_loop(0, n_chunks // n_par, group, 0)


def _gla(p_gla, w2p, gate_b, norm_g, tt):
    bsz, seq, _ = p_gla.shape
    tt = min(tt, seq)
    return pl.pallas_call(
        functools.partial(_gla_kernel, n_chunks=tt // CHUNK),
        grid=(bsz, seq // tt),
        in_specs=[pl.BlockSpec((None, tt, GLA_COLS), lambda b, t: (b, t, 0)),
                  pl.BlockSpec((LANE, GLA_QK), lambda b, t: (0, 0)),
                  pl.BlockSpec((1, GLA_QK), lambda b, t: (0, 0)),
                  pl.BlockSpec((1, GLA_DV), lambda b, t: (0, 0))],
        out_specs=pl.BlockSpec((None, tt, GLA_V), lambda b, t: (b, t, 0)),
        out_shape=jax.ShapeDtypeStruct((bsz, seq, GLA_V), BF16),
        scratch_shapes=[pltpu.VMEM((GLA_DV, GLA_QK), F32)],
        compiler_params=_cparams(("parallel", "arbitrary")),
    )(p_gla, w2p, gate_b.reshape(1, GLA_QK), norm_g.reshape(1, GLA_DV))


RW_PAIR = 2 * RW_N
RW_NPAIR = RW_HEADS // 2
RW_PAR_CHUNKS = 4
RW_TT = 256
GLA_TT = 512
GLA_PAR_CHUNKS = 8


def _rwkv_kernel(p_ref, prev_ref, mu_ref, vec_ref, w2_ref, a2_ref, g2_ref, seg_ref, o_ref,
                 st_ref, r_s, k_s, v_s, lw_s, av_s, bv_s, y_s, bonus_s, g_s, *, n_chunks):
    t_idx = pl.program_id(1)

    @pl.when(t_idx == 0)
    def _():
        st_ref[...] = jnp.zeros_like(st_ref)

    tt = p_ref.shape[0]
    w0, a0, k_k, k_a, r_k, ln_g, ln_b = [vec_ref[i:i + 1, :] for i in range(7)]
    seg = seg_ref[...]

    def seg_sum(x):
        return jnp.concatenate(
            [_dot_split_rhs(x[:, i * LANE:(i + 1) * LANE], seg) for i in range(RW_W // LANE)], axis=1)

    p = p_ref[...]
    rows = lax.broadcasted_iota(jnp.int32, (tt, 1), 0)
    first = jnp.where(t_idx > 0, prev_ref[7:8, :], 0.0)
    shifted = jnp.where(rows == 0, first, pltpu.roll(p, 1, 0))
    p = p + mu_ref[...] * (shifted - p)
    r = p[:, 0:RW_W]
    k = p[:, RW_W:2 * RW_W]
    v = p[:, 2 * RW_W:3 * RW_W]
    wl = p[:, 3 * RW_W:3 * RW_W + LANE]
    al = p[:, 3 * RW_W + LANE:3 * RW_W + 2 * LANE]
    gl = p[:, 3 * RW_W + 2 * LANE:RW_COLS]
    w_raw = -_softplus(-(w0 + _bdot(jnp.tanh(wl), w2_ref[...]))) - 0.5
    lw = -jnp.exp(w_raw)
    a = jax.nn.sigmoid(a0 + _bdot(al, a2_ref[...]))
    g = _bdot(jax.nn.sigmoid(gl), g2_ref[...])
    kk = k * k_k
    kk = kk * lax.rsqrt(jnp.maximum(seg_sum(kk * kk), 1e-24))
    k2 = k * (1.0 + (a - 1.0) * k_a)
    bonus_s[...] = seg_sum(r * k2 * r_k) * v
    g_s[...] = g
    r_s[...] = r
    k_s[...] = k2
    v_s[...] = v
    lw_s[...] = lw
    av_s[...] = -kk
    bv_s[...] = kk * a

    c_row = lax.broadcasted_iota(jnp.int32, (CHUNK, CHUNK), 0)
    c_col = lax.broadcasted_iota(jnp.int32, (CHUNK, CHUNK), 1)
    tri = c_col <= c_row
    tri_strict = c_col < c_row
    tri_b = tri.astype(BF16)
    eye = (c_col == c_row).astype(F32)
    lane = lax.broadcasted_iota(jnp.int32, (1, RW_PAIR), 1)
    hmask = [(lane < RW_N).astype(F32), (lane >= RW_N).astype(F32)]
    p_row = lax.broadcasted_iota(jnp.int32, (RW_PAIR, RW_PAIR), 0)
    p_col = lax.broadcasted_iota(jnp.int32, (RW_PAIR, RW_PAIR), 1)
    bd = ((p_row < RW_N) == (p_col < RW_N)).astype(F32)

    n_par = min(RW_PAR_CHUNKS, n_chunks)
    items = [(cc, pi) for cc in range(n_par) for pi in range(RW_NPAIR)]
    heads = [(it, h) for it in range(len(items)) for h in range(2)]

    def group(gi, carry):
        base = gi * (n_par * CHUNK)

        def ld(ref):
            return [ref[pl.ds(pl.multiple_of(base + cc * CHUNK, CHUNK), CHUNK),
                        pi * RW_PAIR:(pi + 1) * RW_PAIR] for cc, pi in items]

        rr, kc, vc, lwc, avc, bvc = ld(r_s), ld(k_s), ld(v_s), ld(lw_s), ld(av_s), ld(bv_s)
        st0 = [st_ref[pi] for pi in range(RW_NPAIR)]
        gcum = [_dot_split(tri_b, x) for x in lwc]
        g_last = [g[CHUNK - 1:CHUNK, :] for g in gcum]
        e_neg = [jnp.exp(-g) for g in gcum]
        e_end = [jnp.exp(gl - g) for gl, g in zip(g_last, gcum)]
        r_t = [x * jnp.exp(g) for x, g in zip(rr, gcum)]
        a_t = [x * jnp.exp(g - lw) for x, g, lw in zip(avc, gcum, lwc)]
        b_t = [x * e for x, e in zip(bvc, e_neg)]
        k_t = [x * e for x, e in zip(kc, e_neg)]
        b_bar = [x * e for x, e in zip(bvc, e_end)]
        k_bar = [x * e for x, e in zip(kc, e_end)]

        a_m = [a_t[it] * hmask[h] for it, h in heads]
        v_m = [vc[it] * hmask[h] for it, h in heads]
        lhs = [jnp.concatenate([a_m[n], r_t[it] * hmask[h]], axis=0) for n, (it, h) in enumerate(heads)]
        gb = [_dot_nt(lhs[n], b_t[it]) for n, (it, h) in enumerate(heads)]
        gk = [_dot_nt(lhs[n], k_t[it]) for n, (it, h) in enumerate(heads)]
        l_ab = [jnp.where(tri_strict, x[:CHUNK], 0.0) for x in gb]
        l_ak = [jnp.where(tri_strict, x[:CHUNK], 0.0) for x in gk]
        m_rb = [jnp.where(tri, x[CHUNK:], 0.0) for x in gb]
        m_rk = [jnp.where(tri, x[CHUNK:], 0.0) for x in gk]
        t_inv = [eye + x for x in l_ab]
        pw = l_ab
        n = 2
        while n < CHUNK:
            pw = [_bdot(x, x) for x in pw]
            t_inv = [t + _bdot(t, x) for t, x in zip(t_inv, pw)]
            n *= 2
        lv = [_bdot(x, v) for x, v in zip(l_ak, v_m)]
        x_h = [_bdot(t, jnp.concatenate([am, l], axis=1)) for t, am, l in zip(t_inv, a_m, lv)]
        ry = [_bdot(mr, x) for mr, x in zip(m_rb, x_h)]
        mv = [_bdot(mr, v) for mr, v in zip(m_rk, v_m)]

        a_hat, u_hat, r_eff, y0 = [], [], [], []
        for it in range(len(items)):
            x0, x1, ry0, ry1 = x_h[2 * it], x_h[2 * it + 1], ry[2 * it], ry[2 * it + 1]
            a_hat.append(x0[:, :RW_PAIR] + x1[:, :RW_PAIR])
            u_hat.append(x0[:, RW_PAIR:] + x1[:, RW_PAIR:])
            r_eff.append(r_t[it] + ry0[:, :RW_PAIR] + ry1[:, :RW_PAIR])
            y0.append(ry0[:, RW_PAIR:] + ry1[:, RW_PAIR:] + mv[2 * it] + mv[2 * it + 1])
        p_t = [_dot_tn(a, b) * bd for a, b in zip(a_hat, b_bar)]
        q_t = [_dot_tn(jnp.concatenate([u, v], axis=0), jnp.concatenate([b, k], axis=0)) * bd
               for u, v, b, k in zip(u_hat, vc, b_bar, k_bar)]
        w_end = [jnp.exp(g) for g in g_last]

        st = list(st0)
        ys = []
        for it, (cc, pi) in enumerate(items):
            stb = st[pi].astype(BF16)
            ys.append(_dot_nt(r_eff[it], stb) + y0[it])
            st[pi] = st[pi] * w_end[it] + _bdot(stb, p_t[it]) + q_t[it]
        for it, (cc, pi) in enumerate(items):
            y_s[pl.ds(pl.multiple_of(base + cc * CHUNK, CHUNK), CHUNK),
                pi * RW_PAIR:(pi + 1) * RW_PAIR] = ys[it]
        for pi in range(RW_NPAIR):
            st_ref[pi] = st[pi]
        return carry

    lax.fori_loop(0, n_chunks // n_par, group, 0)

    y = y_s[...]
    inv_n = 1.0 / RW_N
    mu_y = seg_sum(y) * inv_n
    d = y - mu_y
    var = seg_sum(d * d) * inv_n
    yn = d * lax.rsqrt(var + RW_GN_EPS) * ln_g + ln_b
    o_ref[...] = ((yn + bonus_s[...]) * g_s[...]).astype(o_ref.dtype)


def _dot_split_rhs(x, m01):
    hi = x.astype(BF16)
    lo = (x - hi.astype(F32)).astype(BF16)
    return (jnp.dot(hi, m01, preferred_element_type=F32)
            + jnp.dot(lo, m01, preferred_element_type=F32))


def _rwkv(p_rw, mu, vecs, w2p, a2p, g2p, tt):
    bsz, seq, _ = p_rw.shape
    tt = min(tt, seq)
    seg = (np.arange(LANE)[:, None] // RW_N == np.arange(LANE)[None, :] // RW_N)
    seg = jnp.asarray(seg, BF16)
    row_blocks = tt // 8
    scratch = ([pltpu.VMEM((RW_NPAIR, RW_PAIR, RW_PAIR), F32)]
               + [pltpu.VMEM((tt, RW_W), F32) for _ in range(9)])
    return pl.pallas_call(
        functools.partial(_rwkv_kernel, n_chunks=tt // CHUNK),
        grid=(bsz, seq // tt),
        in_specs=[pl.BlockSpec((None, tt, RW_COLS), lambda b, t: (b, t, 0)),
                  pl.BlockSpec((None, 8, RW_COLS),
                               lambda b, t: (b, jnp.maximum(t * row_blocks - 1, 0), 0)),
                  pl.BlockSpec((1, RW_COLS), lambda b, t: (0, 0)),
                  pl.BlockSpec((8, RW_W), lambda b, t: (0, 0)),
                  pl.BlockSpec((LANE, RW_W), lambda b, t: (0, 0)),
                  pl.BlockSpec((LANE, RW_W), lambda b, t: (0, 0)),
                  pl.BlockSpec((2 * LANE, RW_W), lambda b, t: (0, 0)),
                  pl.BlockSpec((LANE, LANE), lambda b, t: (0, 0))],
        out_specs=pl.BlockSpec((None, tt, RW_W), lambda b, t: (b, t, 0)),
        out_shape=jax.ShapeDtypeStruct((bsz, seq, RW_W), BF16),
        scratch_shapes=scratch,
        compiler_params=_cparams(("parallel", "arbitrary")),
    )(p_rw, p_rw, mu, vecs, w2p, a2p, g2p, seg)


MLA_HD = LANE
MLA_ROPE_OFF = MLA_NOPE
ROLL_ROPE = LANE - MLA_ROPE


def _mla_proj_kernel(x_ref, pos_ref, freq_ref, sgn_ref, win_ref, qg_ref, wq_ref, kvg_ref, wk_ref,
                     wvt_ref, q_ref, k_ref, vt_ref):
    xb = x_ref[...].astype(BF16)
    p = jnp.dot(xb, win_ref[...], preferred_element_type=F32)
    cq = p[:, :MLA_Q_RANK]
    ckv = p[:, MLA_Q_RANK:MLA_Q_RANK + MLA_KV_RANK]
    kpe = p[:, MLA_Q_RANK + MLA_KV_RANK:]
    cqn = cq * lax.rsqrt(jnp.mean(cq * cq, axis=-1, keepdims=True) + RMS_EPS) * qg_ref[...]
    ckvn = ckv * lax.rsqrt(jnp.mean(ckv * ckv, axis=-1, keepdims=True) + RMS_EPS) * kvg_ref[...]
    cqb = cqn.astype(BF16)
    ckvb = ckvn.astype(BF16)

    ang = pos_ref[...].astype(F32) * freq_ref[...]
    lane = lax.broadcasted_iota(jnp.int32, (1, LANE), 1)
    is_rope = (lane >= MLA_ROPE_OFF) & (lane < MLA_ROPE_OFF + MLA_ROPE)
    cos_t = jnp.where(is_rope, jnp.cos(ang), 0.0)
    sin_t = jnp.sin(ang) * sgn_ref[...]
    scale = (MLA_NOPE + MLA_ROPE) ** -0.5
    q_c = jnp.where(lane < MLA_NOPE, 1.0, cos_t) * scale
    q_s = sin_t * scale

    k_rot = kpe * cos_t + pltpu.roll(kpe, ROLL_ROPE, 1) * sin_t
    pair = lambda t: jnp.concatenate([t, t], axis=1)
    q_c2, q_s2, k_rot2 = pair(q_c), pair(q_s), pair(k_rot)
    for hp in range(MLA_HEADS // 2):
        cols = slice(2 * hp * MLA_HD, 2 * (hp + 1) * MLA_HD)
        qh = jnp.dot(cqb, wq_ref[:, cols], preferred_element_type=F32)
        q_ref[:, cols] = (qh * q_c2 + pltpu.roll(qh, 2 * MLA_HD - MLA_ROPE, 1) * q_s2).astype(q_ref.dtype)
        kh = jnp.dot(ckvb, wk_ref[:, cols], preferred_element_type=F32)
        k_ref[:, cols] = (kh + k_rot2).astype(k_ref.dtype)
    vt = lax.dot_general(wvt_ref[...], ckvb, (((1,), (1,)), ((), ())),
                         preferred_element_type=F32)
    vrow = lax.broadcasted_iota(jnp.int32, (vt.shape[0], 1), 0)
    vt_ref[...] = (vt + ((vrow % ATT_VROWS) >= MLA_V).astype(F32)).astype(vt_ref.dtype)


def _mla_proj(x2d, pos2d, freq, sgn, w_in, q_g, w_q, kv_g, w_k, w_vt, bsz, seq, tm):
    m, d = x2d.shape
    tm = min(tm, seq)
    nq = MLA_HEADS * MLA_HD
    nv = MLA_HEADS * ATT_VROWS
    spb = seq // tm
    const = lambda shape: pl.BlockSpec(shape, lambda i: (0,) * len(shape))
    return pl.pallas_call(
        _mla_proj_kernel,
        grid=(m // tm,),
        in_specs=[pl.BlockSpec((tm, d), lambda i: (i, 0)),
                  pl.BlockSpec((tm, 1), lambda i: (i, 0)),
                  const((1, LANE)), const((1, LANE)),
                  const(w_in.shape), const((1, MLA_Q_RANK)), const(w_q.shape),
                  const((1, MLA_KV_RANK)), const(w_k.shape), const(w_vt.shape)],
        out_specs=[pl.BlockSpec((tm, nq), lambda i: (i, 0)),
                   pl.BlockSpec((tm, nq), lambda i: (i, 0)),
                   pl.BlockSpec((None, nv, tm), lambda i: (i // spb, 0, i % spb))],
        out_shape=[jax.ShapeDtypeStruct((m, nq), BF16),
                   jax.ShapeDtypeStruct((m, nq), BF16),
                   jax.ShapeDtypeStruct((bsz, nv, seq), BF16)],
        name="mla_proj",
        compiler_params=_cparams(("parallel",)),
    )(x2d, pos2d, freq, sgn, w_in, q_g.reshape(1, -1), w_q, kv_g.reshape(1, -1), w_k, w_vt)


def _attn_kernel(q_ref, k_ref, vt_ref, o_ref, *, tq, n_heads):
    i = pl.program_id(2)
    q0 = i * tq
    nsub = tq // ATT_SUB
    qhs = [q_ref[:, h * MLA_HD:(h + 1) * MLA_HD] for h in range(n_heads)]
    kr = lax.broadcasted_iota(jnp.int32, (ATT_SUB, ATT_SUB), 0) // CHUNK
    qc = lax.broadcasted_iota(jnp.int32, (ATT_SUB, ATT_SUB), 1) // CHUNK
    diag_ok = kr <= qc

    def scores(k0, h, q_rows):
        return lax.dot_general(k_ref[pl.ds(k0, ATT_SUB), h * MLA_HD:(h + 1) * MLA_HD], q_rows,
                               (((1,), (1,)), ((), ())), preferred_element_type=F32)

    def update(h, k0, m_run, acc, s):
        m_new = jnp.maximum(m_run, jnp.max(s, axis=0, keepdims=True))
        alpha = jnp.exp(m_run - m_new)
        pr = jnp.exp(s - m_new).astype(BF16)
        vt = vt_ref[h * ATT_VROWS:(h + 1) * ATT_VROWS, pl.ds(k0, ATT_SUB)]
        return m_new, acc * alpha + jnp.dot(vt, pr, preferred_element_type=F32)

    def full_tile(j, carry):
        k0 = pl.multiple_of(j * tq, tq)
        ss = [[scores(k0 + u * ATT_SUB, h, qhs[h]) for h in range(n_heads)] for u in range(nsub)]
        cur = list(carry)
        for u in range(nsub):
            for h in range(n_heads):
                cur[h] = update(h, k0 + u * ATT_SUB, *cur[h], ss[u][h])
        return tuple(cur)

    init = tuple((jnp.full((1, tq), -jnp.inf, F32), jnp.zeros((ATT_VROWS, tq), F32))
                 for _ in range(n_heads))
    cur = list(lax.fori_loop(0, i, full_tile, init))

    ds = [[scores(pl.multiple_of(q0 + u * ATT_SUB, ATT_SUB), h, qhs[h][u * ATT_SUB:, :])
           for h in range(n_heads)] for u in range(nsub)]
    for u in range(nsub):
        c0 = u * ATT_SUB
        k0 = pl.multiple_of(q0 + c0, ATT_SUB)
        for h in range(n_heads):
            m_run, acc = cur[h]
            s = ds[u][h]
            s_diag = jnp.where(diag_ok, s[:, :ATT_SUB], -jnp.inf)
            s = s_diag if u == nsub - 1 else jnp.concatenate([s_diag, s[:, ATT_SUB:]], axis=1)
            m_part, acc_part = update(h, k0, m_run[:, c0:], acc[:, c0:], s)
            if u > 0:
                m_part = jnp.concatenate([m_run[:, :c0], m_part], axis=1)
                acc_part = jnp.concatenate([acc[:, :c0], acc_part], axis=1)
            cur[h] = (m_part, acc_part)

    outs = [acc[:MLA_V, :] / acc[MLA_V:MLA_V + 1, :] for _, acc in cur]
    o_ref[...] = jnp.concatenate(outs, axis=0).T.astype(o_ref.dtype)


def _attention(q, k, vt, tq, n_heads):
    bsz, seq, nq = q.shape
    tq = min(tq, seq)
    width = n_heads * MLA_HD
    return pl.pallas_call(
        functools.partial(_attn_kernel, tq=tq, n_heads=n_heads),
        grid=(bsz, MLA_HEADS // n_heads, seq // tq),
        in_specs=[pl.BlockSpec((None, tq, width), lambda b, h, i: (b, i, h)),
                  pl.BlockSpec((None, seq, width), lambda b, h, i: (b, 0, h)),
                  pl.BlockSpec((None, n_heads * ATT_VROWS, seq), lambda b, h, i: (b, h, 0))],
        out_specs=pl.BlockSpec((None, tq, n_heads * MLA_V), lambda b, h, i: (b, i, h)),
        out_shape=jax.ShapeDtypeStruct((bsz, seq, MLA_HEADS * MLA_V), BF16),
        name="mla_attention",
        compiler_params=_cparams(("parallel", "parallel", "arbitrary")),
    )(q, k, vt)


def _pad_cols(w, width):
    return jnp.pad(w, ((0, 0), (0, width - w.shape[1])))


def _pad_rows(w, height):
    return jnp.pad(w, ((0, height - w.shape[0]), (0, 0)))


def _even_weights(w_in, mu):
    gla_in = 2 * GLA_QK + 2 * GLA_V + GLA_GATE_RANK
    w_gla, w_rw = w_in[:, :gla_in], w_in[:, gla_in:]
    o = 3 * RW_W
    rw_parts = [w_rw[:, :o],
                _pad_cols(w_rw[:, o:o + RW_DECAY_RANK], LANE),
                _pad_cols(w_rw[:, o + RW_DECAY_RANK:o + RW_DECAY_RANK + RW_A_RANK], LANE),
                _pad_cols(w_rw[:, o + RW_DECAY_RANK + RW_A_RANK:], 2 * LANE)]
    mu_parts = [mu[:o],
                jnp.pad(mu[o:o + RW_DECAY_RANK], (0, LANE - RW_DECAY_RANK)),
                jnp.pad(mu[o + RW_DECAY_RANK:o + RW_DECAY_RANK + RW_A_RANK], (0, LANE - RW_A_RANK)),
                jnp.pad(mu[o + RW_DECAY_RANK + RW_A_RANK:], (0, 2 * LANE - RW_GATE_RANK))]
    w_all = jnp.concatenate(rw_parts + [_pad_cols(w_gla, GLA_COLS)], axis=1).astype(BF16)
    return w_all, jnp.concatenate(mu_parts).reshape(1, RW_COLS)


def _mla_weights(w_in, w_q_b, w_kv_b):
    swap = np.arange(MLA_ROPE) ^ 1
    d = w_in.shape[0]
    kpe = w_in[:, MLA_Q_RANK + MLA_KV_RANK:]
    w_in_p = jnp.concatenate([w_in[:, :MLA_Q_RANK + MLA_KV_RANK],
                              jnp.zeros((d, MLA_NOPE), F32), kpe, kpe[:, swap]], axis=1).astype(BF16)
    wq = w_q_b.reshape(MLA_Q_RANK, MLA_HEADS, MLA_NOPE + MLA_ROPE)
    wq = jnp.concatenate([wq, wq[:, :, MLA_NOPE:][:, :, swap]], axis=-1)
    wq = wq.reshape(MLA_Q_RANK, MLA_HEADS * MLA_HD).astype(BF16)
    wkv = w_kv_b.reshape(MLA_KV_RANK, MLA_HEADS, MLA_NOPE + MLA_V)
    wk = jnp.pad(wkv[:, :, :MLA_NOPE], ((0, 0), (0, 0), (0, MLA_HD - MLA_NOPE)))
    wk = wk.reshape(MLA_KV_RANK, MLA_HEADS * MLA_HD).astype(BF16)
    wv = jnp.pad(wkv[:, :, MLA_NOPE:], ((0, 0), (0, 0), (0, ATT_VROWS - MLA_V)))
    wvt = wv.reshape(MLA_KV_RANK, MLA_HEADS * ATT_VROWS).T.astype(BF16)
    return w_in_p, wq, wk, wvt


def _rope_tables():
    inv_freq = ROPE_THETA ** (-jnp.arange(0, MLA_ROPE, 2, dtype=jnp.float32) / MLA_ROPE)
    d = np.arange(MLA_ROPE)
    freq = jnp.zeros((LANE,), F32).at[MLA_ROPE_OFF:MLA_ROPE_OFF + MLA_ROPE].set(inv_freq[d // 2])
    sgn = np.zeros((LANE,), np.float32)
    sgn[MLA_ROPE_OFF:MLA_ROPE_OFF + MLA_ROPE] = np.where(d % 2 == 0, -1.0, 1.0)
    return freq.reshape(1, LANE), jnp.asarray(sgn).reshape(1, LANE)


def kernel(x, positions, even_w_in, gla_gate_w2, gla_gate_b, gla_norm_g, rwkv_mu, rwkv_w0, rwkv_w2,
           rwkv_a0, rwkv_a2, rwkv_g2, rwkv_k_k, rwkv_k_a, rwkv_r_k, rwkv_ln_g, rwkv_ln_b, even_w_out,
           mla_w_in, mla_q_norm_g, mla_w_q_b, mla_kv_norm_g, mla_w_kv_b, mla_w_out,
           ffn_w_gate_up, ffn_w_down, ln_g, ln_b):
    bsz, seq, d = x.shape
    m = bsz * seq
    x2d = x.reshape(m, d)
    wgu = ffn_w_gate_up.astype(BF16)
    wdn = ffn_w_down.astype(BF16)

    w_all, mu_p = _even_weights(even_w_in[0], rwkv_mu[0])
    p_rw, p_gla = _inproj(x2d, w_all, (RW_COLS, GLA_COLS), F32, tm=256)
    o_a = _gla(p_gla.reshape(bsz, seq, GLA_COLS),
               _pad_rows(gla_gate_w2[0], LANE).astype(BF16), gla_gate_b[0], gla_norm_g[0], tt=GLA_TT)
    vecs = jnp.stack([rwkv_w0[0], rwkv_a0[0], rwkv_k_k[0], rwkv_k_a[0], rwkv_r_k[0].reshape(RW_W),
                      rwkv_ln_g[0], rwkv_ln_b[0], jnp.zeros((RW_W,), F32)])
    o_b = _rwkv(p_rw.reshape(bsz, seq, RW_COLS), mu_p, vecs,
                _pad_rows(rwkv_w2[0], LANE).astype(BF16), _pad_rows(rwkv_a2[0], LANE).astype(BF16),
                _pad_rows(rwkv_g2[0], 2 * LANE).astype(BF16), tt=RW_TT)
    w_out = even_w_out[0].astype(BF16)
    x2d = _outproj_ln([o_a.reshape(m, GLA_V), o_b.reshape(m, RW_W)], [w_out[:GLA_V], w_out[GLA_V:]],
                      x2d, ln_g[0, 0], ln_b[0, 0], tm=512)
    x2d = _ffn_ln(x2d, wgu[0], wdn[0], ln_g[0, 1], ln_b[0, 1], tm=1024, th=256)

    w_in_p, wq, wk, wvt = _mla_weights(mla_w_in[0], mla_w_q_b[0], mla_w_kv_b[0])
    freq, sgn = _rope_tables()
    q, k, vt = _mla_proj(x2d, positions.reshape(m, 1), freq, sgn, w_in_p, mla_q_norm_g[0], wq,
                         mla_kv_norm_g[0], wk, wvt, bsz, seq, tm=256)
    nq = MLA_HEADS * MLA_HD
    o = _attention(q.reshape(bsz, seq, nq), k.reshape(bsz, seq, nq), vt, tq=ATT_TQ, n_heads=ATT_HEADS)
    x2d = _outproj_ln([o.reshape(m, MLA_HEADS * MLA_V)], [mla_w_out[0].astype(BF16)],
                      x2d, ln_g[1, 0], ln_b[1, 0], tm=512)
    x2d = _ffn_ln(x2d, wgu[1], wdn[1], ln_g[1, 1], ln_b[1, 1], tm=1024, th=256)
    return x2d.reshape(bsz, seq, d)
```

```python
import functools
import math

import jax
import jax.numpy as jnp
import numpy as np
from jax import lax
from jax.experimental import pallas as pl
from jax.experimental.pallas import tpu as pltpu

F32 = jnp.float32
BF16 = jnp.bfloat16

DEPTH = 2
CHUNK = 64
DN_ALPHA = (2.0 * DEPTH) ** 0.25
LN_EPS = 1e-5
RMS_EPS = 1e-6

GLA_HEADS = 4
GLA_DK = 64
GLA_DV = 128
GLA_GATE_RANK = 16
GLA_TAU = 16.0
GLA_QK = GLA_HEADS * GLA_DK
GLA_V = GLA_HEADS * GLA_DV
GLA_SUB = 16
GLA_EXP_CLAMP = 60.0

RW_HEADS = 8
RW_N = 64
RW_W = RW_HEADS * RW_N
RW_DECAY_RANK = 64
RW_A_RANK = 64
RW_GATE_RANK = 160
RW_GN_EPS = 64e-5

MLA_HEADS = 16
MLA_NOPE = 64
MLA_ROPE = 32
MLA_V = 64
MLA_Q_RANK = 768
MLA_KV_RANK = 256
ROPE_THETA = 10000.0

LANE = 128
VMEM_LIMIT = 48 * 1024 * 1024
FFN_TM, FFN_TH = 1024, 256
MLA_TM = 512
ATT_TQ, ATT_HEADS = 1024, 2
ATT_SUB = 256
ATT_VROWS = 80

RW_COLS = 3 * RW_W + LANE + LANE + 2 * LANE
GLA_COLS = 2 * GLA_QK + 2 * GLA_V + LANE


def _cparams(sem):
    return pltpu.CompilerParams(dimension_semantics=sem, vmem_limit_bytes=VMEM_LIMIT)


def _bdot(a, b):
    return jnp.dot(a.astype(BF16), b.astype(BF16), preferred_element_type=F32)


def _dot_nt(a, b):
    return lax.dot_general(a.astype(BF16), b.astype(BF16), (((1,), (1,)), ((), ())),
                           preferred_element_type=F32)


def _dot_tn(a, b):
    return lax.dot_general(a.astype(BF16), b.astype(BF16), (((0,), (0,)), ((), ())),
                           preferred_element_type=F32)


def _dot_split(m01, x):
    hi = x.astype(BF16)
    lo = (x - hi.astype(F32)).astype(BF16)
    m = m01.astype(BF16)
    return (jnp.dot(m, hi, preferred_element_type=F32)
            + jnp.dot(m, lo, preferred_element_type=F32))


def _softplus(x):
    return jnp.maximum(x, 0.0) + jnp.log(1.0 + jnp.exp(-jnp.abs(x)))


def _layer_norm(y, g, b):
    mu = jnp.mean(y, axis=-1, keepdims=True)
    d = y - mu
    var = jnp.mean(d * d, axis=-1, keepdims=True)
    return d * lax.rsqrt(var + LN_EPS) * g + b


def _inproj_kernel(x_ref, w_ref, *o_refs, widths, tn):
    xb = x_ref[...].astype(BF16)
    off = 0
    for o_ref, width in zip(o_refs, widths):
        for j in range(0, width, tn):
            w = min(tn, width - j)
            o_ref[:, j:j + w] = jnp.dot(xb, w_ref[:, off + j:off + j + w],
                                        preferred_element_type=F32).astype(o_ref.dtype)
        off += width


def _inproj(x2d, w_bf16, widths, out_dtype, tm):
    m, k = x2d.shape
    tm = min(tm, m)
    n = sum(widths)
    return pl.pallas_call(
        functools.partial(_inproj_kernel, widths=tuple(widths), tn=512),
        grid=(m // tm,),
        in_specs=[pl.BlockSpec((tm, k), lambda i: (i, 0)),
                  pl.BlockSpec((k, n), lambda i: (0, 0))],
        out_specs=[pl.BlockSpec((tm, wd), lambda i: (i, 0)) for wd in widths],
        out_shape=[jax.ShapeDtypeStruct((m, wd), out_dtype) for wd in widths],
        compiler_params=_cparams(("parallel",)),
    )(x2d, w_bf16)


def _mix_ffn_kernel(*refs, n_in):
    a_refs = refs[:n_in]
    wo_refs = refs[n_in:2 * n_in]
    (x_ref, g0_ref, b0_ref, wg_ref, wu_ref, wd_ref, g1_ref, b1_ref, o_ref,
     x1_ref, xb_ref, acc_ref) = refs[2 * n_in:]
    j = pl.program_id(1)

    @pl.when(j == 0)
    def _():
        y = DN_ALPHA * x_ref[...]
        for a_ref, w_ref in zip(a_refs, wo_refs):
            y = y + jnp.dot(a_ref[...], w_ref[...], preferred_element_type=F32)
        x1 = _layer_norm(y, g0_ref[...], b0_ref[...])
        x1_ref[...] = x1
        xb_ref[...] = x1.astype(BF16)
        acc_ref[...] = jnp.zeros_like(acc_ref)

    xb = xb_ref[...]
    gate = jnp.dot(xb, wg_ref[...], preferred_element_type=F32)
    up = jnp.dot(xb, wu_ref[...], preferred_element_type=F32)
    h = (gate * jax.nn.sigmoid(gate) * up).astype(BF16)
    acc_ref[...] += jnp.dot(h, wd_ref[...], preferred_element_type=F32)

    @pl.when(j == pl.num_programs(1) - 1)
    def _():
        y = DN_ALPHA * x1_ref[...] + acc_ref[...]
        o_ref[...] = _layer_norm(y, g1_ref[...], b1_ref[...])


def _mix_ffn(acts, w_outs, x2d, g0, b0, w_gate_up, w_down, g1, b1, tm, th):
    m, d = x2d.shape
    hidden = w_down.shape[0]
    tm = min(tm, m)
    nh = hidden // th
    n_in = len(acts)
    vec = pl.BlockSpec((1, d), lambda i, j: (0, 0))
    in_specs = ([pl.BlockSpec((tm, a.shape[1]), lambda i, j: (i, 0)) for a in acts]
                + [pl.BlockSpec(w.shape, lambda i, j: (0, 0)) for w in w_outs]
                + [pl.BlockSpec((tm, d), lambda i, j: (i, 0)), vec, vec,
                   pl.BlockSpec((d, th), lambda i, j: (0, j)),
                   pl.BlockSpec((d, th), lambda i, j: (0, j + nh)),
                   pl.BlockSpec((th, d), lambda i, j: (j, 0)), vec, vec])
    return pl.pallas_call(
        functools.partial(_mix_ffn_kernel, n_in=n_in),
        grid=(m // tm, nh),
        in_specs=in_specs,
        out_specs=pl.BlockSpec((tm, d), lambda i, j: (i, 0)),
        out_shape=jax.ShapeDtypeStruct((m, d), F32),
        scratch_shapes=[pltpu.VMEM((tm, d), F32), pltpu.VMEM((tm, d), BF16), pltpu.VMEM((tm, d), F32)],
        name="mix_ffn",
        compiler_params=_cparams(("parallel", "arbitrary")),
    )(*acts, *w_outs, x2d, g0.reshape(1, d), b0.reshape(1, d), w_gate_up, w_gate_up, w_down,
      g1.reshape(1, d), b1.reshape(1, d))


def _gla_kernel(p_ref, w2_ref, gb_ref, ng_ref, o_ref, st_ref, *, n_chunks):
    @pl.when(pl.program_id(1) == 0)
    def _():
        st_ref[...] = jnp.zeros_like(st_ref)

    c_row = lax.broadcasted_iota(jnp.int32, (CHUNK, CHUNK), 0)
    c_col = lax.broadcasted_iota(jnp.int32, (CHUNK, CHUNK), 1)
    tri = (c_col <= c_row)
    tri_loc = tri & (c_col >= (c_row // GLA_SUB) * GLA_SUB)
    cum_mat = jnp.concatenate([tri, tri_loc], axis=0).astype(BF16)
    lane = lax.broadcasted_iota(jnp.int32, (1, GLA_QK), 1)
    head_masks = [(lane // GLA_DK == h).astype(F32) for h in range(GLA_HEADS)]
    n_sub = CHUNK // GLA_SUB

    n_par = min(GLA_PAR_CHUNKS, n_chunks)
    o_v = 2 * GLA_QK
    o_g = o_v + GLA_V
    o_l = o_g + GLA_V

    def group(gi, carry):
        base = gi * (n_par * CHUNK)
        rows = [pl.ds(pl.multiple_of(base + cc * CHUNK, CHUNK), CHUNK) for cc in range(n_par)]
        q = [p_ref[r, 0:GLA_QK] * (GLA_DK ** -0.5) for r in rows]
        k = [p_ref[r, GLA_QK:o_v] for r in rows]
        vb = [p_ref[r, o_v:o_g].astype(BF16) for r in rows]
        gg = [p_ref[r, o_g:o_l] for r in rows]
        z = [_bdot(p_ref[r, o_l:GLA_COLS], w2_ref[...]) + gb_ref[...] for r in rows]
        log_a = [-_softplus(-x) / GLA_TAU for x in z]
        cums = [_dot_split(cum_mat, x) for x in log_a]
        b = [x[:CHUNK] for x in cums]
        b_last = [x[CHUNK - 1:CHUNK, :] for x in b]
        q_loc = [x * jnp.exp(c[CHUNK:]) for x, c in zip(q, cums)]
        a_blocks = []
        for cc in range(n_par):
            blocks = []
            for i in range(n_sub):
                ref_i = (jnp.zeros((1, GLA_QK), F32) if i == 0
                         else b[cc][i * GLA_SUB - 1:i * GLA_SUB, :])
                k_i = k[cc] * jnp.exp(jnp.minimum(ref_i - b[cc], GLA_EXP_CLAMP))
                q_i = q_loc[cc][i * GLA_SUB:(i + 1) * GLA_SUB, :]
                lhs = jnp.concatenate([q_i * m for m in head_masks], axis=0)
                blocks.append(_dot_nt(lhs, k_i))
            a_blocks.append(blocks)
        o_intra = []
        for cc in range(n_par):
            per_head = []
            for h in range(GLA_HEADS):
                a_h = jnp.concatenate(
                    [blk[h * GLA_SUB:(h + 1) * GLA_SUB, :] for blk in a_blocks[cc]], axis=0)
                a_h = jnp.where(tri, a_h, 0.0)
                per_head.append(_bdot(a_h, vb[cc][:, h * GLA_DV:(h + 1) * GLA_DV]))
            o_intra.append(per_head)
        lhs_inter = [jnp.concatenate([x * jnp.exp(bb) * m for m in head_masks], axis=0)
                     for x, bb in zip(q, b)]
        kv = [_dot_tn(v, x * jnp.exp(bl - bb)) for v, x, bl, bb in zip(vb, k, b_last, b)]
        kv = [sum(x[h * GLA_DV:(h + 1) * GLA_DV, :] * head_masks[h] for h in range(GLA_HEADS)) for x in kv]
        decay = [jnp.exp(x) for x in b_last]

        st = st_ref[...]
        outs = []
        for cc in range(n_par):
            o_inter = _dot_nt(lhs_inter[cc], st)
            st = st * decay[cc] + kv[cc]
            for h in range(GLA_HEADS):
                o_h = o_inter[h * CHUNK:(h + 1) * CHUNK, :] + o_intra[cc][h]
                ms = jnp.mean(o_h * o_h, axis=-1, keepdims=True)
                o_h = o_h * lax.rsqrt(ms + RMS_EPS) * ng_ref[...]
                g_h = gg[cc][:, h * GLA_DV:(h + 1) * GLA_DV]
                outs.append((o_h * (g_h * jax.nn.sigmoid(g_h))).astype(o_ref.dtype))
        for cc in range(n_par):
            for h in range(GLA_HEADS):
                o_ref[rows[cc], h * GLA_DV:(h + 1) * GLA_DV] = outs[cc * GLA_HEADS + h]
        st_ref[...] = st
        return carry

    lax.fori_loop(0, n_chunks // n_par, group, 0)


def _gla(p_gla, w2p, gate_b, norm_g, tt):
    bsz, seq, _ = p_gla.shape
    tt = min(tt, seq)
    return pl.pallas_call(
        functools.partial(_gla_kernel, n_chunks=tt // CHUNK),
        grid=(bsz, seq // tt),
        in_specs=[pl.BlockSpec((None, tt, GLA_COLS), lambda b, t: (b, t, 0)),
                  pl.BlockSpec((LANE, GLA_QK), lambda b, t: (0, 0)),
                  pl.BlockSpec((1, GLA_QK), lambda b, t: (0, 0)),
                  pl.BlockSpec((1, GLA_DV), lambda b, t: (0, 0))],
        out_specs=pl.BlockSpec((None, tt, GLA_V), lambda b, t: (b, t, 0)),
        out_shape=jax.ShapeDtypeStruct((bsz, seq, GLA_V), BF16),
        scratch_shapes=[pltpu.VMEM((GLA_DV, GLA_QK), F32)],
        compiler_params=_cparams(("parallel", "arbitrary")),
    )(p_gla, w2p, gate_b.reshape(1, GLA_QK), norm_g.reshape(1, GLA_DV))


RW_PAIR = 2 * RW_N
RW_NPAIR = RW_HEADS // 2
RW_PAR_CHUNKS = 4
RW_TT = 256
GLA_TT = 512
GLA_PAR_CHUNKS = 8


def _rwkv_kernel(p_ref, prev_ref, mu_ref, vec_ref, w2_ref, a2_ref, g2_ref, seg_ref, o_ref,
                 st_ref, r_s, k_s, v_s, lw_s, av_s, bv_s, y_s, bonus_s, g_s, *, n_chunks):
    t_idx = pl.program_id(1)

    @pl.when(t_idx == 0)
    def _():
        st_ref[...] = jnp.zeros_like(st_ref)

    tt = p_ref.shape[0]
    w0, a0, k_k, k_a, r_k, ln_g, ln_b = [vec_ref[i:i + 1, :] for i in range(7)]
    seg = seg_ref[...]

    def seg_sum(x):
        return jnp.concatenate(
            [_dot_split_rhs(x[:, i * LANE:(i + 1) * LANE], seg) for i in range(RW_W // LANE)], axis=1)

    p = p_ref[...]
    rows = lax.broadcasted_iota(jnp.int32, (tt, 1), 0)
    first = jnp.where(t_idx > 0, prev_ref[7:8, :], 0.0)
    shifted = jnp.where(rows == 0, first, pltpu.roll(p, 1, 0))
    p = p + mu_ref[...] * (shifted - p)
    r = p[:, 0:RW_W]
    k = p[:, RW_W:2 * RW_W]
    v = p[:, 2 * RW_W:3 * RW_W]
    wl = p[:, 3 * RW_W:3 * RW_W + LANE]
    al = p[:, 3 * RW_W + LANE:3 * RW_W + 2 * LANE]
    gl = p[:, 3 * RW_W + 2 * LANE:RW_COLS]
    w_raw = -_softplus(-(w0 + _bdot(jnp.tanh(wl), w2_ref[...]))) - 0.5
    lw = -jnp.exp(w_raw)
    a = jax.nn.sigmoid(a0 + _bdot(al, a2_ref[...]))
    g = _bdot(jax.nn.sigmoid(gl), g2_ref[...])
    kk = k * k_k
    kk = kk * lax.rsqrt(jnp.maximum(seg_sum(kk * kk), 1e-24))
    k2 = k * (1.0 + (a - 1.0) * k_a)
    bonus_s[...] = seg_sum(r * k2 * r_k) * v
    g_s[...] = g
    r_s[...] = r
    k_s[...] = k2
    v_s[...] = v
    lw_s[...] = lw
    av_s[...] = -kk
    bv_s[...] = kk * a

    c_row = lax.broadcasted_iota(jnp.int32, (CHUNK, CHUNK), 0)
    c_col = lax.broadcasted_iota(jnp.int32, (CHUNK, CHUNK), 1)
    tri = c_col <= c_row
    tri_strict = c_col < c_row
    tri_b = tri.astype(BF16)
    eye = (c_col == c_row).astype(F32)
    lane = lax.broadcasted_iota(jnp.int32, (1, RW_PAIR), 1)
    hmask = [(lane < RW_N).astype(F32), (lane >= RW_N).astype(F32)]
    p_row = lax.broadcasted_iota(jnp.int32, (RW_PAIR, RW_PAIR), 0)
    p_col = lax.broadcasted_iota(jnp.int32, (RW_PAIR, RW_PAIR), 1)
    bd = ((p_row < RW_N) == (p_col < RW_N)).astype(F32)

    n_par = min(RW_PAR_CHUNKS, n_chunks)
    items = [(cc, pi) for cc in range(n_par) for pi in range(RW_NPAIR)]
    r4 = lax.broadcasted_iota(jnp.int32, (CHUNK, 4 * CHUNK), 0)
    c4 = lax.broadcasted_iota(jnp.int32, (CHUNK, 4 * CHUNK), 1) % CHUNK
    tri4, tri_strict4 = c4 <= r4, c4 < r4
    eye4 = (c4 == r4).astype(F32)
    hmask2 = [jnp.concatenate([mk, mk], axis=1) for mk in hmask]
    q_row = lax.broadcasted_iota(jnp.int32, (4 * CHUNK, 4 * CHUNK), 0) // CHUNK
    q_col = lax.broadcasted_iota(jnp.int32, (4 * CHUNK, 4 * CHUNK), 1) // CHUNK
    bd4 = q_row == q_col

    def group(gi, carry):
        base = gi * (n_par * CHUNK)

        def ld(ref):
            return [ref[pl.ds(pl.multiple_of(base + cc * CHUNK, CHUNK), CHUNK),
                        pi * RW_PAIR:(pi + 1) * RW_PAIR] for cc, pi in items]

        rr, kc, vc, lwc, avc, bvc = ld(r_s), ld(k_s), ld(v_s), ld(lw_s), ld(av_s), ld(bv_s)
        st0 = [st_ref[pi] for pi in range(RW_NPAIR)]
        gcum = [_dot_split(tri_b, x) for x in lwc]
        g_last = [g[CHUNK - 1:CHUNK, :] for g in gcum]
        e_neg = [jnp.exp(-g) for g in gcum]
        e_end = [jnp.exp(gl - g) for gl, g in zip(g_last, gcum)]
        r_t = [x * jnp.exp(g) for x, g in zip(rr, gcum)]
        a_t = [x * jnp.exp(g - lw) for x, g, lw in zip(avc, gcum, lwc)]
        b_t = [x * e for x, e in zip(bvc, e_neg)]
        k_t = [x * e for x, e in zip(kc, e_neg)]
        b_bar = [x * e for x, e in zip(bvc, e_end)]
        k_bar = [x * e for x, e in zip(kc, e_end)]

        ar = [jnp.concatenate([a, r], axis=0) for a, r in zip(a_t, r_t)]
        w_rows = [jnp.concatenate([b * hmask[0], b * hmask[1], k * hmask[0], k * hmask[1]], axis=0)
                  for b, k in zip(b_t, k_t)]
        g_all = [_dot_nt(x, w) for x, w in zip(ar, w_rows)]
        l_all = [jnp.where(tri_strict4, g[:CHUNK], 0.0) for g in g_all]
        m_all = [jnp.where(tri4, g[CHUNK:], 0.0) for g in g_all]

        def blockdiag(p4):
            p4 = p4.astype(BF16)
            return jnp.where(bd4, jnp.concatenate([p4, p4, p4, p4], axis=0), jnp.zeros((), BF16))
        pw = [jnp.concatenate([l_all[2 * qd][:, :RW_PAIR], l_all[2 * qd + 1][:, :RW_PAIR]], axis=1)
              for qd in range(len(items) // 2)]
        t_inv = [eye4 + x for x in pw]
        pw_bd = [blockdiag(x) for x in pw]
        n = 2
        while n < CHUNK:
            pw = [jnp.dot(x.astype(BF16), xb, preferred_element_type=F32) for x, xb in zip(pw, pw_bd)]
            pw_bd = [blockdiag(x) for x in pw]
            t_inv = [t + jnp.dot(t.astype(BF16), xb, preferred_element_type=F32)
                     for t, xb in zip(t_inv, pw_bd)]
            n *= 2
        t_pair = [t_inv[it // 2][:, (it % 2) * RW_PAIR:(it % 2 + 1) * RW_PAIR] for it in range(len(items))]

        v_rows = [jnp.concatenate([v * hmask[0], v * hmask[1]], axis=0) for v in vc]
        lm = [_bdot(jnp.concatenate([l[:, RW_PAIR:], mm[:, RW_PAIR:]], axis=0), v)
              for l, mm, v in zip(l_all, m_all, v_rows)]
        x_rows = [jnp.concatenate(
            [jnp.concatenate([a * hmask[h], x[:CHUNK] * hmask[h]], axis=1) for h in range(2)], axis=0)
            for a, x in zip(a_t, lm)]
        x_p = [_bdot(t, x) for t, x in zip(t_pair, x_rows)]
        ry = [_bdot(mm[:, :RW_PAIR], jnp.concatenate([x * hmask2[0], x * hmask2[1]], axis=0))
              for mm, x in zip(m_all, x_p)]
        a_hat = [x[:, :RW_PAIR] for x in x_p]
        u_hat = [x[:, RW_PAIR:] for x in x_p]
        r_eff = [r + y[:, :RW_PAIR] for r, y in zip(r_t, ry)]
        y0 = [y[:, RW_PAIR:] + x[CHUNK:] for y, x in zip(ry, lm)]
        p_t = [_dot_tn(a, b) * bd for a, b in zip(a_hat, b_bar)]
        q_t = [_dot_tn(jnp.concatenate([u, v], axis=0), jnp.concatenate([b, k], axis=0)) * bd
               for u, v, b, k in zip(u_hat, vc, b_bar, k_bar)]
        w_end = [jnp.exp(g) for g in g_last]

        st = list(st0)
        ys = []
        for it, (cc, pi) in enumerate(items):
            stb = st[pi].astype(BF16)
            ys.append(_dot_nt(r_eff[it], stb) + y0[it])
            st[pi] = st[pi] * w_end[it] + _bdot(stb, p_t[it]) + q_t[it]
        for it, (cc, pi) in enumerate(items):
            y_s[pl.ds(pl.multiple_of(base + cc * CHUNK, CHUNK), CHUNK),
                pi * RW_PAIR:(pi + 1) * RW_PAIR] = ys[it]
        for pi in range(RW_NPAIR):
            st_ref[pi] = st[pi]
        return carry

    lax.fori_loop(0, n_chunks // n_par, group, 0)

    y = y_s[...]
    inv_n = 1.0 / RW_N
    mu_y = seg_sum(y) * inv_n
    d = y - mu_y
    var = seg_sum(d * d) * inv_n
    yn = d * lax.rsqrt(var + RW_GN_EPS) * ln_g + ln_b
    o_ref[...] = ((yn + bonus_s[...]) * g_s[...]).astype(o_ref.dtype)


def _dot_split_rhs(x, m01):
    hi = x.astype(BF16)
    lo = (x - hi.astype(F32)).astype(BF16)
    return (jnp.dot(hi, m01, preferred_element_type=F32)
            + jnp.dot(lo, m01, preferred_element_type=F32))


def _rwkv(p_rw, mu, vecs, w2p, a2p, g2p, tt):
    bsz, seq, _ = p_rw.shape
    tt = min(tt, seq)
    seg = (np.arange(LANE)[:, None] // RW_N == np.arange(LANE)[None, :] // RW_N)
    seg = jnp.asarray(seg, BF16)
    row_blocks = tt // 8
    scratch = ([pltpu.VMEM((RW_NPAIR, RW_PAIR, RW_PAIR), F32)]
               + [pltpu.VMEM((tt, RW_W), F32) for _ in range(9)])
    return pl.pallas_call(
        functools.partial(_rwkv_kernel, n_chunks=tt // CHUNK),
        grid=(bsz, seq // tt),
        in_specs=[pl.BlockSpec((None, tt, RW_COLS), lambda b, t: (b, t, 0)),
                  pl.BlockSpec((None, 8, RW_COLS),
                               lambda b, t: (b, jnp.maximum(t * row_blocks - 1, 0), 0)),
                  pl.BlockSpec((1, RW_COLS), lambda b, t: (0, 0)),
                  pl.BlockSpec((8, RW_W), lambda b, t: (0, 0)),
                  pl.BlockSpec((LANE, RW_W), lambda b, t: (0, 0)),
                  pl.BlockSpec((LANE, RW_W), lambda b, t: (0, 0)),
                  pl.BlockSpec((2 * LANE, RW_W), lambda b, t: (0, 0)),
                  pl.BlockSpec((LANE, LANE), lambda b, t: (0, 0))],
        out_specs=pl.BlockSpec((None, tt, RW_W), lambda b, t: (b, t, 0)),
        out_shape=jax.ShapeDtypeStruct((bsz, seq, RW_W), BF16),
        scratch_shapes=scratch,
        compiler_params=_cparams(("parallel", "arbitrary")),
    )(p_rw, p_rw, mu, vecs, w2p, a2p, g2p, seg)


MLA_HD = LANE
MLA_ROPE_OFF = MLA_NOPE
ROLL_ROPE = LANE - MLA_ROPE


def _mla_proj_kernel(x_ref, pos_ref, freq_ref, sgn_ref, win_ref, qg_ref, wq_ref, kvg_ref, wk_ref,
                     wvt_ref, q_ref, k_ref, vt_ref):
    xb = x_ref[...].astype(BF16)
    p = jnp.dot(xb, win_ref[...], preferred_element_type=F32)
    cq = p[:, :MLA_Q_RANK]
    ckv = p[:, MLA_Q_RANK:MLA_Q_RANK + MLA_KV_RANK]
    kpe = p[:, MLA_Q_RANK + MLA_KV_RANK:]
    cqn = cq * lax.rsqrt(jnp.mean(cq * cq, axis=-1, keepdims=True) + RMS_EPS) * qg_ref[...]
    ckvn = ckv * lax.rsqrt(jnp.mean(ckv * ckv, axis=-1, keepdims=True) + RMS_EPS) * kvg_ref[...]
    cqb = cqn.astype(BF16)
    ckvb = ckvn.astype(BF16)

    ang = pos_ref[...].astype(F32) * freq_ref[...]
    lane = lax.broadcasted_iota(jnp.int32, (1, LANE), 1)
    is_rope = (lane >= MLA_ROPE_OFF) & (lane < MLA_ROPE_OFF + MLA_ROPE)
    cos_t = jnp.where(is_rope, jnp.cos(ang), 0.0)
    sin_t = jnp.sin(ang) * sgn_ref[...]
    scale = (MLA_NOPE + MLA_ROPE) ** -0.5 * math.log2(math.e)
    q_c = jnp.where(lane < MLA_NOPE, 1.0, cos_t) * scale
    q_s = sin_t * scale

    k_rot = kpe * cos_t + pltpu.roll(kpe, ROLL_ROPE, 1) * sin_t
    pair = lambda t: jnp.concatenate([t, t], axis=1)
    q_c2, q_s2, k_rot2 = pair(q_c), pair(q_s), pair(k_rot)
    for hp in range(MLA_HEADS // 2):
        cols = slice(2 * hp * MLA_HD, 2 * (hp + 1) * MLA_HD)
        qh = jnp.dot(cqb, wq_ref[:, cols], preferred_element_type=F32)
        q_ref[:, cols] = (qh * q_c2 + pltpu.roll(qh, 2 * MLA_HD - MLA_ROPE, 1) * q_s2).astype(q_ref.dtype)
        kh = jnp.dot(ckvb, wk_ref[:, cols], preferred_element_type=F32)
        k_ref[:, cols] = (kh + k_rot2).astype(k_ref.dtype)
    vt = lax.dot_general(wvt_ref[...], ckvb, (((1,), (1,)), ((), ())),
                         preferred_element_type=F32)
    vrow = lax.broadcasted_iota(jnp.int32, (vt.shape[0], 1), 0)
    vt_ref[...] = (vt + ((vrow % ATT_VROWS) >= MLA_V).astype(F32)).astype(vt_ref.dtype)


def _mla_proj(x2d, pos2d, freq, sgn, w_in, q_g, w_q, kv_g, w_k, w_vt, bsz, seq, tm):
    m, d = x2d.shape
    tm = min(tm, seq)
    nq = MLA_HEADS * MLA_HD
    nv = MLA_HEADS * ATT_VROWS
    spb = seq // tm
    const = lambda shape: pl.BlockSpec(shape, lambda i: (0,) * len(shape))
    return pl.pallas_call(
        _mla_proj_kernel,
        grid=(m // tm,),
        in_specs=[pl.BlockSpec((tm, d), lambda i: (i, 0)),
                  pl.BlockSpec((tm, 1), lambda i: (i, 0)),
                  const((1, LANE)), const((1, LANE)),
                  const(w_in.shape), const((1, MLA_Q_RANK)), const(w_q.shape),
                  const((1, MLA_KV_RANK)), const(w_k.shape), const(w_vt.shape)],
        out_specs=[pl.BlockSpec((tm, nq), lambda i: (i, 0)),
                   pl.BlockSpec((tm, nq), lambda i: (i, 0)),
                   pl.BlockSpec((None, nv, tm), lambda i: (i // spb, 0, i % spb))],
        out_shape=[jax.ShapeDtypeStruct((m, nq), BF16),
                   jax.ShapeDtypeStruct((m, nq), BF16),
                   jax.ShapeDtypeStruct((bsz, nv, seq), BF16)],
        name="mla_proj",
        compiler_params=_cparams(("parallel",)),
    )(x2d, pos2d, freq, sgn, w_in, q_g.reshape(1, -1), w_q, kv_g.reshape(1, -1), w_k, w_vt)


def _attn_kernel(q_ref, k_ref, vt_ref, o_ref, *, tq, n_heads):
    i = pl.program_id(2)
    q0 = i * tq
    nsub = tq // ATT_SUB
    qhs = [q_ref[:, h * MLA_HD:(h + 1) * MLA_HD] for h in range(n_heads)]
    kr = lax.broadcasted_iota(jnp.int32, (ATT_SUB, ATT_SUB), 0) // CHUNK
    qc = lax.broadcasted_iota(jnp.int32, (ATT_SUB, ATT_SUB), 1) // CHUNK
    diag_ok = kr <= qc

    def scores(k0, h, q_rows):
        return lax.dot_general(k_ref[pl.ds(k0, ATT_SUB), h * MLA_HD:(h + 1) * MLA_HD], q_rows,
                               (((1,), (1,)), ((), ())), preferred_element_type=F32)

    def update(h, k0, m_run, acc, s):
        m_new = jnp.maximum(m_run, jnp.max(s, axis=0, keepdims=True))
        alpha = jnp.exp2(m_run - m_new)
        pr = jnp.exp2(s - m_new).astype(BF16)
        vt = vt_ref[h * ATT_VROWS:(h + 1) * ATT_VROWS, pl.ds(k0, ATT_SUB)]
        return m_new, acc * alpha + jnp.dot(vt, pr, preferred_element_type=F32)

    def full_tile(j, carry):
        k0 = pl.multiple_of(j * tq, tq)
        ss = [[scores(k0 + u * ATT_SUB, h, qhs[h]) for h in range(n_heads)] for u in range(nsub)]
        cur = list(carry)
        for u in range(nsub):
            for h in range(n_heads):
                cur[h] = update(h, k0 + u * ATT_SUB, *cur[h], ss[u][h])
        return tuple(cur)

    init = tuple((jnp.full((1, tq), -jnp.inf, F32), jnp.zeros((ATT_VROWS, tq), F32))
                 for _ in range(n_heads))
    cur = list(lax.fori_loop(0, i, full_tile, init))

    ds = [[scores(pl.multiple_of(q0 + u * ATT_SUB, ATT_SUB), h, qhs[h][u * ATT_SUB:, :])
           for h in range(n_heads)] for u in range(nsub)]
    for u in range(nsub):
        c0 = u * ATT_SUB
        k0 = pl.multiple_of(q0 + c0, ATT_SUB)
        for h in range(n_heads):
            m_run, acc = cur[h]
            s = ds[u][h]
            s_diag = jnp.where(diag_ok, s[:, :ATT_SUB], -jnp.inf)
            s = s_diag if u == nsub - 1 else jnp.concatenate([s_diag, s[:, ATT_SUB:]], axis=1)
            m_part, acc_part = update(h, k0, m_run[:, c0:], acc[:, c0:], s)
            if u > 0:
                m_part = jnp.concatenate([m_run[:, :c0], m_part], axis=1)
                acc_part = jnp.concatenate([acc[:, :c0], acc_part], axis=1)
            cur[h] = (m_part, acc_part)

    outs = [acc[:MLA_V, :] / acc[MLA_V:MLA_V + 1, :] for _, acc in cur]
    o_ref[...] = jnp.concatenate(outs, axis=0).T.astype(o_ref.dtype)


def _attention(q, k, vt, tq, n_heads):
    bsz, seq, nq = q.shape
    tq = min(tq, seq)
    width = n_heads * MLA_HD
    return pl.pallas_call(
        functools.partial(_attn_kernel, tq=tq, n_heads=n_heads),
        grid=(bsz, MLA_HEADS // n_heads, seq // tq),
        in_specs=[pl.BlockSpec((None, tq, width), lambda b, h, i: (b, i, h)),
                  pl.BlockSpec((None, seq, width), lambda b, h, i: (b, 0, h)),
                  pl.BlockSpec((None, n_heads * ATT_VROWS, seq), lambda b, h, i: (b, h, 0))],
        out_specs=pl.BlockSpec((None, tq, n_heads * MLA_V), lambda b, h, i: (b, i, h)),
        out_shape=jax.ShapeDtypeStruct((bsz, seq, MLA_HEADS * MLA_V), BF16),
        name="mla_attention",
        compiler_params=_cparams(("parallel", "parallel", "arbitrary")),
    )(q, k, vt)


def _pad_cols(w, width):
    return jnp.pad(w, ((0, 0), (0, width - w.shape[1])))


def _pad_rows(w, height):
    return jnp.pad(w, ((0, height - w.shape[0]), (0, 0)))


def _even_weights(w_in, mu):
    gla_in = 2 * GLA_QK + 2 * GLA_V + GLA_GATE_RANK
    w_gla, w_rw = w_in[:, :gla_in], w_in[:, gla_in:]
    o = 3 * RW_W
    rw_parts = [w_rw[:, :o],
                _pad_cols(w_rw[:, o:o + RW_DECAY_RANK], LANE),
                _pad_cols(w_rw[:, o + RW_DECAY_RANK:o + RW_DECAY_RANK + RW_A_RANK], LANE),
                _pad_cols(w_rw[:, o + RW_DECAY_RANK + RW_A_RANK:], 2 * LANE)]
    mu_parts = [mu[:o],
                jnp.pad(mu[o:o + RW_DECAY_RANK], (0, LANE - RW_DECAY_RANK)),
                jnp.pad(mu[o + RW_DECAY_RANK:o + RW_DECAY_RANK + RW_A_RANK], (0, LANE - RW_A_RANK)),
                jnp.pad(mu[o + RW_DECAY_RANK + RW_A_RANK:], (0, 2 * LANE - RW_GATE_RANK))]
    w_all = jnp.concatenate(rw_parts + [_pad_cols(w_gla, GLA_COLS)], axis=1).astype(BF16)
    return w_all, jnp.concatenate(mu_parts).reshape(1, RW_COLS)


def _mla_weights(w_in, w_q_b, w_kv_b):
    swap = np.arange(MLA_ROPE) ^ 1
    d = w_in.shape[0]
    kpe = w_in[:, MLA_Q_RANK + MLA_KV_RANK:]
    w_in_p = jnp.concatenate([w_in[:, :MLA_Q_RANK + MLA_KV_RANK],
                              jnp.zeros((d, MLA_NOPE), F32), kpe, kpe[:, swap]], axis=1).astype(BF16)
    wq = w_q_b.reshape(MLA_Q_RANK, MLA_HEADS, MLA_NOPE + MLA_ROPE)
    wq = jnp.concatenate([wq, wq[:, :, MLA_NOPE:][:, :, swap]], axis=-1)
    wq = wq.reshape(MLA_Q_RANK, MLA_HEADS * MLA_HD).astype(BF16)
    wkv = w_kv_b.reshape(MLA_KV_RANK, MLA_HEADS, MLA_NOPE + MLA_V)
    wk = jnp.pad(wkv[:, :, :MLA_NOPE], ((0, 0), (0, 0), (0, MLA_HD - MLA_NOPE)))
    wk = wk.reshape(MLA_KV_RANK, MLA_HEADS * MLA_HD).astype(BF16)
    wv = jnp.pad(wkv[:, :, MLA_NOPE:], ((0, 0), (0, 0), (0, ATT_VROWS - MLA_V)))
    wvt = wv.reshape(MLA_KV_RANK, MLA_HEADS * ATT_VROWS).T.astype(BF16)
    return w_in_p, wq, wk, wvt


def _rope_tables():
    inv_freq = ROPE_THETA ** (-jnp.arange(0, MLA_ROPE, 2, dtype=jnp.float32) / MLA_ROPE)
    d = np.arange(MLA_ROPE)
    freq = jnp.zeros((LANE,), F32).at[MLA_ROPE_OFF:MLA_ROPE_OFF + MLA_ROPE].set(inv_freq[d // 2])
    sgn = np.zeros((LANE,), np.float32)
    sgn[MLA_ROPE_OFF:MLA_ROPE_OFF + MLA_ROPE] = np.where(d % 2 == 0, -1.0, 1.0)
    return freq.reshape(1, LANE), jnp.asarray(sgn).reshape(1, LANE)


def kernel(x, positions, even_w_in, gla_gate_w2, gla_gate_b, gla_norm_g, rwkv_mu, rwkv_w0, rwkv_w2,
           rwkv_a0, rwkv_a2, rwkv_g2, rwkv_k_k, rwkv_k_a, rwkv_r_k, rwkv_ln_g, rwkv_ln_b, even_w_out,
           mla_w_in, mla_q_norm_g, mla_w_q_b, mla_kv_norm_g, mla_w_kv_b, mla_w_out,
           ffn_w_gate_up, ffn_w_down, ln_g, ln_b):
    bsz, seq, d = x.shape
    m = bsz * seq
    x2d = x.reshape(m, d)
    wgu = ffn_w_gate_up.astype(BF16)
    wdn = ffn_w_down.astype(BF16)

    w_all, mu_p = _even_weights(even_w_in[0], rwkv_mu[0])
    p_rw, p_gla = _inproj(x2d, w_all, (RW_COLS, GLA_COLS), F32, tm=256)
    o_a = _gla(p_gla.reshape(bsz, seq, GLA_COLS),
               _pad_rows(gla_gate_w2[0], LANE).astype(BF16), gla_gate_b[0], gla_norm_g[0], tt=GLA_TT)
    vecs = jnp.stack([rwkv_w0[0], rwkv_a0[0], rwkv_k_k[0], rwkv_k_a[0], rwkv_r_k[0].reshape(RW_W),
                      rwkv_ln_g[0], rwkv_ln_b[0], jnp.zeros((RW_W,), F32)])
    o_b = _rwkv(p_rw.reshape(bsz, seq, RW_COLS), mu_p, vecs,
                _pad_rows(rwkv_w2[0], LANE).astype(BF16), _pad_rows(rwkv_a2[0], LANE).astype(BF16),
                _pad_rows(rwkv_g2[0], 2 * LANE).astype(BF16), tt=RW_TT)
    w_out = even_w_out[0].astype(BF16)
    x2d = _mix_ffn([o_a.reshape(m, GLA_V), o_b.reshape(m, RW_W)], [w_out[:GLA_V], w_out[GLA_V:]],
                   x2d, ln_g[0, 0], ln_b[0, 0], wgu[0], wdn[0], ln_g[0, 1], ln_b[0, 1],
                   tm=FFN_TM, th=FFN_TH)

    w_in_p, wq, wk, wvt = _mla_weights(mla_w_in[0], mla_w_q_b[0], mla_w_kv_b[0])
    freq, sgn = _rope_tables()
    q, k, vt = _mla_proj(x2d, positions.reshape(m, 1), freq, sgn, w_in_p, mla_q_norm_g[0], wq,
                         mla_kv_norm_g[0], wk, wvt, bsz, seq, tm=MLA_TM)
    nq = MLA_HEADS * MLA_HD
    o = _attention(q.reshape(bsz, seq, nq), k.reshape(bsz, seq, nq), vt, tq=ATT_TQ, n_heads=ATT_HEADS)
    x2d = _mix_ffn([o.reshape(m, MLA_HEADS * MLA_V)], [mla_w_out[0].astype(BF16)],
                   x2d, ln_g[1, 0], ln_b[1, 0], wgu[1], wdn[1], ln_g[1, 1], ln_b[1, 1],
                   tm=FFN_TM, th=FFN_TH)
    return x2d.reshape(bsz, seq, d)
```

```python
import functools
import math

import jax
import jax.numpy as jnp
import numpy as np
from jax import lax
from jax.experimental import pallas as pl
from jax.experimental.pallas import tpu as pltpu

F32 = jnp.float32
BF16 = jnp.bfloat16

DEPTH = 2
CHUNK = 64
DN_ALPHA = (2.0 * DEPTH) ** 0.25
LN_EPS = 1e-5
RMS_EPS = 1e-6

GLA_HEADS = 4
GLA_DK = 64
GLA_DV = 128
GLA_GATE_RANK = 16
GLA_TAU = 16.0
GLA_QK = GLA_HEADS * GLA_DK
GLA_V = GLA_HEADS * GLA_DV
GLA_SUB = 16
GLA_EXP_CLAMP = 60.0

RW_HEADS = 8
RW_N = 64
RW_W = RW_HEADS * RW_N
RW_DECAY_RANK = 64
RW_A_RANK = 64
RW_GATE_RANK = 160
RW_GN_EPS = 64e-5

MLA_HEADS = 16
MLA_NOPE = 64
MLA_ROPE = 32
MLA_V = 64
MLA_Q_RANK = 768
MLA_KV_RANK = 256
ROPE_THETA = 10000.0

LANE = 128
VMEM_LIMIT = 48 * 1024 * 1024
FFN_TM, FFN_TH = 1024, 256
MLA_TM = 512
ATT_TQ, ATT_HEADS = 1024, 2
ATT_SUB = 256
ATT_VROWS = 80

RW_COLS = 3 * RW_W + LANE + LANE + 2 * LANE
GLA_COLS = 2 * GLA_QK + 2 * GLA_V + LANE


def _cparams(sem):
    return pltpu.CompilerParams(dimension_semantics=sem, vmem_limit_bytes=VMEM_LIMIT)


def _bdot(a, b):
    return jnp.dot(a.astype(BF16), b.astype(BF16), preferred_element_type=F32)


def _dot_nt(a, b):
    return lax.dot_general(a.astype(BF16), b.astype(BF16), (((1,), (1,)), ((), ())),
                           preferred_element_type=F32)


def _dot_tn(a, b):
    return lax.dot_general(a.astype(BF16), b.astype(BF16), (((0,), (0,)), ((), ())),
                           preferred_element_type=F32)


def _dot_split(m01, x):
    hi = x.astype(BF16)
    lo = (x - hi.astype(F32)).astype(BF16)
    m = m01.astype(BF16)
    return (jnp.dot(m, hi, preferred_element_type=F32)
            + jnp.dot(m, lo, preferred_element_type=F32))


def _softplus(x):
    return jnp.maximum(x, 0.0) + jnp.log(1.0 + jnp.exp(-jnp.abs(x)))


def _layer_norm(y, g, b):
    mu = jnp.mean(y, axis=-1, keepdims=True)
    d = y - mu
    var = jnp.mean(d * d, axis=-1, keepdims=True)
    return d * lax.rsqrt(var + LN_EPS) * g + b


def _inproj_kernel(x_ref, w_ref, *o_refs, widths, tn):
    xb = x_ref[...].astype(BF16)
    off = 0
    for o_ref, width in zip(o_refs, widths):
        for j in range(0, width, tn):
            w = min(tn, width - j)
            o_ref[:, j:j + w] = jnp.dot(xb, w_ref[:, off + j:off + j + w],
                                        preferred_element_type=F32).astype(o_ref.dtype)
        off += width


def _inproj(x2d, w_bf16, widths, out_dtype, tm):
    m, k = x2d.shape
    tm = min(tm, m)
    n = sum(widths)
    return pl.pallas_call(
        functools.partial(_inproj_kernel, widths=tuple(widths), tn=512),
        grid=(m // tm,),
        in_specs=[pl.BlockSpec((tm, k), lambda i: (i, 0)),
                  pl.BlockSpec((k, n), lambda i: (0, 0))],
        out_specs=[pl.BlockSpec((tm, wd), lambda i: (i, 0)) for wd in widths],
        out_shape=[jax.ShapeDtypeStruct((m, wd), out_dtype) for wd in widths],
        compiler_params=_cparams(("parallel",)),
    )(x2d, w_bf16)


def _mix_ffn_kernel(*refs, n_in):
    a_refs = refs[:n_in]
    wo_refs = refs[n_in:2 * n_in]
    (x_ref, g0_ref, b0_ref, wg_ref, wu_ref, wd_ref, g1_ref, b1_ref, o_ref,
     x1_ref, xb_ref, acc_ref) = refs[2 * n_in:]
    j = pl.program_id(1)

    @pl.when(j == 0)
    def _():
        y = DN_ALPHA * x_ref[...]
        for a_ref, w_ref in zip(a_refs, wo_refs):
            y = y + jnp.dot(a_ref[...], w_ref[...], preferred_element_type=F32)
        x1 = _layer_norm(y, g0_ref[...], b0_ref[...])
        x1_ref[...] = x1
        xb_ref[...] = x1.astype(BF16)
        acc_ref[...] = jnp.zeros_like(acc_ref)

    xb = xb_ref[...]
    gate = jnp.dot(xb, wg_ref[...], preferred_element_type=F32)
    up = jnp.dot(xb, wu_ref[...], preferred_element_type=F32)
    h = (gate * jax.nn.sigmoid(gate) * up).astype(BF16)
    acc_ref[...] += jnp.dot(h, wd_ref[...], preferred_element_type=F32)

    @pl.when(j == pl.num_programs(1) - 1)
    def _():
        y = DN_ALPHA * x1_ref[...] + acc_ref[...]
        o_ref[...] = _layer_norm(y, g1_ref[...], b1_ref[...])


def _mix_ffn(acts, w_outs, x2d, g0, b0, w_gate_up, w_down, g1, b1, tm, th):
    m, d = x2d.shape
    hidden = w_down.shape[0]
    tm = min(tm, m)
    nh = hidden // th
    n_in = len(acts)
    vec = pl.BlockSpec((1, d), lambda i, j: (0, 0))
    in_specs = ([pl.BlockSpec((tm, a.shape[1]), lambda i, j: (i, 0)) for a in acts]
                + [pl.BlockSpec(w.shape, lambda i, j: (0, 0)) for w in w_outs]
                + [pl.BlockSpec((tm, d), lambda i, j: (i, 0)), vec, vec,
                   pl.BlockSpec((d, th), lambda i, j: (0, j)),
                   pl.BlockSpec((d, th), lambda i, j: (0, j + nh)),
                   pl.BlockSpec((th, d), lambda i, j: (j, 0)), vec, vec])
    return pl.pallas_call(
        functools.partial(_mix_ffn_kernel, n_in=n_in),
        grid=(m // tm, nh),
        in_specs=in_specs,
        out_specs=pl.BlockSpec((tm, d), lambda i, j: (i, 0)),
        out_shape=jax.ShapeDtypeStruct((m, d), F32),
        scratch_shapes=[pltpu.VMEM((tm, d), F32), pltpu.VMEM((tm, d), BF16), pltpu.VMEM((tm, d), F32)],
        name="mix_ffn",
        compiler_params=_cparams(("parallel", "arbitrary")),
    )(*acts, *w_outs, x2d, g0.reshape(1, d), b0.reshape(1, d), w_gate_up, w_gate_up, w_down,
      g1.reshape(1, d), b1.reshape(1, d))


def _gla_kernel(p_ref, w2_ref, gb_ref, ng_ref, o_ref, st_ref, *, n_chunks):
    @pl.when(pl.program_id(1) == 0)
    def _():
        st_ref[...] = jnp.zeros_like(st_ref)

    c_row = lax.broadcasted_iota(jnp.int32, (CHUNK, CHUNK), 0)
    c_col = lax.broadcasted_iota(jnp.int32, (CHUNK, CHUNK), 1)
    tri = (c_col <= c_row)
    tri_loc = tri & (c_col >= (c_row // GLA_SUB) * GLA_SUB)
    cum_mat = jnp.concatenate([tri, tri_loc], axis=0).astype(BF16)
    lane = lax.broadcasted_iota(jnp.int32, (1, GLA_QK), 1)
    head_masks = [(lane // GLA_DK == h).astype(F32) for h in range(GLA_HEADS)]
    n_sub = CHUNK // GLA_SUB

    n_par = min(GLA_PAR_CHUNKS, n_chunks)
    o_v = 2 * GLA_QK
    o_g = o_v + GLA_V
    o_l = o_g + GLA_V

    def group(gi, carry):
        base = gi * (n_par * CHUNK)
        rows = [pl.ds(pl.multiple_of(base + cc * CHUNK, CHUNK), CHUNK) for cc in range(n_par)]
        q = [p_ref[r, 0:GLA_QK] * (GLA_DK ** -0.5) for r in rows]
        k = [p_ref[r, GLA_QK:o_v] for r in rows]
        vb = [p_ref[r, o_v:o_g].astype(BF16) for r in rows]
        gg = [p_ref[r, o_g:o_l] for r in rows]
        z = [_bdot(p_ref[r, o_l:GLA_COLS], w2_ref[...]) + gb_ref[...] for r in rows]
        log_a = [-_softplus(-x) / GLA_TAU for x in z]
        cums = [_dot_split(cum_mat, x) for x in log_a]
        b = [x[:CHUNK] for x in cums]
        b_last = [x[CHUNK - 1:CHUNK, :] for x in b]
        q_loc = [x * jnp.exp(c[CHUNK:]) for x, c in zip(q, cums)]
        a_blocks = []
        for cc in range(n_par):
            blocks = []
            for i in range(n_sub):
                ref_i = (jnp.zeros((1, GLA_QK), F32) if i == 0
                         else b[cc][i * GLA_SUB - 1:i * GLA_SUB, :])
                k_i = k[cc] * jnp.exp(jnp.minimum(ref_i - b[cc], GLA_EXP_CLAMP))
                q_i = q_loc[cc][i * GLA_SUB:(i + 1) * GLA_SUB, :]
                lhs = jnp.concatenate([q_i * m for m in head_masks], axis=0)
                blocks.append(_dot_nt(lhs, k_i))
            a_blocks.append(blocks)
        o_intra = []
        for cc in range(n_par):
            per_head = []
            for h in range(GLA_HEADS):
                a_h = jnp.concatenate(
                    [blk[h * GLA_SUB:(h + 1) * GLA_SUB, :] for blk in a_blocks[cc]], axis=0)
                a_h = jnp.where(tri, a_h, 0.0)
                per_head.append(_bdot(a_h, vb[cc][:, h * GLA_DV:(h + 1) * GLA_DV]))
            o_intra.append(per_head)
        lhs_inter = [jnp.concatenate([x * jnp.exp(bb) * m for m in head_masks], axis=0)
                     for x, bb in zip(q, b)]
        kv = [_dot_tn(v, x * jnp.exp(bl - bb)) for v, x, bl, bb in zip(vb, k, b_last, b)]
        kv = [sum(x[h * GLA_DV:(h + 1) * GLA_DV, :] * head_masks[h] for h in range(GLA_HEADS)) for x in kv]
        decay = [jnp.exp(x) for x in b_last]

        st = st_ref[...]
        outs = []
        for cc in range(n_par):
            o_inter = _dot_nt(lhs_inter[cc], st)
            st = st * decay[cc] + kv[cc]
            for h in range(GLA_HEADS):
                o_h = o_inter[h * CHUNK:(h + 1) * CHUNK, :] + o_intra[cc][h]
                ms = jnp.mean(o_h * o_h, axis=-1, keepdims=True)
                o_h = o_h * lax.rsqrt(ms + RMS_EPS) * ng_ref[...]
                g_h = gg[cc][:, h * GLA_DV:(h + 1) * GLA_DV]
                outs.append((o_h * (g_h * jax.nn.sigmoid(g_h))).astype(o_ref.dtype))
        for cc in range(n_par):
            for h in range(GLA_HEADS):
                o_ref[rows[cc], h * GLA_DV:(h + 1) * GLA_DV] = outs[cc * GLA_HEADS + h]
        st_ref[...] = st
        return carry

    lax.fori_loop(0, n_chunks // n_par, group, 0)


def _gla(p_gla, w2p, gate_b, norm_g, tt):
    bsz, seq, _ = p_gla.shape
    tt = min(tt, seq)
    return pl.pallas_call(
        functools.partial(_gla_kernel, n_chunks=tt // CHUNK),
        grid=(bsz, seq // tt),
        in_specs=[pl.BlockSpec((None, tt, GLA_COLS), lambda b, t: (b, t, 0)),
                  pl.BlockSpec((LANE, GLA_QK), lambda b, t: (0, 0)),
                  pl.BlockSpec((1, GLA_QK), lambda b, t: (0, 0)),
                  pl.BlockSpec((1, GLA_DV), lambda b, t: (0, 0))],
        out_specs=pl.BlockSpec((None, tt, GLA_V), lambda b, t: (b, t, 0)),
        out_shape=jax.ShapeDtypeStruct((bsz, seq, GLA_V), BF16),
        scratch_shapes=[pltpu.VMEM((GLA_DV, GLA_QK), F32)],
        compiler_params=_cparams(("parallel", "arbitrary")),
    )(p_gla, w2p, gate_b.reshape(1, GLA_QK), norm_g.reshape(1, GLA_DV))


RW_PAIR = 2 * RW_N
RW_NPAIR = RW_HEADS // 2
RW_PAR_CHUNKS = 4
RW_TT = 256
GLA_TT = 512
GLA_PAR_CHUNKS = 8


def _rwkv_kernel(p_ref, prev_ref, mu_ref, vec_ref, w2_ref, a2_ref, g2_ref, seg_ref, o_ref,
                 st_ref, r_s, k_s, v_s, lw_s, av_s, bv_s, y_s, bonus_s, g_s, *, n_chunks):
    t_idx = pl.program_id(1)

    @pl.when(t_idx == 0)
    def _():
        st_ref[...] = jnp.zeros_like(st_ref)

    tt = p_ref.shape[0]
    w0, a0, k_k, k_a, r_k, ln_g, ln_b = [vec_ref[i:i + 1, :] for i in range(7)]
    seg = seg_ref[...]

    def seg_sum(x):
        return jnp.concatenate(
            [_dot_split_rhs(x[:, i * LANE:(i + 1) * LANE], seg) for i in range(RW_W // LANE)], axis=1)

    p = p_ref[...]
    rows = lax.broadcasted_iota(jnp.int32, (tt, 1), 0)
    first = jnp.where(t_idx > 0, prev_ref[7:8, :], 0.0)
    shifted = jnp.where(rows == 0, first, pltpu.roll(p, 1, 0))
    p = p + mu_ref[...] * (shifted - p)
    r = p[:, 0:RW_W]
    k = p[:, RW_W:2 * RW_W]
    v = p[:, 2 * RW_W:3 * RW_W]
    wl = p[:, 3 * RW_W:3 * RW_W + LANE]
    al = p[:, 3 * RW_W + LANE:3 * RW_W + 2 * LANE]
    gl = p[:, 3 * RW_W + 2 * LANE:RW_COLS]
    w_raw = -_softplus(-(w0 + _bdot(jnp.tanh(wl), w2_ref[...]))) - 0.5
    lw = -jnp.exp(w_raw)
    a = jax.nn.sigmoid(a0 + _bdot(al, a2_ref[...]))
    g = _bdot(jax.nn.sigmoid(gl), g2_ref[...])
    kk = k * k_k
    kk = kk * lax.rsqrt(jnp.maximum(seg_sum(kk * kk), 1e-24))
    k2 = k * (1.0 + (a - 1.0) * k_a)
    bonus_s[...] = seg_sum(r * k2 * r_k) * v
    g_s[...] = g
    r_s[...] = r
    k_s[...] = k2
    v_s[...] = v
    lw_s[...] = lw
    av_s[...] = -kk
    bv_s[...] = kk * a

    c_row = lax.broadcasted_iota(jnp.int32, (CHUNK, CHUNK), 0)
    c_col = lax.broadcasted_iota(jnp.int32, (CHUNK, CHUNK), 1)
    tri = c_col <= c_row
    tri_strict = c_col < c_row
    tri_b = tri.astype(BF16)
    eye = (c_col == c_row).astype(F32)
    lane = lax.broadcasted_iota(jnp.int32, (1, RW_PAIR), 1)
    hmask = [(lane < RW_N).astype(F32), (lane >= RW_N).astype(F32)]
    p_row = lax.broadcasted_iota(jnp.int32, (RW_PAIR, RW_PAIR), 0)
    p_col = lax.broadcasted_iota(jnp.int32, (RW_PAIR, RW_PAIR), 1)
    bd = ((p_row < RW_N) == (p_col < RW_N)).astype(F32)

    n_par = min(RW_PAR_CHUNKS, n_chunks)
    items = [(cc, pi) for cc in range(n_par) for pi in range(RW_NPAIR)]
    r4 = lax.broadcasted_iota(jnp.int32, (CHUNK, 4 * CHUNK), 0)
    c4 = lax.broadcasted_iota(jnp.int32, (CHUNK, 4 * CHUNK), 1) % CHUNK
    tri4, tri_strict4 = c4 <= r4, c4 < r4
    eye4 = (c4 == r4).astype(F32)
    hmask2 = [jnp.concatenate([mk, mk], axis=1) for mk in hmask]
    q_row = lax.broadcasted_iota(jnp.int32, (4 * CHUNK, 4 * CHUNK), 0) // CHUNK
    q_col = lax.broadcasted_iota(jnp.int32, (4 * CHUNK, 4 * CHUNK), 1) // CHUNK
    bd4 = q_row == q_col

    def group(gi, carry):
        base = gi * (n_par * CHUNK)

        def ld(ref):
            return [ref[pl.ds(pl.multiple_of(base + cc * CHUNK, CHUNK), CHUNK),
                        pi * RW_PAIR:(pi + 1) * RW_PAIR] for cc, pi in items]

        rr, kc, vc, lwc, avc, bvc = ld(r_s), ld(k_s), ld(v_s), ld(lw_s), ld(av_s), ld(bv_s)
        st0 = [st_ref[pi] for pi in range(RW_NPAIR)]
        gcum = [_dot_split(tri_b, x) for x in lwc]
        g_last = [g[CHUNK - 1:CHUNK, :] for g in gcum]
        e_neg = [jnp.exp(-g) for g in gcum]
        e_end = [jnp.exp(gl - g) for gl, g in zip(g_last, gcum)]
        r_t = [x * jnp.exp(g) for x, g in zip(rr, gcum)]
        a_t = [x * jnp.exp(g - lw) for x, g, lw in zip(avc, gcum, lwc)]
        b_t = [x * e for x, e in zip(bvc, e_neg)]
        k_t = [x * e for x, e in zip(kc, e_neg)]
        b_bar = [x * e for x, e in zip(bvc, e_end)]
        k_bar = [x * e for x, e in zip(kc, e_end)]

        ar = [jnp.concatenate([a, r], axis=0) for a, r in zip(a_t, r_t)]
        w_rows = [jnp.concatenate([b * hmask[0], b * hmask[1], k * hmask[0], k * hmask[1]], axis=0)
                  for b, k in zip(b_t, k_t)]
        g_all = [_dot_nt(x, w) for x, w in zip(ar, w_rows)]
        l_all = [jnp.where(tri_strict4, g[:CHUNK], 0.0) for g in g_all]
        m_all = [jnp.where(tri4, g[CHUNK:], 0.0) for g in g_all]

        def blockdiag(p4):
            p4 = p4.astype(BF16)
            return jnp.where(bd4, jnp.concatenate([p4, p4, p4, p4], axis=0), jnp.zeros((), BF16))
        pw = [jnp.concatenate([l_all[2 * qd][:, :RW_PAIR], l_all[2 * qd + 1][:, :RW_PAIR]], axis=1)
              for qd in range(len(items) // 2)]
        t_inv = [eye4 + x for x in pw]
        pw_bd = [blockdiag(x) for x in pw]
        n = 2
        while n < CHUNK:
            pw = [jnp.dot(x.astype(BF16), xb, preferred_element_type=F32) for x, xb in zip(pw, pw_bd)]
            pw_bd = [blockdiag(x) for x in pw]
            t_inv = [t + jnp.dot(t.astype(BF16), xb, preferred_element_type=F32)
                     for t, xb in zip(t_inv, pw_bd)]
            n *= 2
        t_pair = [t_inv[it // 2][:, (it % 2) * RW_PAIR:(it % 2 + 1) * RW_PAIR] for it in range(len(items))]

        v_rows = [jnp.concatenate([v * hmask[0], v * hmask[1]], axis=0) for v in vc]
        lm = [_bdot(jnp.concatenate([l[:, RW_PAIR:], mm[:, RW_PAIR:]], axis=0), v)
              for l, mm, v in zip(l_all, m_all, v_rows)]
        x_rows = [jnp.concatenate(
            [jnp.concatenate([a * hmask[h], x[:CHUNK] * hmask[h]], axis=1) for h in range(2)], axis=0)
            for a, x in zip(a_t, lm)]
        x_p = [_bdot(t, x) for t, x in zip(t_pair, x_rows)]
        ry = [_bdot(mm[:, :RW_PAIR], jnp.concatenate([x * hmask2[0], x * hmask2[1]], axis=0))
              for mm, x in zip(m_all, x_p)]
        a_hat = [x[:, :RW_PAIR] for x in x_p]
        u_hat = [x[:, RW_PAIR:] for x in x_p]
        r_eff = [r + y[:, :RW_PAIR] for r, y in zip(r_t, ry)]
        y0 = [y[:, RW_PAIR:] + x[CHUNK:] for y, x in zip(ry, lm)]
        p_t = [_dot_tn(a, b) * bd for a, b in zip(a_hat, b_bar)]
        q_t = [_dot_tn(jnp.concatenate([u, v], axis=0), jnp.concatenate([b, k], axis=0)) * bd
               for u, v, b, k in zip(u_hat, vc, b_bar, k_bar)]
        w_end = [jnp.exp(g) for g in g_last]

        st = list(st0)
        ys = []
        for it, (cc, pi) in enumerate(items):
            stb = st[pi].astype(BF16)
            ys.append(_dot_nt(r_eff[it], stb) + y0[it])
            st[pi] = st[pi] * w_end[it] + _bdot(stb, p_t[it]) + q_t[it]
        for it, (cc, pi) in enumerate(items):
            y_s[pl.ds(pl.multiple_of(base + cc * CHUNK, CHUNK), CHUNK),
                pi * RW_PAIR:(pi + 1) * RW_PAIR] = ys[it]
        for pi in range(RW_NPAIR):
            st_ref[pi] = st[pi]
        return carry

    lax.fori_loop(0, n_chunks // n_par, group, 0)

    y = y_s[...]
    inv_n = 1.0 / RW_N
    mu_y = seg_sum(y) * inv_n
    d = y - mu_y
    var = seg_sum(d * d) * inv_n
    yn = d * lax.rsqrt(var + RW_GN_EPS) * ln_g + ln_b
    o_ref[...] = ((yn + bonus_s[...]) * g_s[...]).astype(o_ref.dtype)


def _dot_split_rhs(x, m01):
    hi = x.astype(BF16)
    lo = (x - hi.astype(F32)).astype(BF16)
    return (jnp.dot(hi, m01, preferred_element_type=F32)
            + jnp.dot(lo, m01, preferred_element_type=F32))


def _rwkv(p_rw, mu, vecs, w2p, a2p, g2p, tt):
    bsz, seq, _ = p_rw.shape
    tt = min(tt, seq)
    seg = (np.arange(LANE)[:, None] // RW_N == np.arange(LANE)[None, :] // RW_N)
    seg = jnp.asarray(seg, BF16)
    row_blocks = tt // 8
    scratch = ([pltpu.VMEM((RW_NPAIR, RW_PAIR, RW_PAIR), F32)]
               + [pltpu.VMEM((tt, RW_W), F32) for _ in range(9)])
    return pl.pallas_call(
        functools.partial(_rwkv_kernel, n_chunks=tt // CHUNK),
        grid=(bsz, seq // tt),
        in_specs=[pl.BlockSpec((None, tt, RW_COLS), lambda b, t: (b, t, 0)),
                  pl.BlockSpec((None, 8, RW_COLS),
                               lambda b, t: (b, jnp.maximum(t * row_blocks - 1, 0), 0)),
                  pl.BlockSpec((1, RW_COLS), lambda b, t: (0, 0)),
                  pl.BlockSpec((8, RW_W), lambda b, t: (0, 0)),
                  pl.BlockSpec((LANE, RW_W), lambda b, t: (0, 0)),
                  pl.BlockSpec((LANE, RW_W), lambda b, t: (0, 0)),
                  pl.BlockSpec((2 * LANE, RW_W), lambda b, t: (0, 0)),
                  pl.BlockSpec((LANE, LANE), lambda b, t: (0, 0))],
        out_specs=pl.BlockSpec((None, tt, RW_W), lambda b, t: (b, t, 0)),
        out_shape=jax.ShapeDtypeStruct((bsz, seq, RW_W), BF16),
        scratch_shapes=scratch,
        compiler_params=_cparams(("parallel", "arbitrary")),
    )(p_rw, p_rw, mu, vecs, w2p, a2p, g2p, seg)


MLA_HD = LANE
MLA_ROPE_OFF = MLA_NOPE
ROLL_ROPE = LANE - MLA_ROPE


def _mla_proj_kernel(x_ref, pos_ref, freq_ref, sgn_ref, win_ref, qg_ref, wq_ref, kvg_ref, wk_ref,
                     wvt_ref, q_ref, k_ref, vt_ref):
    xb = x_ref[...].astype(BF16)
    p = jnp.dot(xb, win_ref[...], preferred_element_type=F32)
    cq = p[:, :MLA_Q_RANK]
    ckv = p[:, MLA_Q_RANK:MLA_Q_RANK + MLA_KV_RANK]
    kpe = p[:, MLA_Q_RANK + MLA_KV_RANK:]
    cqn = cq * lax.rsqrt(jnp.mean(cq * cq, axis=-1, keepdims=True) + RMS_EPS) * qg_ref[...]
    ckvn = ckv * lax.rsqrt(jnp.mean(ckv * ckv, axis=-1, keepdims=True) + RMS_EPS) * kvg_ref[...]
    cqb = cqn.astype(BF16)
    ckvb = ckvn.astype(BF16)

    ang = pos_ref[...].astype(F32) * freq_ref[...]
    lane = lax.broadcasted_iota(jnp.int32, (1, LANE), 1)
    is_rope = (lane >= MLA_ROPE_OFF) & (lane < MLA_ROPE_OFF + MLA_ROPE)
    cos_t = jnp.where(is_rope, jnp.cos(ang), 0.0)
    sin_t = jnp.sin(ang) * sgn_ref[...]
    scale = (MLA_NOPE + MLA_ROPE) ** -0.5 * math.log2(math.e)
    q_c = jnp.where(lane < MLA_NOPE, 1.0, cos_t) * scale
    q_s = sin_t * scale

    k_rot = kpe * cos_t + pltpu.roll(kpe, ROLL_ROPE, 1) * sin_t
    pair = lambda t: jnp.concatenate([t, t], axis=1)
    q_c2, q_s2, k_rot2 = pair(q_c), pair(q_s), pair(k_rot)
    for hp in range(MLA_HEADS // 2):
        cols = slice(2 * hp * MLA_HD, 2 * (hp + 1) * MLA_HD)
        qh = jnp.dot(cqb, wq_ref[:, cols], preferred_element_type=F32)
        q_ref[:, cols] = (qh * q_c2 + pltpu.roll(qh, 2 * MLA_HD - MLA_ROPE, 1) * q_s2).astype(q_ref.dtype)
        kh = jnp.dot(ckvb, wk_ref[:, cols], preferred_element_type=F32)
        k_ref[:, cols] = (kh + k_rot2).astype(k_ref.dtype)
    vt = lax.dot_general(wvt_ref[...], ckvb, (((1,), (1,)), ((), ())),
                         preferred_element_type=F32)
    vrow = lax.broadcasted_iota(jnp.int32, (vt.shape[0], 1), 0)
    vt_ref[...] = (vt + ((vrow % ATT_VROWS) >= MLA_V).astype(F32)).astype(vt_ref.dtype)


def _mla_proj(x2d, pos2d, freq, sgn, w_in, q_g, w_q, kv_g, w_k, w_vt, bsz, seq, tm):
    m, d = x2d.shape
    tm = min(tm, seq)
    nq = MLA_HEADS * MLA_HD
    nv = MLA_HEADS * ATT_VROWS
    spb = seq // tm
    const = lambda shape: pl.BlockSpec(shape, lambda i: (0,) * len(shape))
    return pl.pallas_call(
        _mla_proj_kernel,
        grid=(m // tm,),
        in_specs=[pl.BlockSpec((tm, d), lambda i: (i, 0)),
                  pl.BlockSpec((tm, 1), lambda i: (i, 0)),
                  const((1, LANE)), const((1, LANE)),
                  const(w_in.shape), const((1, MLA_Q_RANK)), const(w_q.shape),
                  const((1, MLA_KV_RANK)), const(w_k.shape), const(w_vt.shape)],
        out_specs=[pl.BlockSpec((tm, nq), lambda i: (i, 0)),
                   pl.BlockSpec((tm, nq), lambda i: (i, 0)),
                   pl.BlockSpec((None, nv, tm), lambda i: (i // spb, 0, i % spb))],
        out_shape=[jax.ShapeDtypeStruct((m, nq), BF16),
                   jax.ShapeDtypeStruct((m, nq), BF16),
                   jax.ShapeDtypeStruct((bsz, nv, seq), BF16)],
        name="mla_proj",
        compiler_params=_cparams(("parallel",)),
    )(x2d, pos2d, freq, sgn, w_in, q_g.reshape(1, -1), w_q, kv_g.reshape(1, -1), w_k, w_vt)


def _attn_kernel(q_ref, k_ref, vt_ref, o_ref, *, tq, n_heads):
    i = pl.program_id(2)
    q0 = i * tq
    nsub = tq // ATT_SUB
    qhs = [q_ref[:, h * MLA_HD:(h + 1) * MLA_HD] for h in range(n_heads)]
    kr = lax.broadcasted_iota(jnp.int32, (ATT_SUB, ATT_SUB), 0) // CHUNK
    qc = lax.broadcasted_iota(jnp.int32, (ATT_SUB, ATT_SUB), 1) // CHUNK
    diag_ok = kr <= qc

    def scores(k0, h, q_rows):
        return lax.dot_general(k_ref[pl.ds(k0, ATT_SUB), h * MLA_HD:(h + 1) * MLA_HD], q_rows,
                               (((1,), (1,)), ((), ())), preferred_element_type=F32)

    def update(h, k0, m_run, acc, s):
        m_new = jnp.maximum(m_run, jnp.max(s, axis=0, keepdims=True))
        alpha = jnp.exp2(m_run - m_new)
        pr = jnp.exp2(s - m_new).astype(BF16)
        vt = vt_ref[h * ATT_VROWS:(h + 1) * ATT_VROWS, pl.ds(k0, ATT_SUB)]
        return m_new, acc * alpha + jnp.dot(vt, pr, preferred_element_type=F32)

    def full_tile(j, carry):
        k0 = pl.multiple_of(j * tq, tq)
        ss = [lax.dot_general(k_ref[pl.ds(k0, tq), h * MLA_HD:(h + 1) * MLA_HD], qhs[h],
                              (((1,), (1,)), ((), ())), preferred_element_type=F32)
              for h in range(n_heads)]
        cur = list(carry)
        for u in range(nsub):
            for h in range(n_heads):
                cur[h] = update(h, k0 + u * ATT_SUB, *cur[h], ss[h][u * ATT_SUB:(u + 1) * ATT_SUB, :])
        return tuple(cur)

    init = tuple((jnp.full((1, tq), -jnp.inf, F32), jnp.zeros((ATT_VROWS, tq), F32))
                 for _ in range(n_heads))
    cur = list(lax.fori_loop(0, i, full_tile, init))

    ds = [[scores(pl.multiple_of(q0 + u * ATT_SUB, ATT_SUB), h, qhs[h][u * ATT_SUB:, :])
           for h in range(n_heads)] for u in range(nsub)]
    for u in range(nsub):
        c0 = u * ATT_SUB
        k0 = pl.multiple_of(q0 + c0, ATT_SUB)
        for h in range(n_heads):
            m_run, acc = cur[h]
            s = ds[u][h]
            s_diag = jnp.where(diag_ok, s[:, :ATT_SUB], -jnp.inf)
            s = s_diag if u == nsub - 1 else jnp.concatenate([s_diag, s[:, ATT_SUB:]], axis=1)
            m_part, acc_part = update(h, k0, m_run[:, c0:], acc[:, c0:], s)
            if u > 0:
                m_part = jnp.concatenate([m_run[:, :c0], m_part], axis=1)
                acc_part = jnp.concatenate([acc[:, :c0], acc_part], axis=1)
            cur[h] = (m_part, acc_part)

    outs = [acc[:MLA_V, :] / acc[MLA_V:MLA_V + 1, :] for _, acc in cur]
    o_ref[...] = jnp.concatenate(outs, axis=0).T.astype(o_ref.dtype)


def _attention(q, k, vt, tq, n_heads):
    bsz, seq, nq = q.shape
    tq = min(tq, seq)
    width = n_heads * MLA_HD
    return pl.pallas_call(
        functools.partial(_attn_kernel, tq=tq, n_heads=n_heads),
        grid=(bsz, MLA_HEADS // n_heads, seq // tq),
        in_specs=[pl.BlockSpec((None, tq, width), lambda b, h, i: (b, i, h)),
                  pl.BlockSpec((None, seq, width), lambda b, h, i: (b, 0, h)),
                  pl.BlockSpec((None, n_heads * ATT_VROWS, seq), lambda b, h, i: (b, h, 0))],
        out_specs=pl.BlockSpec((None, tq, n_heads * MLA_V), lambda b, h, i: (b, i, h)),
        out_shape=jax.ShapeDtypeStruct((bsz, seq, MLA_HEADS * MLA_V), BF16),
        name="mla_attention",
        compiler_params=_cparams(("parallel", "parallel", "arbitrary")),
    )(q, k, vt)


def _pad_cols(w, width):
    return jnp.pad(w, ((0, 0), (0, width - w.shape[1])))


def _pad_rows(w, height):
    return jnp.pad(w, ((0, height - w.shape[0]), (0, 0)))


def _even_weights(w_in, mu):
    gla_in = 2 * GLA_QK + 2 * GLA_V + GLA_GATE_RANK
    w_gla, w_rw = w_in[:, :gla_in], w_in[:, gla_in:]
    o = 3 * RW_W
    rw_parts = [w_rw[:, :o],
                _pad_cols(w_rw[:, o:o + RW_DECAY_RANK], LANE),
                _pad_cols(w_rw[:, o + RW_DECAY_RANK:o + RW_DECAY_RANK + RW_A_RANK], LANE),
                _pad_cols(w_rw[:, o + RW_DECAY_RANK + RW_A_RANK:], 2 * LANE)]
    mu_parts = [mu[:o],
                jnp.pad(mu[o:o + RW_DECAY_RANK], (0, LANE - RW_DECAY_RANK)),
                jnp.pad(mu[o + RW_DECAY_RANK:o + RW_DECAY_RANK + RW_A_RANK], (0, LANE - RW_A_RANK)),
                jnp.pad(mu[o + RW_DECAY_RANK + RW_A_RANK:], (0, 2 * LANE - RW_GATE_RANK))]
    w_all = jnp.concatenate(rw_parts + [_pad_cols(w_gla, GLA_COLS)], axis=1).astype(BF16)
    return w_all, jnp.concatenate(mu_parts).reshape(1, RW_COLS)


def _mla_weights(w_in, w_q_b, w_kv_b):
    swap = np.arange(MLA_ROPE) ^ 1
    d = w_in.shape[0]
    kpe = w_in[:, MLA_Q_RANK + MLA_KV_RANK:]
    w_in_p = jnp.concatenate([w_in[:, :MLA_Q_RANK + MLA_KV_RANK],
                              jnp.zeros((d, MLA_NOPE), F32), kpe, kpe[:, swap]], axis=1).astype(BF16)
    wq = w_q_b.reshape(MLA_Q_RANK, MLA_HEADS, MLA_NOPE + MLA_ROPE)
    wq = jnp.concatenate([wq, wq[:, :, MLA_NOPE:][:, :, swap]], axis=-1)
    wq = wq.reshape(MLA_Q_RANK, MLA_HEADS * MLA_HD).astype(BF16)
    wkv = w_kv_b.reshape(MLA_KV_RANK, MLA_HEADS, MLA_NOPE + MLA_V)
    wk = jnp.pad(wkv[:, :, :MLA_NOPE], ((0, 0), (0, 0), (0, MLA_HD - MLA_NOPE)))
    wk = wk.reshape(MLA_KV_RANK, MLA_HEADS * MLA_HD).astype(BF16)
    wv = jnp.pad(wkv[:, :, MLA_NOPE:], ((0, 0), (0, 0), (0, ATT_VROWS - MLA_V)))
    wvt = wv.reshape(MLA_KV_RANK, MLA_HEADS * ATT_VROWS).T.astype(BF16)
    return w_in_p, wq, wk, wvt


def _rope_tables():
    inv_freq = ROPE_THETA ** (-jnp.arange(0, MLA_ROPE, 2, dtype=jnp.float32) / MLA_ROPE)
    d = np.arange(MLA_ROPE)
    freq = jnp.zeros((LANE,), F32).at[MLA_ROPE_OFF:MLA_ROPE_OFF + MLA_ROPE].set(inv_freq[d // 2])
    sgn = np.zeros((LANE,), np.float32)
    sgn[MLA_ROPE_OFF:MLA_ROPE_OFF + MLA_ROPE] = np.where(d % 2 == 0, -1.0, 1.0)
    return freq.reshape(1, LANE), jnp.asarray(sgn).reshape(1, LANE)


def kernel(x, positions, even_w_in, gla_gate_w2, gla_gate_b, gla_norm_g, rwkv_mu, rwkv_w0, rwkv_w2,
           rwkv_a0, rwkv_a2, rwkv_g2, rwkv_k_k, rwkv_k_a, rwkv_r_k, rwkv_ln_g, rwkv_ln_b, even_w_out,
           mla_w_in, mla_q_norm_g, mla_w_q_b, mla_kv_norm_g, mla_w_kv_b, mla_w_out,
           ffn_w_gate_up, ffn_w_down, ln_g, ln_b):
    bsz, seq, d = x.shape
    m = bsz * seq
    x2d = x.reshape(m, d)
    wgu = ffn_w_gate_up.astype(BF16)
    wdn = ffn_w_down.astype(BF16)

    w_all, mu_p = _even_weights(even_w_in[0], rwkv_mu[0])
    p_rw, p_gla = _inproj(x2d, w_all, (RW_COLS, GLA_COLS), F32, tm=256)
    o_a = _gla(p_gla.reshape(bsz, seq, GLA_COLS),
               _pad_rows(gla_gate_w2[0], LANE).astype(BF16), gla_gate_b[0], gla_norm_g[0], tt=GLA_TT)
    vecs = jnp.stack([rwkv_w0[0], rwkv_a0[0], rwkv_k_k[0], rwkv_k_a[0], rwkv_r_k[0].reshape(RW_W),
                      rwkv_ln_g[0], rwkv_ln_b[0], jnp.zeros((RW_W,), F32)])
    o_b = _rwkv(p_rw.reshape(bsz, seq, RW_COLS), mu_p, vecs,
                _pad_rows(rwkv_w2[0], LANE).astype(BF16), _pad_rows(rwkv_a2[0], LANE).astype(BF16),
                _pad_rows(rwkv_g2[0], 2 * LANE).astype(BF16), tt=RW_TT)
    w_out = even_w_out[0].astype(BF16)
    x2d = _mix_ffn([o_a.reshape(m, GLA_V), o_b.reshape(m, RW_W)], [w_out[:GLA_V], w_out[GLA_V:]],
                   x2d, ln_g[0, 0], ln_b[0, 0], wgu[0], wdn[0], ln_g[0, 1], ln_b[0, 1],
                   tm=FFN_TM, th=FFN_TH)

    w_in_p, wq, wk, wvt = _mla_weights(mla_w_in[0], mla_w_q_b[0], mla_w_kv_b[0])
    freq, sgn = _rope_tables()
    q, k, vt = _mla_proj(x2d, positions.reshape(m, 1), freq, sgn, w_in_p, mla_q_norm_g[0], wq,
                         mla_kv_norm_g[0], wk, wvt, bsz, seq, tm=MLA_TM)
    nq = MLA_HEADS * MLA_HD
    o = _attention(q.reshape(bsz, seq, nq), k.reshape(bsz, seq, nq), vt, tq=ATT_TQ, n_heads=ATT_HEADS)
    x2d = _mix_ffn([o.reshape(m, MLA_HEADS * MLA_V)], [mla_w_out[0].astype(BF16)],
                   x2d, ln_g[1, 0], ln_b[1, 0], wgu[1], wdn[1], ln_g[1, 1], ln_b[1, 1],
                   tm=FFN_TM, th=FFN_TH)
    return x2d.reshape(bsz, seq, d)
```

```python
import functools
import math

import jax
import jax.numpy as jnp
import numpy as np
from jax import lax
from jax.experimental import pallas as pl
from jax.experimental.pallas import tpu as pltpu

F32 = jnp.float32
BF16 = jnp.bfloat16

DEPTH = 2
CHUNK = 64
DN_ALPHA = (2.0 * DEPTH) ** 0.25
LN_EPS = 1e-5
RMS_EPS = 1e-6

GLA_HEADS = 4
GLA_DK = 64
GLA_DV = 128
GLA_GATE_RANK = 16
GLA_TAU = 16.0
GLA_QK = GLA_HEADS * GLA_DK
GLA_V = GLA_HEADS * GLA_DV
GLA_SUB = 16
GLA_EXP_CLAMP = 60.0

RW_HEADS = 8
RW_N = 64
RW_W = RW_HEADS * RW_N
RW_DECAY_RANK = 64
RW_A_RANK = 64
RW_GATE_RANK = 160
RW_GN_EPS = 64e-5

MLA_HEADS = 16
MLA_NOPE = 64
MLA_ROPE = 32
MLA_V = 64
MLA_Q_RANK = 768
MLA_KV_RANK = 256
ROPE_THETA = 10000.0

LANE = 128
VMEM_LIMIT = 48 * 1024 * 1024
FFN_TM, FFN_TH = 1024, 256
FFN_SPLIT = 2
MLA_TM = 512
ATT_TQ, ATT_HEADS = 1024, 2
ATT_SUB = 256
ATT_VROWS = 80

RW_COLS = 3 * RW_W + LANE + LANE + 2 * LANE
GLA_COLS = 2 * GLA_QK + 2 * GLA_V + LANE


def _cparams(sem):
    return pltpu.CompilerParams(dimension_semantics=sem, vmem_limit_bytes=VMEM_LIMIT)


def _bdot(a, b):
    return jnp.dot(a.astype(BF16), b.astype(BF16), preferred_element_type=F32)


def _dot_nt(a, b):
    return lax.dot_general(a.astype(BF16), b.astype(BF16), (((1,), (1,)), ((), ())),
                           preferred_element_type=F32)


def _dot_tn(a, b):
    return lax.dot_general(a.astype(BF16), b.astype(BF16), (((0,), (0,)), ((), ())),
                           preferred_element_type=F32)


def _dot_split(m01, x):
    hi = x.astype(BF16)
    lo = (x - hi.astype(F32)).astype(BF16)
    m = m01.astype(BF16)
    return (jnp.dot(m, hi, preferred_element_type=F32)
            + jnp.dot(m, lo, preferred_element_type=F32))


def _softplus(x):
    return jnp.maximum(x, 0.0) + jnp.log(1.0 + jnp.exp(-jnp.abs(x)))


def _layer_norm(y, g, b):
    mu = jnp.mean(y, axis=-1, keepdims=True)
    d = y - mu
    var = jnp.mean(d * d, axis=-1, keepdims=True)
    return d * lax.rsqrt(var + LN_EPS) * g + b


def _inproj_kernel(x_ref, w_ref, *o_refs, widths, tn):
    xb = x_ref[...].astype(BF16)
    off = 0
    for o_ref, width in zip(o_refs, widths):
        for j in range(0, width, tn):
            w = min(tn, width - j)
            o_ref[:, j:j + w] = jnp.dot(xb, w_ref[:, off + j:off + j + w],
                                        preferred_element_type=F32).astype(o_ref.dtype)
        off += width


def _inproj(x2d, w_bf16, widths, out_dtype, tm):
    m, k = x2d.shape
    tm = min(tm, m)
    n = sum(widths)
    return pl.pallas_call(
        functools.partial(_inproj_kernel, widths=tuple(widths), tn=512),
        grid=(m // tm,),
        in_specs=[pl.BlockSpec((tm, k), lambda i: (i, 0)),
                  pl.BlockSpec((k, n), lambda i: (0, 0))],
        out_specs=[pl.BlockSpec((tm, wd), lambda i: (i, 0)) for wd in widths],
        out_shape=[jax.ShapeDtypeStruct((m, wd), out_dtype) for wd in widths],
        compiler_params=_cparams(("parallel",)),
    )(x2d, w_bf16)


def _mix_ffn_kernel(*refs, n_in):
    a_refs = refs[:n_in]
    wo_refs = refs[n_in:2 * n_in]
    (x_ref, g0_ref, b0_ref, wg_ref, wu_ref, wd_ref, g1_ref, b1_ref, o_ref,
     x1_ref, xb_ref, acc_ref) = refs[2 * n_in:]
    j = pl.program_id(1)
    last = pl.num_programs(1) - 1
    rows_per = x_ref.shape[0] // FFN_SPLIT
    groups = [slice(r * rows_per, (r + 1) * rows_per) for r in range(FFN_SPLIT)]

    def ffn(xb):
        gate = jnp.dot(xb, wg_ref[...], preferred_element_type=F32)
        up = jnp.dot(xb, wu_ref[...], preferred_element_type=F32)
        h = (gate * jax.nn.sigmoid(gate) * up).astype(BF16)
        return jnp.dot(h, wd_ref[...], preferred_element_type=F32)

    @pl.when(j == 0)
    def _():
        for rows in groups:
            y = DN_ALPHA * x_ref[rows, :]
            for a_ref, w_ref in zip(a_refs, wo_refs):
                y = y + jnp.dot(a_ref[rows, :], w_ref[...], preferred_element_type=F32)
            x1 = _layer_norm(y, g0_ref[...], b0_ref[...])
            xb = x1.astype(BF16)
            x1_ref[rows, :] = x1
            xb_ref[rows, :] = xb
            acc_ref[rows, :] = ffn(xb)

    @pl.when((j > 0) & (j < last))
    def _():
        acc_ref[...] += ffn(xb_ref[...])

    @pl.when(j == last)
    def _():
        for rows in groups:
            y = DN_ALPHA * x1_ref[rows, :] + acc_ref[rows, :] + ffn(xb_ref[rows, :])
            o_ref[rows, :] = _layer_norm(y, g1_ref[...], b1_ref[...])


def _mix_ffn(acts, w_outs, x2d, g0, b0, w_gate_up, w_down, g1, b1, tm, th):
    m, d = x2d.shape
    hidden = w_down.shape[0]
    tm = min(tm, m)
    nh = hidden // th
    n_in = len(acts)
    vec = pl.BlockSpec((1, d), lambda i, j: (0, 0))
    in_specs = ([pl.BlockSpec((tm, a.shape[1]), lambda i, j: (i, 0)) for a in acts]
                + [pl.BlockSpec(w.shape, lambda i, j: (0, 0)) for w in w_outs]
                + [pl.BlockSpec((tm, d), lambda i, j: (i, 0)), vec, vec,
                   pl.BlockSpec((d, th), lambda i, j: (0, j)),
                   pl.BlockSpec((d, th), lambda i, j: (0, j + nh)),
                   pl.BlockSpec((th, d), lambda i, j: (j, 0)), vec, vec])
    return pl.pallas_call(
        functools.partial(_mix_ffn_kernel, n_in=n_in),
        grid=(m // tm, nh),
        in_specs=in_specs,
        out_specs=pl.BlockSpec((tm, d), lambda i, j: (i, 0)),
        out_shape=jax.ShapeDtypeStruct((m, d), F32),
        scratch_shapes=[pltpu.VMEM((tm, d), F32), pltpu.VMEM((tm, d), BF16), pltpu.VMEM((tm, d), F32)],
        name="mix_ffn",
        compiler_params=_cparams(("parallel", "arbitrary")),
    )(*acts, *w_outs, x2d, g0.reshape(1, d), b0.reshape(1, d), w_gate_up, w_gate_up, w_down,
      g1.reshape(1, d), b1.reshape(1, d))


def _gla_kernel(p_ref, w2_ref, gb_ref, ng_ref, o_ref, st_ref, *, n_chunks):
    @pl.when(pl.program_id(1) == 0)
    def _():
        st_ref[...] = jnp.zeros_like(st_ref)

    c_row = lax.broadcasted_iota(jnp.int32, (CHUNK, CHUNK), 0)
    c_col = lax.broadcasted_iota(jnp.int32, (CHUNK, CHUNK), 1)
    tri = (c_col <= c_row)
    tri_loc = tri & (c_col >= (c_row // GLA_SUB) * GLA_SUB)
    cum_mat = jnp.concatenate([tri, tri_loc], axis=0).astype(BF16)
    lane = lax.broadcasted_iota(jnp.int32, (1, GLA_QK), 1)
    head_masks = [(lane // GLA_DK == h).astype(F32) for h in range(GLA_HEADS)]
    n_sub = CHUNK // GLA_SUB

    n_par = min(GLA_PAR_CHUNKS, n_chunks)
    o_v = 2 * GLA_QK
    o_g = o_v + GLA_V
    o_l = o_g + GLA_V

    def group(gi, carry):
        base = gi * (n_par * CHUNK)
        rows = [pl.ds(pl.multiple_of(base + cc * CHUNK, CHUNK), CHUNK) for cc in range(n_par)]
        q = [p_ref[r, 0:GLA_QK] * (GLA_DK ** -0.5) for r in rows]
        k = [p_ref[r, GLA_QK:o_v] for r in rows]
        vb = [p_ref[r, o_v:o_g].astype(BF16) for r in rows]
        gg = [p_ref[r, o_g:o_l] for r in rows]
        z = [_bdot(p_ref[r, o_l:GLA_COLS], w2_ref[...]) + gb_ref[...] for r in rows]
        log_a = [-_softplus(-x) / GLA_TAU for x in z]
        cums = [_dot_split(cum_mat, x) for x in log_a]
        b = [x[:CHUNK] for x in cums]
        b_last = [x[CHUNK - 1:CHUNK, :] for x in b]
        q_loc = [x * jnp.exp(c[CHUNK:]) for x, c in zip(q, cums)]
        a_blocks = []
        for cc in range(n_par):
            blocks = []
            for i in range(n_sub):
                ref_i = (jnp.zeros((1, GLA_QK), F32) if i == 0
                         else b[cc][i * GLA_SUB - 1:i * GLA_SUB, :])
                k_i = k[cc] * jnp.exp(jnp.minimum(ref_i - b[cc], GLA_EXP_CLAMP))
                q_i = q_loc[cc][i * GLA_SUB:(i + 1) * GLA_SUB, :]
                lhs = jnp.concatenate([q_i * m for m in head_masks], axis=0)
                blocks.append(_dot_nt(lhs, k_i))
            a_blocks.append(blocks)
        o_intra = []
        for cc in range(n_par):
            per_head = []
            for h in range(GLA_HEADS):
                a_h = jnp.concatenate(
                    [blk[h * GLA_SUB:(h + 1) * GLA_SUB, :] for blk in a_blocks[cc]], axis=0)
                a_h = jnp.where(tri, a_h, 0.0)
                per_head.append(_bdot(a_h, vb[cc][:, h * GLA_DV:(h + 1) * GLA_DV]))
            o_intra.append(per_head)
        lhs_inter = [jnp.concatenate([x * jnp.exp(bb) * m for m in head_masks], axis=0)
                     for x, bb in zip(q, b)]
        kv = [_dot_tn(v, x * jnp.exp(bl - bb)) for v, x, bl, bb in zip(vb, k, b_last, b)]
        kv = [sum(x[h * GLA_DV:(h + 1) * GLA_DV, :] * head_masks[h] for h in range(GLA_HEADS)) for x in kv]
        decay = [jnp.exp(x) for x in b_last]

        st = st_ref[...]
        outs = []
        for cc in range(n_par):
            o_inter = _dot_nt(lhs_inter[cc], st)
            st = st * decay[cc] + kv[cc]
            for h in range(GLA_HEADS):
                o_h = o_inter[h * CHUNK:(h + 1) * CHUNK, :] + o_intra[cc][h]
                ms = jnp.mean(o_h * o_h, axis=-1, keepdims=True)
                o_h = o_h * lax.rsqrt(ms + RMS_EPS) * ng_ref[...]
                g_h = gg[cc][:, h * GLA_DV:(h + 1) * GLA_DV]
                outs.append((o_h * (g_h * jax.nn.sigmoid(g_h))).astype(o_ref.dtype))
        for cc in range(n_par):
            for h in range(GLA_HEADS):
                o_ref[rows[cc], h * GLA_DV:(h + 1) * GLA_DV] = outs[cc * GLA_HEADS + h]
        st_ref[...] = st
        return carry

    lax.fori_loop(0, n_chunks // n_par, group, 0)


def _gla(p_gla, w2p, gate_b, norm_g, tt):
    bsz, seq, _ = p_gla.shape
    tt = min(tt, seq)
    return pl.pallas_call(
        functools.partial(_gla_kernel, n_chunks=tt // CHUNK),
        grid=(bsz, seq // tt),
        in_specs=[pl.BlockSpec((None, tt, GLA_COLS), lambda b, t: (b, t, 0)),
                  pl.BlockSpec((LANE, GLA_QK), lambda b, t: (0, 0)),
                  pl.BlockSpec((1, GLA_QK), lambda b, t: (0, 0)),
                  pl.BlockSpec((1, GLA_DV), lambda b, t: (0, 0))],
        out_specs=pl.BlockSpec((None, tt, GLA_V), lambda b, t: (b, t, 0)),
        out_shape=jax.ShapeDtypeStruct((bsz, seq, GLA_V), BF16),
        scratch_shapes=[pltpu.VMEM((GLA_DV, GLA_QK), F32)],
        compiler_params=_cparams(("parallel", "arbitrary")),
    )(p_gla, w2p, gate_b.reshape(1, GLA_QK), norm_g.reshape(1, GLA_DV))


RW_PAIR = 2 * RW_N
RW_NPAIR = RW_HEADS // 2
RW_PAR_CHUNKS = 4
RW_TT = 256
GLA_TT = 512
GLA_PAR_CHUNKS = 8


def _rwkv_kernel(p_ref, prev_ref, mu_ref, vec_ref, w2_ref, a2_ref, g2_ref, seg_ref, o_ref,
                 st_ref, r_s, k_s, v_s, lw_s, av_s, bv_s, y_s, bonus_s, g_s, *, n_chunks):
    t_idx = pl.program_id(1)

    @pl.when(t_idx == 0)
    def _():
        st_ref[...] = jnp.zeros_like(st_ref)

    tt = p_ref.shape[0]
    w0, a0, k_k, k_a, r_k, ln_g, ln_b = [vec_ref[i:i + 1, :] for i in range(7)]
    seg = seg_ref[...]

    def seg_sum(x):
        return jnp.concatenate(
            [_dot_split_rhs(x[:, i * LANE:(i + 1) * LANE], seg) for i in range(RW_W // LANE)], axis=1)

    p = p_ref[...]
    rows = lax.broadcasted_iota(jnp.int32, (tt, 1), 0)
    first = jnp.where(t_idx > 0, prev_ref[7:8, :], 0.0)
    shifted = jnp.where(rows == 0, first, pltpu.roll(p, 1, 0))
    p = p + mu_ref[...] * (shifted - p)
    r = p[:, 0:RW_W]
    k = p[:, RW_W:2 * RW_W]
    v = p[:, 2 * RW_W:3 * RW_W]
    wl = p[:, 3 * RW_W:3 * RW_W + LANE]
    al = p[:, 3 * RW_W + LANE:3 * RW_W + 2 * LANE]
    gl = p[:, 3 * RW_W + 2 * LANE:RW_COLS]
    w_raw = -_softplus(-(w0 + _bdot(jnp.tanh(wl), w2_ref[...]))) - 0.5
    lw = -jnp.exp(w_raw)
    a = jax.nn.sigmoid(a0 + _bdot(al, a2_ref[...]))
    g = _bdot(jax.nn.sigmoid(gl), g2_ref[...])
    kk = k * k_k
    kk = kk * lax.rsqrt(jnp.maximum(seg_sum(kk * kk), 1e-24))
    k2 = k * (1.0 + (a - 1.0) * k_a)
    bonus_s[...] = seg_sum(r * k2 * r_k) * v
    g_s[...] = g
    r_s[...] = r
    k_s[...] = k2
    v_s[...] = v
    lw_s[...] = lw
    av_s[...] = -kk
    bv_s[...] = kk * a

    c_row = lax.broadcasted_iota(jnp.int32, (CHUNK, CHUNK), 0)
    c_col = lax.broadcasted_iota(jnp.int32, (CHUNK, CHUNK), 1)
    tri = c_col <= c_row
    tri_strict = c_col < c_row
    tri_b = tri.astype(BF16)
    eye = (c_col == c_row).astype(F32)
    lane = lax.broadcasted_iota(jnp.int32, (1, RW_PAIR), 1)
    hmask = [(lane < RW_N).astype(F32), (lane >= RW_N).astype(F32)]
    p_row = lax.broadcasted_iota(jnp.int32, (RW_PAIR, RW_PAIR), 0)
    p_col = lax.broadcasted_iota(jnp.int32, (RW_PAIR, RW_PAIR), 1)
    bd = ((p_row < RW_N) == (p_col < RW_N)).astype(F32)

    n_par = min(RW_PAR_CHUNKS, n_chunks)
    items = [(cc, pi) for cc in range(n_par) for pi in range(RW_NPAIR)]
    r4 = lax.broadcasted_iota(jnp.int32, (CHUNK, 4 * CHUNK), 0)
    c4 = lax.broadcasted_iota(jnp.int32, (CHUNK, 4 * CHUNK), 1) % CHUNK
    tri4, tri_strict4 = c4 <= r4, c4 < r4
    eye4 = (c4 == r4).astype(F32)
    hmask2 = [jnp.concatenate([mk, mk], axis=1) for mk in hmask]
    q_row = lax.broadcasted_iota(jnp.int32, (4 * CHUNK, 4 * CHUNK), 0) // CHUNK
    q_col = lax.broadcasted_iota(jnp.int32, (4 * CHUNK, 4 * CHUNK), 1) // CHUNK
    bd4 = q_row == q_col

    def group(gi, carry):
        base = gi * (n_par * CHUNK)

        def ld(ref):
            return [ref[pl.ds(pl.multiple_of(base + cc * CHUNK, CHUNK), CHUNK),
                        pi * RW_PAIR:(pi + 1) * RW_PAIR] for cc, pi in items]

        rr, kc, vc, lwc, avc, bvc = ld(r_s), ld(k_s), ld(v_s), ld(lw_s), ld(av_s), ld(bv_s)
        st0 = [st_ref[pi] for pi in range(RW_NPAIR)]
        gcum = [_dot_split(tri_b, x) for x in lwc]
        g_last = [g[CHUNK - 1:CHUNK, :] for g in gcum]
        e_neg = [jnp.exp(-g) for g in gcum]
        e_end = [jnp.exp(gl - g) for gl, g in zip(g_last, gcum)]
        r_t = [x * jnp.exp(g) for x, g in zip(rr, gcum)]
        a_t = [x * jnp.exp(g - lw) for x, g, lw in zip(avc, gcum, lwc)]
        b_t = [x * e for x, e in zip(bvc, e_neg)]
        k_t = [x * e for x, e in zip(kc, e_neg)]
        b_bar = [x * e for x, e in zip(bvc, e_end)]
        k_bar = [x * e for x, e in zip(kc, e_end)]

        ar = [jnp.concatenate([a, r], axis=0) for a, r in zip(a_t, r_t)]
        w_rows = [jnp.concatenate([b * hmask[0], b * hmask[1], k * hmask[0], k * hmask[1]], axis=0)
                  for b, k in zip(b_t, k_t)]
        g_all = [_dot_nt(x, w) for x, w in zip(ar, w_rows)]
        l_all = [jnp.where(tri_strict4, g[:CHUNK], 0.0) for g in g_all]
        m_all = [jnp.where(tri4, g[CHUNK:], 0.0) for g in g_all]

        def blockdiag(p4):
            p4 = p4.astype(BF16)
            return jnp.where(bd4, jnp.concatenate([p4, p4, p4, p4], axis=0), jnp.zeros((), BF16))
        pw = [jnp.concatenate([l_all[2 * qd][:, :RW_PAIR], l_all[2 * qd + 1][:, :RW_PAIR]], axis=1)
              for qd in range(len(items) // 2)]
        t_inv = [eye4 + x for x in pw]
        pw_bd = [blockdiag(x) for x in pw]
        n = 2
        while n < CHUNK:
            pw = [jnp.dot(x.astype(BF16), xb, preferred_element_type=F32) for x, xb in zip(pw, pw_bd)]
            pw_bd = [blockdiag(x) for x in pw]
            t_inv = [t + jnp.dot(t.astype(BF16), xb, preferred_element_type=F32)
                     for t, xb in zip(t_inv, pw_bd)]
            n *= 2
        t_pair = [t_inv[it // 2][:, (it % 2) * RW_PAIR:(it % 2 + 1) * RW_PAIR] for it in range(len(items))]

        v_rows = [jnp.concatenate([v * hmask[0], v * hmask[1]], axis=0) for v in vc]
        lm = [_bdot(jnp.concatenate([l[:, RW_PAIR:], mm[:, RW_PAIR:]], axis=0), v)
              for l, mm, v in zip(l_all, m_all, v_rows)]
        x_rows = [jnp.concatenate(
            [jnp.concatenate([a * hmask[h], x[:CHUNK] * hmask[h]], axis=1) for h in range(2)], axis=0)
            for a, x in zip(a_t, lm)]
        x_p = [_bdot(t, x) for t, x in zip(t_pair, x_rows)]
        ry = [_bdot(mm[:, :RW_PAIR], jnp.concatenate([x * hmask2[0], x * hmask2[1]], axis=0))
              for mm, x in zip(m_all, x_p)]
        a_hat = [x[:, :RW_PAIR] for x in x_p]
        u_hat = [x[:, RW_PAIR:] for x in x_p]
        r_eff = [r + y[:, :RW_PAIR] for r, y in zip(r_t, ry)]
        y0 = [y[:, RW_PAIR:] + x[CHUNK:] for y, x in zip(ry, lm)]
        p_t = [_dot_tn(a, b) * bd for a, b in zip(a_hat, b_bar)]
        q_t = [_dot_tn(jnp.concatenate([u, v], axis=0), jnp.concatenate([b, k], axis=0)) * bd
               for u, v, b, k in zip(u_hat, vc, b_bar, k_bar)]
        w_end = [jnp.exp(g) for g in g_last]

        st = list(st0)
        ys = []
        for it, (cc, pi) in enumerate(items):
            stb = st[pi].astype(BF16)
            ys.append(_dot_nt(r_eff[it], stb) + y0[it])
            st[pi] = st[pi] * w_end[it] + _bdot(stb, p_t[it]) + q_t[it]
        for it, (cc, pi) in enumerate(items):
            y_s[pl.ds(pl.multiple_of(base + cc * CHUNK, CHUNK), CHUNK),
                pi * RW_PAIR:(pi + 1) * RW_PAIR] = ys[it]
        for pi in range(RW_NPAIR):
            st_ref[pi] = st[pi]
        return carry

    lax.fori_loop(0, n_chunks // n_par, group, 0)

    y = y_s[...]
    inv_n = 1.0 / RW_N
    mu_y = seg_sum(y) * inv_n
    d = y - mu_y
    var = seg_sum(d * d) * inv_n
    yn = d * lax.rsqrt(var + RW_GN_EPS) * ln_g + ln_b
    o_ref[...] = ((yn + bonus_s[...]) * g_s[...]).astype(o_ref.dtype)


def _dot_split_rhs(x, m01):
    hi = x.astype(BF16)
    lo = (x - hi.astype(F32)).astype(BF16)
    return (jnp.dot(hi, m01, preferred_element_type=F32)
            + jnp.dot(lo, m01, preferred_element_type=F32))


def _rwkv(p_rw, mu, vecs, w2p, a2p, g2p, tt):
    bsz, seq, _ = p_rw.shape
    tt = min(tt, seq)
    seg = (np.arange(LANE)[:, None] // RW_N == np.arange(LANE)[None, :] // RW_N)
    seg = jnp.asarray(seg, BF16)
    row_blocks = tt // 8
    scratch = ([pltpu.VMEM((RW_NPAIR, RW_PAIR, RW_PAIR), F32)]
               + [pltpu.VMEM((tt, RW_W), F32) for _ in range(9)])
    return pl.pallas_call(
        functools.partial(_rwkv_kernel, n_chunks=tt // CHUNK),
        grid=(bsz, seq // tt),
        in_specs=[pl.BlockSpec((None, tt, RW_COLS), lambda b, t: (b, t, 0)),
                  pl.BlockSpec((None, 8, RW_COLS),
                               lambda b, t: (b, jnp.maximum(t * row_blocks - 1, 0), 0)),
                  pl.BlockSpec((1, RW_COLS), lambda b, t: (0, 0)),
                  pl.BlockSpec((8, RW_W), lambda b, t: (0, 0)),
                  pl.BlockSpec((LANE, RW_W), lambda b, t: (0, 0)),
                  pl.BlockSpec((LANE, RW_W), lambda b, t: (0, 0)),
                  pl.BlockSpec((2 * LANE, RW_W), lambda b, t: (0, 0)),
                  pl.BlockSpec((LANE, LANE), lambda b, t: (0, 0))],
        out_specs=pl.BlockSpec((None, tt, RW_W), lambda b, t: (b, t, 0)),
        out_shape=jax.ShapeDtypeStruct((bsz, seq, RW_W), BF16),
        scratch_shapes=scratch,
        compiler_params=_cparams(("parallel", "arbitrary")),
    )(p_rw, p_rw, mu, vecs, w2p, a2p, g2p, seg)


MLA_HD = LANE
MLA_ROPE_OFF = MLA_NOPE
ROLL_ROPE = LANE - MLA_ROPE


def _mla_proj_kernel(x_ref, pos_ref, freq_ref, sgn_ref, win_ref, qg_ref, wq_ref, kvg_ref, wk_ref,
                     wvt_ref, q_ref, k_ref, vt_ref):
    xb = x_ref[...].astype(BF16)
    p = jnp.dot(xb, win_ref[...], preferred_element_type=F32)
    cq = p[:, :MLA_Q_RANK]
    ckv = p[:, MLA_Q_RANK:MLA_Q_RANK + MLA_KV_RANK]
    kpe = p[:, MLA_Q_RANK + MLA_KV_RANK:]
    cqn = cq * lax.rsqrt(jnp.mean(cq * cq, axis=-1, keepdims=True) + RMS_EPS) * qg_ref[...]
    ckvn = ckv * lax.rsqrt(jnp.mean(ckv * ckv, axis=-1, keepdims=True) + RMS_EPS) * kvg_ref[...]
    cqb = cqn.astype(BF16)
    ckvb = ckvn.astype(BF16)

    ang = pos_ref[...].astype(F32) * freq_ref[...]
    lane = lax.broadcasted_iota(jnp.int32, (1, LANE), 1)
    is_rope = (lane >= MLA_ROPE_OFF) & (lane < MLA_ROPE_OFF + MLA_ROPE)
    cos_t = jnp.where(is_rope, jnp.cos(ang), 0.0)
    sin_t = jnp.sin(ang) * sgn_ref[...]
    scale = (MLA_NOPE + MLA_ROPE) ** -0.5 * math.log2(math.e)
    q_c = jnp.where(lane < MLA_NOPE, 1.0, cos_t) * scale
    q_s = sin_t * scale

    k_rot = kpe * cos_t + pltpu.roll(kpe, ROLL_ROPE, 1) * sin_t
    pair = lambda t: jnp.concatenate([t, t], axis=1)
    q_c2, q_s2, k_rot2 = pair(q_c), pair(q_s), pair(k_rot)
    for hp in range(MLA_HEADS // 2):
        cols = slice(2 * hp * MLA_HD, 2 * (hp + 1) * MLA_HD)
        qh = jnp.dot(cqb, wq_ref[:, cols], preferred_element_type=F32)
        q_ref[:, cols] = (qh * q_c2 + pltpu.roll(qh, 2 * MLA_HD - MLA_ROPE, 1) * q_s2).astype(q_ref.dtype)
        kh = jnp.dot(ckvb, wk_ref[:, cols], preferred_element_type=F32)
        k_ref[:, cols] = (kh + k_rot2).astype(k_ref.dtype)
    vt = lax.dot_general(wvt_ref[...], ckvb, (((1,), (1,)), ((), ())),
                         preferred_element_type=F32)
    vrow = lax.broadcasted_iota(jnp.int32, (vt.shape[0], 1), 0)
    vt_ref[...] = (vt + ((vrow % ATT_VROWS) >= MLA_V).astype(F32)).astype(vt_ref.dtype)


def _mla_proj(x2d, pos2d, freq, sgn, w_in, q_g, w_q, kv_g, w_k, w_vt, bsz, seq, tm):
    m, d = x2d.shape
    tm = min(tm, seq)
    nq = MLA_HEADS * MLA_HD
    nv = MLA_HEADS * ATT_VROWS
    spb = seq // tm
    const = lambda shape: pl.BlockSpec(shape, lambda i: (0,) * len(shape))
    return pl.pallas_call(
        _mla_proj_kernel,
        grid=(m // tm,),
        in_specs=[pl.BlockSpec((tm, d), lambda i: (i, 0)),
                  pl.BlockSpec((tm, 1), lambda i: (i, 0)),
                  const((1, LANE)), const((1, LANE)),
                  const(w_in.shape), const((1, MLA_Q_RANK)), const(w_q.shape),
                  const((1, MLA_KV_RANK)), const(w_k.shape), const(w_vt.shape)],
        out_specs=[pl.BlockSpec((tm, nq), lambda i: (i, 0)),
                   pl.BlockSpec((tm, nq), lambda i: (i, 0)),
                   pl.BlockSpec((None, nv, tm), lambda i: (i // spb, 0, i % spb))],
        out_shape=[jax.ShapeDtypeStruct((m, nq), BF16),
                   jax.ShapeDtypeStruct((m, nq), BF16),
                   jax.ShapeDtypeStruct((bsz, nv, seq), BF16)],
        name="mla_proj",
        compiler_params=_cparams(("parallel",)),
    )(x2d, pos2d, freq, sgn, w_in, q_g.reshape(1, -1), w_q, kv_g.reshape(1, -1), w_k, w_vt)


def _attn_kernel(q_ref, k_ref, vt_ref, o_ref, *, tq, n_heads):
    i = pl.program_id(2)
    q0 = i * tq
    nsub = tq // ATT_SUB
    qhs = [q_ref[:, h * MLA_HD:(h + 1) * MLA_HD] for h in range(n_heads)]
    kr = lax.broadcasted_iota(jnp.int32, (ATT_SUB, ATT_SUB), 0) // CHUNK
    qc = lax.broadcasted_iota(jnp.int32, (ATT_SUB, ATT_SUB), 1) // CHUNK
    diag_ok = kr <= qc

    def scores(k0, h, q_rows):
        return lax.dot_general(k_ref[pl.ds(k0, ATT_SUB), h * MLA_HD:(h + 1) * MLA_HD], q_rows,
                               (((1,), (1,)), ((), ())), preferred_element_type=F32)

    def update(h, k0, m_run, acc, s):
        m_new = jnp.maximum(m_run, jnp.max(s, axis=0, keepdims=True))
        alpha = jnp.exp2(m_run - m_new)
        pr = jnp.exp2(s - m_new).astype(BF16)
        vt = vt_ref[h * ATT_VROWS:(h + 1) * ATT_VROWS, pl.ds(k0, ATT_SUB)]
        return m_new, acc * alpha + jnp.dot(vt, pr, preferred_element_type=F32)

    def full_tile(j, carry):
        k0 = pl.multiple_of(j * tq, tq)
        ss = [lax.dot_general(k_ref[pl.ds(k0, tq), h * MLA_HD:(h + 1) * MLA_HD], qhs[h],
                              (((1,), (1,)), ((), ())), preferred_element_type=F32)
              for h in range(n_heads)]
        cur = list(carry)
        for u in range(nsub):
            for h in range(n_heads):
                cur[h] = update(h, k0 + u * ATT_SUB, *cur[h], ss[h][u * ATT_SUB:(u + 1) * ATT_SUB, :])
        return tuple(cur)

    init = tuple((jnp.full((1, tq), -jnp.inf, F32), jnp.zeros((ATT_VROWS, tq), F32))
                 for _ in range(n_heads))
    cur = list(lax.fori_loop(0, i, full_tile, init))

    ds = [[scores(pl.multiple_of(q0 + u * ATT_SUB, ATT_SUB), h, qhs[h][u * ATT_SUB:, :])
           for h in range(n_heads)] for u in range(nsub)]
    for u in range(nsub):
        c0 = u * ATT_SUB
        k0 = pl.multiple_of(q0 + c0, ATT_SUB)
        for h in range(n_heads):
            m_run, acc = cur[h]
            s = ds[u][h]
            s_diag = jnp.where(diag_ok, s[:, :ATT_SUB], -jnp.inf)
            s = s_diag if u == nsub - 1 else jnp.concatenate([s_diag, s[:, ATT_SUB:]], axis=1)
            m_part, acc_part = update(h, k0, m_run[:, c0:], acc[:, c0:], s)
            if u > 0:
                m_part = jnp.concatenate([m_run[:, :c0], m_part], axis=1)
                acc_part = jnp.concatenate([acc[:, :c0], acc_part], axis=1)
            cur[h] = (m_part, acc_part)

    outs = [acc[:MLA_V, :] / acc[MLA_V:MLA_V + 1, :] for _, acc in cur]
    o_ref[...] = jnp.concatenate(outs, axis=0).T.astype(o_ref.dtype)


def _attention(q, k, vt, tq, n_heads):
    bsz, seq, nq = q.shape
    tq = min(tq, seq)
    width = n_heads * MLA_HD
    return pl.pallas_call(
        functools.partial(_attn_kernel, tq=tq, n_heads=n_heads),
        grid=(bsz, MLA_HEADS // n_heads, seq // tq),
        in_specs=[pl.BlockSpec((None, tq, width), lambda b, h, i: (b, i, h)),
                  pl.BlockSpec((None, seq, width), lambda b, h, i: (b, 0, h)),
                  pl.BlockSpec((None, n_heads * ATT_VROWS, seq), lambda b, h, i: (b, h, 0))],
        out_specs=pl.BlockSpec((None, tq, n_heads * MLA_V), lambda b, h, i: (b, i, h)),
        out_shape=jax.ShapeDtypeStruct((bsz, seq, MLA_HEADS * MLA_V), BF16),
        name="mla_attention",
        compiler_params=_cparams(("parallel", "parallel", "arbitrary")),
    )(q, k, vt)


def _pad_cols(w, width):
    return jnp.pad(w, ((0, 0), (0, width - w.shape[1])))


def _pad_rows(w, height):
    return jnp.pad(w, ((0, height - w.shape[0]), (0, 0)))


def _even_weights(w_in, mu):
    gla_in = 2 * GLA_QK + 2 * GLA_V + GLA_GATE_RANK
    w_gla, w_rw = w_in[:, :gla_in], w_in[:, gla_in:]
    o = 3 * RW_W
    rw_parts = [w_rw[:, :o],
                _pad_cols(w_rw[:, o:o + RW_DECAY_RANK], LANE),
                _pad_cols(w_rw[:, o + RW_DECAY_RANK:o + RW_DECAY_RANK + RW_A_RANK], LANE),
                _pad_cols(w_rw[:, o + RW_DECAY_RANK + RW_A_RANK:], 2 * LANE)]
    mu_parts = [mu[:o],
                jnp.pad(mu[o:o + RW_DECAY_RANK], (0, LANE - RW_DECAY_RANK)),
                jnp.pad(mu[o + RW_DECAY_RANK:o + RW_DECAY_RANK + RW_A_RANK], (0, LANE - RW_A_RANK)),
                jnp.pad(mu[o + RW_DECAY_RANK + RW_A_RANK:], (0, 2 * LANE - RW_GATE_RANK))]
    w_all = jnp.concatenate(rw_parts + [_pad_cols(w_gla, GLA_COLS)], axis=1).astype(BF16)
    return w_all, jnp.concatenate(mu_parts).reshape(1, RW_COLS)


def _mla_weights(w_in, w_q_b, w_kv_b):
    swap = np.arange(MLA_ROPE) ^ 1
    d = w_in.shape[0]
    kpe = w_in[:, MLA_Q_RANK + MLA_KV_RANK:]
    w_in_p = jnp.concatenate([w_in[:, :MLA_Q_RANK + MLA_KV_RANK],
                              jnp.zeros((d, MLA_NOPE), F32), kpe, kpe[:, swap]], axis=1).astype(BF16)
    wq = w_q_b.reshape(MLA_Q_RANK, MLA_HEADS, MLA_NOPE + MLA_ROPE)
    wq = jnp.concatenate([wq, wq[:, :, MLA_NOPE:][:, :, swap]], axis=-1)
    wq = wq.reshape(MLA_Q_RANK, MLA_HEADS * MLA_HD).astype(BF16)
    wkv = w_kv_b.reshape(MLA_KV_RANK, MLA_HEADS, MLA_NOPE + MLA_V)
    wk = jnp.pad(wkv[:, :, :MLA_NOPE], ((0, 0), (0, 0), (0, MLA_HD - MLA_NOPE)))
    wk = wk.reshape(MLA_KV_RANK, MLA_HEADS * MLA_HD).astype(BF16)
    wv = jnp.pad(wkv[:, :, MLA_NOPE:], ((0, 0), (0, 0), (0, ATT_VROWS - MLA_V)))
    wvt = wv.reshape(MLA_KV_RANK, MLA_HEADS * ATT_VROWS).T.astype(BF16)
    return w_in_p, wq, wk, wvt


def _rope_tables():
    inv_freq = ROPE_THETA ** (-jnp.arange(0, MLA_ROPE, 2, dtype=jnp.float32) / MLA_ROPE)
    d = np.arange(MLA_ROPE)
    freq = jnp.zeros((LANE,), F32).at[MLA_ROPE_OFF:MLA_ROPE_OFF + MLA_ROPE].set(inv_freq[d // 2])
    sgn = np.zeros((LANE,), np.float32)
    sgn[MLA_ROPE_OFF:MLA_ROPE_OFF + MLA_ROPE] = np.where(d % 2 == 0, -1.0, 1.0)
    return freq.reshape(1, LANE), jnp.asarray(sgn).reshape(1, LANE)


def kernel(x, positions, even_w_in, gla_gate_w2, gla_gate_b, gla_norm_g, rwkv_mu, rwkv_w0, rwkv_w2,
           rwkv_a0, rwkv_a2, rwkv_g2, rwkv_k_k, rwkv_k_a, rwkv_r_k, rwkv_ln_g, rwkv_ln_b, even_w_out,
           mla_w_in, mla_q_norm_g, mla_w_q_b, mla_kv_norm_g, mla_w_kv_b, mla_w_out,
           ffn_w_gate_up, ffn_w_down, ln_g, ln_b):
    bsz, seq, d = x.shape
    m = bsz * seq
    x2d = x.reshape(m, d)
    wgu = ffn_w_gate_up.astype(BF16)
    wdn = ffn_w_down.astype(BF16)

    w_all, mu_p = _even_weights(even_w_in[0], rwkv_mu[0])
    p_rw, p_gla = _inproj(x2d, w_all, (RW_COLS, GLA_COLS), F32, tm=256)
    o_a = _gla(p_gla.reshape(bsz, seq, GLA_COLS),
               _pad_rows(gla_gate_w2[0], LANE).astype(BF16), gla_gate_b[0], gla_norm_g[0], tt=GLA_TT)
    vecs = jnp.stack([rwkv_w0[0], rwkv_a0[0], rwkv_k_k[0], rwkv_k_a[0], rwkv_r_k[0].reshape(RW_W),
                      rwkv_ln_g[0], rwkv_ln_b[0], jnp.zeros((RW_W,), F32)])
    o_b = _rwkv(p_rw.reshape(bsz, seq, RW_COLS), mu_p, vecs,
                _pad_rows(rwkv_w2[0], LANE).astype(BF16), _pad_rows(rwkv_a2[0], LANE).astype(BF16),
                _pad_rows(rwkv_g2[0], 2 * LANE).astype(BF16), tt=RW_TT)
    w_out = even_w_out[0].astype(BF16)
    x2d = _mix_ffn([o_a.reshape(m, GLA_V), o_b.reshape(m, RW_W)], [w_out[:GLA_V], w_out[GLA_V:]],
                   x2d, ln_g[0, 0], ln_b[0, 0], wgu[0], wdn[0], ln_g[0, 1], ln_b[0, 1],
                   tm=FFN_TM, th=FFN_TH)

    w_in_p, wq, wk, wvt = _mla_weights(mla_w_in[0], mla_w_q_b[0], mla_w_kv_b[0])
    freq, sgn = _rope_tables()
    q, k, vt = _mla_proj(x2d, positions.reshape(m, 1), freq, sgn, w_in_p, mla_q_norm_g[0], wq,
                         mla_kv_norm_g[0], wk, wvt, bsz, seq, tm=MLA_TM)
    nq = MLA_HEADS * MLA_HD
    o = _attention(q.reshape(bsz, seq, nq), k.reshape(bsz, seq, nq), vt, tq=ATT_TQ, n_heads=ATT_HEADS)
    x2d = _mix_ffn([o.reshape(m, MLA_HEADS * MLA_V)], [mla_w_out[0].astype(BF16)],
                   x2d, ln_g[1, 0], ln_b[1, 0], wgu[1], wdn[1], ln_g[1, 1], ln_b[1, 1],
                   tm=FFN_TM, th=FFN_TH)
    return x2d.reshape(bsz, seq, d)
```

```python
import functools
import math

import jax
import jax.numpy as jnp
import numpy as np
from jax import lax
from jax.experimental import pallas as pl
from jax.experimental.pallas import tpu as pltpu

F32 = jnp.float32
BF16 = jnp.bfloat16

DEPTH = 2
CHUNK = 64
DN_ALPHA = (2.0 * DEPTH) ** 0.25
LN_EPS = 1e-5
RMS_EPS = 1e-6

GLA_HEADS = 4
GLA_DK = 64
GLA_DV = 128
GLA_GATE_RANK = 16
GLA_TAU = 16.0
GLA_QK = GLA_HEADS * GLA_DK
GLA_V = GLA_HEADS * GLA_DV
GLA_SUB = 16
GLA_EXP_CLAMP = 60.0

RW_HEADS = 8
RW_N = 64
RW_W = RW_HEADS * RW_N
RW_DECAY_RANK = 64
RW_A_RANK = 64
RW_GATE_RANK = 160
RW_GN_EPS = 64e-5

MLA_HEADS = 16
MLA_NOPE = 64
MLA_ROPE = 32
MLA_V = 64
MLA_Q_RANK = 768
MLA_KV_RANK = 256
ROPE_THETA = 10000.0

LANE = 128
VMEM_LIMIT = 48 * 1024 * 1024
FFN_TM, FFN_TH = 1024, 256
FFN_SPLIT = 2
MLA_TM = 512
ATT_TQ, ATT_HEADS = 1024, 4
ATT_SUB = 256
ATT_VROWS = 80

RW_COLS = 3 * RW_W + LANE + LANE + 2 * LANE
GLA_COLS = 2 * GLA_QK + 2 * GLA_V + LANE


def _cparams(sem):
    return pltpu.CompilerParams(dimension_semantics=sem, vmem_limit_bytes=VMEM_LIMIT)


def _bdot(a, b):
    return jnp.dot(a.astype(BF16), b.astype(BF16), preferred_element_type=F32)


def _dot_nt(a, b):
    return lax.dot_general(a.astype(BF16), b.astype(BF16), (((1,), (1,)), ((), ())),
                           preferred_element_type=F32)


def _dot_tn(a, b):
    return lax.dot_general(a.astype(BF16), b.astype(BF16), (((0,), (0,)), ((), ())),
                           preferred_element_type=F32)


def _dot_split(m01, x):
    hi = x.astype(BF16)
    lo = (x - hi.astype(F32)).astype(BF16)
    m = m01.astype(BF16)
    return (jnp.dot(m, hi, preferred_element_type=F32)
            + jnp.dot(m, lo, preferred_element_type=F32))


def _softplus(x):
    return jnp.maximum(x, 0.0) + jnp.log(1.0 + jnp.exp(-jnp.abs(x)))


def _layer_norm(y, g, b):
    mu = jnp.mean(y, axis=-1, keepdims=True)
    d = y - mu
    var = jnp.mean(d * d, axis=-1, keepdims=True)
    return d * lax.rsqrt(var + LN_EPS) * g + b


def _inproj_kernel(x_ref, w_ref, *o_refs, widths, tn):
    xb = x_ref[...].astype(BF16)
    off = 0
    for o_ref, width in zip(o_refs, widths):
        for j in range(0, width, tn):
            w = min(tn, width - j)
            o_ref[:, j:j + w] = jnp.dot(xb, w_ref[:, off + j:off + j + w],
                                        preferred_element_type=F32).astype(o_ref.dtype)
        off += width


def _inproj(x2d, w_bf16, widths, out_dtype, tm):
    m, k = x2d.shape
    tm = min(tm, m)
    n = sum(widths)
    return pl.pallas_call(
        functools.partial(_inproj_kernel, widths=tuple(widths), tn=512),
        grid=(m // tm,),
        in_specs=[pl.BlockSpec((tm, k), lambda i: (i, 0)),
                  pl.BlockSpec((k, n), lambda i: (0, 0))],
        out_specs=[pl.BlockSpec((tm, wd), lambda i: (i, 0)) for wd in widths],
        out_shape=[jax.ShapeDtypeStruct((m, wd), out_dtype) for wd in widths],
        compiler_params=_cparams(("parallel",)),
    )(x2d, w_bf16)


def _mix_ffn_kernel(*refs, n_in):
    a_refs = refs[:n_in]
    wo_refs = refs[n_in:2 * n_in]
    (x_ref, g0_ref, b0_ref, wg_ref, wu_ref, wd_ref, g1_ref, b1_ref, o_ref,
     x1_ref, xb_ref, acc_ref) = refs[2 * n_in:]
    j = pl.program_id(1)
    last = pl.num_programs(1) - 1
    rows_per = x_ref.shape[0] // FFN_SPLIT
    groups = [slice(r * rows_per, (r + 1) * rows_per) for r in range(FFN_SPLIT)]

    def ffn(xb):
        gate = jnp.dot(xb, wg_ref[...], preferred_element_type=F32)
        up = jnp.dot(xb, wu_ref[...], preferred_element_type=F32)
        h = (gate * jax.nn.sigmoid(gate) * up).astype(BF16)
        return jnp.dot(h, wd_ref[...], preferred_element_type=F32)

    @pl.when(j == 0)
    def _():
        for rows in groups:
            y = DN_ALPHA * x_ref[rows, :]
            for a_ref, w_ref in zip(a_refs, wo_refs):
                y = y + jnp.dot(a_ref[rows, :], w_ref[...], preferred_element_type=F32)
            x1 = _layer_norm(y, g0_ref[...], b0_ref[...])
            xb = x1.astype(BF16)
            x1_ref[rows, :] = x1
            xb_ref[rows, :] = xb
            acc_ref[rows, :] = ffn(xb)

    @pl.when((j > 0) & (j < last))
    def _():
        acc_ref[...] += ffn(xb_ref[...])

    @pl.when(j == last)
    def _():
        for rows in groups:
            y = DN_ALPHA * x1_ref[rows, :] + acc_ref[rows, :] + ffn(xb_ref[rows, :])
            o_ref[rows, :] = _layer_norm(y, g1_ref[...], b1_ref[...])


def _mix_ffn(acts, w_outs, x2d, g0, b0, w_gate_up, w_down, g1, b1, tm, th):
    m, d = x2d.shape
    hidden = w_down.shape[0]
    tm = min(tm, m)
    nh = hidden // th
    n_in = len(acts)
    vec = pl.BlockSpec((1, d), lambda i, j: (0, 0))
    in_specs = ([pl.BlockSpec((tm, a.shape[1]), lambda i, j: (i, 0)) for a in acts]
                + [pl.BlockSpec(w.shape, lambda i, j: (0, 0)) for w in w_outs]
                + [pl.BlockSpec((tm, d), lambda i, j: (i, 0)), vec, vec,
                   pl.BlockSpec((d, th), lambda i, j: (0, j)),
                   pl.BlockSpec((d, th), lambda i, j: (0, j + nh)),
                   pl.BlockSpec((th, d), lambda i, j: (j, 0)), vec, vec])
    return pl.pallas_call(
        functools.partial(_mix_ffn_kernel, n_in=n_in),
        grid=(m // tm, nh),
        in_specs=in_specs,
        out_specs=pl.BlockSpec((tm, d), lambda i, j: (i, 0)),
        out_shape=jax.ShapeDtypeStruct((m, d), F32),
        scratch_shapes=[pltpu.VMEM((tm, d), F32), pltpu.VMEM((tm, d), BF16), pltpu.VMEM((tm, d), F32)],
        name="mix_ffn",
        compiler_params=_cparams(("parallel", "arbitrary")),
    )(*acts, *w_outs, x2d, g0.reshape(1, d), b0.reshape(1, d), w_gate_up, w_gate_up, w_down,
      g1.reshape(1, d), b1.reshape(1, d))


def _gla_kernel(p_ref, w2_ref, gb_ref, ng_ref, o_ref, st_ref, *, n_chunks):
    @pl.when(pl.program_id(1) == 0)
    def _():
        st_ref[...] = jnp.zeros_like(st_ref)

    c_row = lax.broadcasted_iota(jnp.int32, (CHUNK, CHUNK), 0)
    c_col = lax.broadcasted_iota(jnp.int32, (CHUNK, CHUNK), 1)
    tri = (c_col <= c_row)
    tri_loc = tri & (c_col >= (c_row // GLA_SUB) * GLA_SUB)
    cum_mat = jnp.concatenate([tri, tri_loc], axis=0).astype(BF16)
    lane = lax.broadcasted_iota(jnp.int32, (1, GLA_QK), 1)
    head_masks = [(lane // GLA_DK == h).astype(F32) for h in range(GLA_HEADS)]
    n_sub = CHUNK // GLA_SUB

    n_par = min(GLA_PAR_CHUNKS, n_chunks)
    o_v = 2 * GLA_QK
    o_g = o_v + GLA_V
    o_l = o_g + GLA_V

    def group(gi, carry):
        base = gi * (n_par * CHUNK)
        rows = [pl.ds(pl.multiple_of(base + cc * CHUNK, CHUNK), CHUNK) for cc in range(n_par)]
        q = [p_ref[r, 0:GLA_QK] * (GLA_DK ** -0.5) for r in rows]
        k = [p_ref[r, GLA_QK:o_v] for r in rows]
        vb = [p_ref[r, o_v:o_g].astype(BF16) for r in rows]
        gg = [p_ref[r, o_g:o_l] for r in rows]
        z = [_bdot(p_ref[r, o_l:GLA_COLS], w2_ref[...]) + gb_ref[...] for r in rows]
        log_a = [-_softplus(-x) / GLA_TAU for x in z]
        cums = [_dot_split(cum_mat, x) for x in log_a]
        b = [x[:CHUNK] for x in cums]
        b_last = [x[CHUNK - 1:CHUNK, :] for x in b]
        q_loc = [x * jnp.exp(c[CHUNK:]) for x, c in zip(q, cums)]
        a_blocks = []
        for cc in range(n_par):
            blocks = []
            for i in range(n_sub):
                ref_i = (jnp.zeros((1, GLA_QK), F32) if i == 0
                         else b[cc][i * GLA_SUB - 1:i * GLA_SUB, :])
                k_i = k[cc] * jnp.exp(jnp.minimum(ref_i - b[cc], GLA_EXP_CLAMP))
                q_i = q_loc[cc][i * GLA_SUB:(i + 1) * GLA_SUB, :]
                lhs = jnp.concatenate([q_i * m for m in head_masks], axis=0)
                blocks.append(_dot_nt(lhs, k_i))
            a_blocks.append(blocks)
        o_intra = []
        for cc in range(n_par):
            per_head = []
            for h in range(GLA_HEADS):
                a_h = jnp.concatenate(
                    [blk[h * GLA_SUB:(h + 1) * GLA_SUB, :] for blk in a_blocks[cc]], axis=0)
                a_h = jnp.where(tri, a_h, 0.0)
                per_head.append(_bdot(a_h, vb[cc][:, h * GLA_DV:(h + 1) * GLA_DV]))
            o_intra.append(per_head)
        lhs_inter = [jnp.concatenate([x * jnp.exp(bb) * m for m in head_masks], axis=0)
                     for x, bb in zip(q, b)]
        kv = [_dot_tn(v, x * jnp.exp(bl - bb)) for v, x, bl, bb in zip(vb, k, b_last, b)]
        kv = [sum(x[h * GLA_DV:(h + 1) * GLA_DV, :] * head_masks[h] for h in range(GLA_HEADS)) for x in kv]
        decay = [jnp.exp(x) for x in b_last]

        st = st_ref[...]
        outs = []
        for cc in range(n_par):
            o_inter = _dot_nt(lhs_inter[cc], st)
            st = st * decay[cc] + kv[cc]
            for h in range(GLA_HEADS):
                o_h = o_inter[h * CHUNK:(h + 1) * CHUNK, :] + o_intra[cc][h]
                ms = jnp.mean(o_h * o_h, axis=-1, keepdims=True)
                o_h = o_h * lax.rsqrt(ms + RMS_EPS) * ng_ref[...]
                g_h = gg[cc][:, h * GLA_DV:(h + 1) * GLA_DV]
                outs.append((o_h * (g_h * jax.nn.sigmoid(g_h))).astype(o_ref.dtype))
        for cc in range(n_par):
            for h in range(GLA_HEADS):
                o_ref[rows[cc], h * GLA_DV:(h + 1) * GLA_DV] = outs[cc * GLA_HEADS + h]
        st_ref[...] = st
        return carry

    lax.fori_loop(0, n_chunks // n_par, group, 0)


def _gla(p_gla, w2p, gate_b, norm_g, tt):
    bsz, seq, _ = p_gla.shape
    tt = min(tt, seq)
    return pl.pallas_call(
        functools.partial(_gla_kernel, n_chunks=tt // CHUNK),
        grid=(bsz, seq // tt),
        in_specs=[pl.BlockSpec((None, tt, GLA_COLS), lambda b, t: (b, t, 0)),
                  pl.BlockSpec((LANE, GLA_QK), lambda b, t: (0, 0)),
                  pl.BlockSpec((1, GLA_QK), lambda b, t: (0, 0)),
                  pl.BlockSpec((1, GLA_DV), lambda b, t: (0, 0))],
        out_specs=pl.BlockSpec((None, tt, GLA_V), lambda b, t: (b, t, 0)),
        out_shape=jax.ShapeDtypeStruct((bsz, seq, GLA_V), BF16),
        scratch_shapes=[pltpu.VMEM((GLA_DV, GLA_QK), F32)],
        compiler_params=_cparams(("parallel", "arbitrary")),
    )(p_gla, w2p, gate_b.reshape(1, GLA_QK), norm_g.reshape(1, GLA_DV))


RW_PAIR = 2 * RW_N
RW_NPAIR = RW_HEADS // 2
RW_PAR_CHUNKS = 4
RW_TT = 256
GLA_TT = 512
GLA_PAR_CHUNKS = 8


def _rwkv_kernel(p_ref, prev_ref, mu_ref, vec_ref, w2_ref, a2_ref, g2_ref, seg_ref, o_ref,
                 st_ref, r_s, k_s, v_s, lw_s, av_s, bv_s, y_s, bonus_s, g_s, *, n_chunks):
    t_idx = pl.program_id(1)

    @pl.when(t_idx == 0)
    def _():
        st_ref[...] = jnp.zeros_like(st_ref)

    tt = p_ref.shape[0]
    w0, a0, k_k, k_a, r_k, ln_g, ln_b = [vec_ref[i:i + 1, :] for i in range(7)]
    seg = seg_ref[...]

    def seg_sum(x):
        return jnp.concatenate(
            [_dot_split_rhs(x[:, i * LANE:(i + 1) * LANE], seg) for i in range(RW_W // LANE)], axis=1)

    p = p_ref[...]
    rows = lax.broadcasted_iota(jnp.int32, (tt, 1), 0)
    first = jnp.where(t_idx > 0, prev_ref[7:8, :], 0.0)
    shifted = jnp.where(rows == 0, first, pltpu.roll(p, 1, 0))
    p = p + mu_ref[...] * (shifted - p)
    r = p[:, 0:RW_W]
    k = p[:, RW_W:2 * RW_W]
    v = p[:, 2 * RW_W:3 * RW_W]
    wl = p[:, 3 * RW_W:3 * RW_W + LANE]
    al = p[:, 3 * RW_W + LANE:3 * RW_W + 2 * LANE]
    gl = p[:, 3 * RW_W + 2 * LANE:RW_COLS]
    w_raw = -_softplus(-(w0 + _bdot(jnp.tanh(wl), w2_ref[...]))) - 0.5
    lw = -jnp.exp(w_raw)
    a = jax.nn.sigmoid(a0 + _bdot(al, a2_ref[...]))
    g = _bdot(jax.nn.sigmoid(gl), g2_ref[...])
    kk = k * k_k
    kk = kk * lax.rsqrt(jnp.maximum(seg_sum(kk * kk), 1e-24))
    k2 = k * (1.0 + (a - 1.0) * k_a)
    bonus_s[...] = seg_sum(r * k2 * r_k) * v
    g_s[...] = g
    r_s[...] = r
    k_s[...] = k2
    v_s[...] = v
    lw_s[...] = lw
    av_s[...] = -kk
    bv_s[...] = kk * a

    c_row = lax.broadcasted_iota(jnp.int32, (CHUNK, CHUNK), 0)
    c_col = lax.broadcasted_iota(jnp.int32, (CHUNK, CHUNK), 1)
    tri = c_col <= c_row
    tri_strict = c_col < c_row
    tri_b = tri.astype(BF16)
    eye = (c_col == c_row).astype(F32)
    lane = lax.broadcasted_iota(jnp.int32, (1, RW_PAIR), 1)
    hmask = [(lane < RW_N).astype(F32), (lane >= RW_N).astype(F32)]
    p_row = lax.broadcasted_iota(jnp.int32, (RW_PAIR, RW_PAIR), 0)
    p_col = lax.broadcasted_iota(jnp.int32, (RW_PAIR, RW_PAIR), 1)
    bd = ((p_row < RW_N) == (p_col < RW_N)).astype(F32)

    n_par = min(RW_PAR_CHUNKS, n_chunks)
    items = [(cc, pi) for cc in range(n_par) for pi in range(RW_NPAIR)]
    r4 = lax.broadcasted_iota(jnp.int32, (CHUNK, 4 * CHUNK), 0)
    c4 = lax.broadcasted_iota(jnp.int32, (CHUNK, 4 * CHUNK), 1) % CHUNK
    tri4, tri_strict4 = c4 <= r4, c4 < r4
    eye4 = (c4 == r4).astype(F32)
    hmask2 = [jnp.concatenate([mk, mk], axis=1) for mk in hmask]
    q_row = lax.broadcasted_iota(jnp.int32, (4 * CHUNK, 4 * CHUNK), 0) // CHUNK
    q_col = lax.broadcasted_iota(jnp.int32, (4 * CHUNK, 4 * CHUNK), 1) // CHUNK
    bd4 = q_row == q_col

    def group(gi, carry):
        base = gi * (n_par * CHUNK)

        def ld(ref):
            return [ref[pl.ds(pl.multiple_of(base + cc * CHUNK, CHUNK), CHUNK),
                        pi * RW_PAIR:(pi + 1) * RW_PAIR] for cc, pi in items]

        rr, kc, vc, lwc, avc, bvc = ld(r_s), ld(k_s), ld(v_s), ld(lw_s), ld(av_s), ld(bv_s)
        st0 = [st_ref[pi] for pi in range(RW_NPAIR)]
        gcum = [_dot_split(tri_b, x) for x in lwc]
        g_last = [g[CHUNK - 1:CHUNK, :] for g in gcum]
        e_neg = [jnp.exp(-g) for g in gcum]
        e_end = [jnp.exp(gl - g) for gl, g in zip(g_last, gcum)]
        r_t = [x * jnp.exp(g) for x, g in zip(rr, gcum)]
        a_t = [x * jnp.exp(g - lw) for x, g, lw in zip(avc, gcum, lwc)]
        b_t = [x * e for x, e in zip(bvc, e_neg)]
        k_t = [x * e for x, e in zip(kc, e_neg)]
        b_bar = [x * e for x, e in zip(bvc, e_end)]
        k_bar = [x * e for x, e in zip(kc, e_end)]

        ar = [jnp.concatenate([a, r], axis=0) for a, r in zip(a_t, r_t)]
        w_rows = [jnp.concatenate([b * hmask[0], b * hmask[1], k * hmask[0], k * hmask[1]], axis=0)
                  for b, k in zip(b_t, k_t)]
        g_all = [_dot_nt(x, w) for x, w in zip(ar, w_rows)]
        l_all = [jnp.where(tri_strict4, g[:CHUNK], 0.0) for g in g_all]
        m_all = [jnp.where(tri4, g[CHUNK:], 0.0) for g in g_all]

        def blockdiag(p4):
            p4 = p4.astype(BF16)
            return jnp.where(bd4, jnp.concatenate([p4, p4, p4, p4], axis=0), jnp.zeros((), BF16))
        pw = [jnp.concatenate([l_all[2 * qd][:, :RW_PAIR], l_all[2 * qd + 1][:, :RW_PAIR]], axis=1)
              for qd in range(len(items) // 2)]
        t_inv = [eye4 + x for x in pw]
        pw_bd = [blockdiag(x) for x in pw]
        n = 2
        while n < CHUNK:
            pw = [jnp.dot(x.astype(BF16), xb, preferred_element_type=F32) for x, xb in zip(pw, pw_bd)]
            pw_bd = [blockdiag(x) for x in pw]
            t_inv = [t + jnp.dot(t.astype(BF16), xb, preferred_element_type=F32)
                     for t, xb in zip(t_inv, pw_bd)]
            n *= 2
        t_pair = [t_inv[it // 2][:, (it % 2) * RW_PAIR:(it % 2 + 1) * RW_PAIR] for it in range(len(items))]

        v_rows = [jnp.concatenate([v * hmask[0], v * hmask[1]], axis=0) for v in vc]
        lm = [_bdot(jnp.concatenate([l[:, RW_PAIR:], mm[:, RW_PAIR:]], axis=0), v)
              for l, mm, v in zip(l_all, m_all, v_rows)]
        x_rows = [jnp.concatenate(
            [jnp.concatenate([a * hmask[h], x[:CHUNK] * hmask[h]], axis=1) for h in range(2)], axis=0)
            for a, x in zip(a_t, lm)]
        x_p = [_bdot(t, x) for t, x in zip(t_pair, x_rows)]
        ry = [_bdot(mm[:, :RW_PAIR], jnp.concatenate([x * hmask2[0], x * hmask2[1]], axis=0))
              for mm, x in zip(m_all, x_p)]
        a_hat = [x[:, :RW_PAIR] for x in x_p]
        u_hat = [x[:, RW_PAIR:] for x in x_p]
        r_eff = [r + y[:, :RW_PAIR] for r, y in zip(r_t, ry)]
        y0 = [y[:, RW_PAIR:] + x[CHUNK:] for y, x in zip(ry, lm)]
        p_t = [_dot_tn(a, b) * bd for a, b in zip(a_hat, b_bar)]
        q_t = [_dot_tn(jnp.concatenate([u, v], axis=0), jnp.concatenate([b, k], axis=0)) * bd
               for u, v, b, k in zip(u_hat, vc, b_bar, k_bar)]
        w_end = [jnp.exp(g) for g in g_last]

        st = list(st0)
        ys = []
        for it, (cc, pi) in enumerate(items):
            stb = st[pi].astype(BF16)
            ys.append(_dot_nt(r_eff[it], stb) + y0[it])
            st[pi] = st[pi] * w_end[it] + _bdot(stb, p_t[it]) + q_t[it]
        for it, (cc, pi) in enumerate(items):
            y_s[pl.ds(pl.multiple_of(base + cc * CHUNK, CHUNK), CHUNK),
                pi * RW_PAIR:(pi + 1) * RW_PAIR] = ys[it]
        for pi in range(RW_NPAIR):
            st_ref[pi] = st[pi]
        return carry

    lax.fori_loop(0, n_chunks // n_par, group, 0)

    y = y_s[...]
    inv_n = 1.0 / RW_N
    mu_y = seg_sum(y) * inv_n
    d = y - mu_y
    var = seg_sum(d * d) * inv_n
    yn = d * lax.rsqrt(var + RW_GN_EPS) * ln_g + ln_b
    o_ref[...] = ((yn + bonus_s[...]) * g_s[...]).astype(o_ref.dtype)


def _dot_split_rhs(x, m01):
    hi = x.astype(BF16)
    lo = (x - hi.astype(F32)).astype(BF16)
    return (jnp.dot(hi, m01, preferred_element_type=F32)
            + jnp.dot(lo, m01, preferred_element_type=F32))


def _rwkv(p_rw, mu, vecs, w2p, a2p, g2p, tt):
    bsz, seq, _ = p_rw.shape
    tt = min(tt, seq)
    seg = (np.arange(LANE)[:, None] // RW_N == np.arange(LANE)[None, :] // RW_N)
    seg = jnp.asarray(seg, BF16)
    row_blocks = tt // 8
    scratch = ([pltpu.VMEM((RW_NPAIR, RW_PAIR, RW_PAIR), F32)]
               + [pltpu.VMEM((tt, RW_W), F32) for _ in range(9)])
    return pl.pallas_call(
        functools.partial(_rwkv_kernel, n_chunks=tt // CHUNK),
        grid=(bsz, seq // tt),
        in_specs=[pl.BlockSpec((None, tt, RW_COLS), lambda b, t: (b, t, 0)),
                  pl.BlockSpec((None, 8, RW_COLS),
                               lambda b, t: (b, jnp.maximum(t * row_blocks - 1, 0), 0)),
                  pl.BlockSpec((1, RW_COLS), lambda b, t: (0, 0)),
                  pl.BlockSpec((8, RW_W), lambda b, t: (0, 0)),
                  pl.BlockSpec((LANE, RW_W), lambda b, t: (0, 0)),
                  pl.BlockSpec((LANE, RW_W), lambda b, t: (0, 0)),
                  pl.BlockSpec((2 * LANE, RW_W), lambda b, t: (0, 0)),
                  pl.BlockSpec((LANE, LANE), lambda b, t: (0, 0))],
        out_specs=pl.BlockSpec((None, tt, RW_W), lambda b, t: (b, t, 0)),
        out_shape=jax.ShapeDtypeStruct((bsz, seq, RW_W), BF16),
        scratch_shapes=scratch,
        compiler_params=_cparams(("parallel", "arbitrary")),
    )(p_rw, p_rw, mu, vecs, w2p, a2p, g2p, seg)


MLA_HD = LANE
MLA_ROPE_OFF = MLA_NOPE
ROLL_ROPE = LANE - MLA_ROPE


def _mla_proj_kernel(x_ref, pos_ref, freq_ref, sgn_ref, win_ref, qg_ref, wq_ref, kvg_ref, wk_ref,
                     wvt_ref, q_ref, k_ref, vt_ref):
    xb = x_ref[...].astype(BF16)
    p = jnp.dot(xb, win_ref[...], preferred_element_type=F32)
    cq = p[:, :MLA_Q_RANK]
    ckv = p[:, MLA_Q_RANK:MLA_Q_RANK + MLA_KV_RANK]
    kpe = p[:, MLA_Q_RANK + MLA_KV_RANK:]
    cqn = cq * lax.rsqrt(jnp.mean(cq * cq, axis=-1, keepdims=True) + RMS_EPS) * qg_ref[...]
    ckvn = ckv * lax.rsqrt(jnp.mean(ckv * ckv, axis=-1, keepdims=True) + RMS_EPS) * kvg_ref[...]
    cqb = cqn.astype(BF16)
    ckvb = ckvn.astype(BF16)

    ang = pos_ref[...].astype(F32) * freq_ref[...]
    lane = lax.broadcasted_iota(jnp.int32, (1, LANE), 1)
    is_rope = (lane >= MLA_ROPE_OFF) & (lane < MLA_ROPE_OFF + MLA_ROPE)
    cos_t = jnp.where(is_rope, jnp.cos(ang), 0.0)
    sin_t = jnp.sin(ang) * sgn_ref[...]
    scale = (MLA_NOPE + MLA_ROPE) ** -0.5 * math.log2(math.e)
    q_c = jnp.where(lane < MLA_NOPE, 1.0, cos_t) * scale
    q_s = sin_t * scale

    k_rot = kpe * cos_t + pltpu.roll(kpe, ROLL_ROPE, 1) * sin_t
    pair = lambda t: jnp.concatenate([t, t], axis=1)
    q_c2, q_s2, k_rot2 = pair(q_c), pair(q_s), pair(k_rot)
    for hp in range(MLA_HEADS // 2):
        cols = slice(2 * hp * MLA_HD, 2 * (hp + 1) * MLA_HD)
        qh = jnp.dot(cqb, wq_ref[:, cols], preferred_element_type=F32)
        q_ref[:, cols] = (qh * q_c2 + pltpu.roll(qh, 2 * MLA_HD - MLA_ROPE, 1) * q_s2).astype(q_ref.dtype)
        kh = jnp.dot(ckvb, wk_ref[:, cols], preferred_element_type=F32)
        k_ref[:, cols] = (kh + k_rot2).astype(k_ref.dtype)
    vt = lax.dot_general(wvt_ref[...], ckvb, (((1,), (1,)), ((), ())),
                         preferred_element_type=F32)
    vrow = lax.broadcasted_iota(jnp.int32, (vt.shape[0], 1), 0)
    vt_ref[...] = (vt + ((vrow % ATT_VROWS) >= MLA_V).astype(F32)).astype(vt_ref.dtype)


def _mla_proj(x2d, pos2d, freq, sgn, w_in, q_g, w_q, kv_g, w_k, w_vt, bsz, seq, tm):
    m, d = x2d.shape
    tm = min(tm, seq)
    nq = MLA_HEADS * MLA_HD
    nv = MLA_HEADS * ATT_VROWS
    spb = seq // tm
    const = lambda shape: pl.BlockSpec(shape, lambda i: (0,) * len(shape))
    return pl.pallas_call(
        _mla_proj_kernel,
        grid=(m // tm,),
        in_specs=[pl.BlockSpec((tm, d), lambda i: (i, 0)),
                  pl.BlockSpec((tm, 1), lambda i: (i, 0)),
                  const((1, LANE)), const((1, LANE)),
                  const(w_in.shape), const((1, MLA_Q_RANK)), const(w_q.shape),
                  const((1, MLA_KV_RANK)), const(w_k.shape), const(w_vt.shape)],
        out_specs=[pl.BlockSpec((tm, nq), lambda i: (i, 0)),
                   pl.BlockSpec((tm, nq), lambda i: (i, 0)),
                   pl.BlockSpec((None, nv, tm), lambda i: (i // spb, 0, i % spb))],
        out_shape=[jax.ShapeDtypeStruct((m, nq), BF16),
                   jax.ShapeDtypeStruct((m, nq), BF16),
                   jax.ShapeDtypeStruct((bsz, nv, seq), BF16)],
        name="mla_proj",
        compiler_params=_cparams(("parallel",)),
    )(x2d, pos2d, freq, sgn, w_in, q_g.reshape(1, -1), w_q, kv_g.reshape(1, -1), w_k, w_vt)


def _attn_kernel(q_ref, k_ref, vt_ref, o_ref, *, tq, n_heads):
    i = pl.program_id(2)
    q0 = i * tq
    nsub = tq // ATT_SUB
    qhs = [q_ref[:, h * MLA_HD:(h + 1) * MLA_HD] for h in range(n_heads)]
    kr = lax.broadcasted_iota(jnp.int32, (ATT_SUB, ATT_SUB), 0) // CHUNK
    qc = lax.broadcasted_iota(jnp.int32, (ATT_SUB, ATT_SUB), 1) // CHUNK
    diag_ok = kr <= qc

    def scores(k0, h, q_rows):
        return lax.dot_general(k_ref[pl.ds(k0, ATT_SUB), h * MLA_HD:(h + 1) * MLA_HD], q_rows,
                               (((1,), (1,)), ((), ())), preferred_element_type=F32)

    def update(h, k0, m_run, acc, s):
        m_new = jnp.maximum(m_run, jnp.max(s, axis=0, keepdims=True))
        alpha = jnp.exp2(m_run - m_new)
        pr = jnp.exp2(s - m_new).astype(BF16)
        vt = vt_ref[h * ATT_VROWS:(h + 1) * ATT_VROWS, pl.ds(k0, ATT_SUB)]
        return m_new, acc * alpha + jnp.dot(vt, pr, preferred_element_type=F32)

    def full_tile(j, carry):
        k0 = pl.multiple_of(j * tq, tq)
        ss = [lax.dot_general(k_ref[pl.ds(k0, tq), h * MLA_HD:(h + 1) * MLA_HD], qhs[h],
                              (((1,), (1,)), ((), ())), preferred_element_type=F32)
              for h in range(n_heads)]
        cur = list(carry)
        for u in range(nsub):
            for h in range(n_heads):
                cur[h] = update(h, k0 + u * ATT_SUB, *cur[h], ss[h][u * ATT_SUB:(u + 1) * ATT_SUB, :])
        return tuple(cur)

    init = tuple((jnp.full((1, tq), -jnp.inf, F32), jnp.zeros((ATT_VROWS, tq), F32))
                 for _ in range(n_heads))
    cur = list(lax.fori_loop(0, i, full_tile, init))

    ds = [[scores(pl.multiple_of(q0 + u * ATT_SUB, ATT_SUB), h, qhs[h][u * ATT_SUB:, :])
           for h in range(n_heads)] for u in range(nsub)]
    for u in range(nsub):
        c0 = u * ATT_SUB
        k0 = pl.multiple_of(q0 + c0, ATT_SUB)
        for h in range(n_heads):
            m_run, acc = cur[h]
            s = ds[u][h]
            s_diag = jnp.where(diag_ok, s[:, :ATT_SUB], -jnp.inf)
            s = s_diag if u == nsub - 1 else jnp.concatenate([s_diag, s[:, ATT_SUB:]], axis=1)
            m_part, acc_part = update(h, k0, m_run[:, c0:], acc[:, c0:], s)
            if u > 0:
                m_part = jnp.concatenate([m_run[:, :c0], m_part], axis=1)
                acc_part = jnp.concatenate([acc[:, :c0], acc_part], axis=1)
            cur[h] = (m_part, acc_part)

    outs = [acc[:MLA_V, :] / acc[MLA_V:MLA_V + 1, :] for _, acc in cur]
    o_ref[...] = jnp.concatenate(outs, axis=0).T.astype(o_ref.dtype)


def _attention(q, k, vt, tq, n_heads):
    bsz, seq, nq = q.shape
    tq = min(tq, seq)
    width = n_heads * MLA_HD
    return pl.pallas_call(
        functools.partial(_attn_kernel, tq=tq, n_heads=n_heads),
        grid=(bsz, MLA_HEADS // n_heads, seq // tq),
        in_specs=[pl.BlockSpec((None, tq, width), lambda b, h, i: (b, i, h)),
                  pl.BlockSpec((None, seq, width), lambda b, h, i: (b, 0, h)),
                  pl.BlockSpec((None, n_heads * ATT_VROWS, seq), lambda b, h, i: (b, h, 0))],
        out_specs=pl.BlockSpec((None, tq, n_heads * MLA_V), lambda b, h, i: (b, i, h)),
        out_shape=jax.ShapeDtypeStruct((bsz, seq, MLA_HEADS * MLA_V), BF16),
        name="mla_attention",
        compiler_params=_cparams(("parallel", "parallel", "arbitrary")),
    )(q, k, vt)


def _pad_cols(w, width):
    return jnp.pad(w, ((0, 0), (0, width - w.shape[1])))


def _pad_rows(w, height):
    return jnp.pad(w, ((0, height - w.shape[0]), (0, 0)))


def _even_weights(w_in, mu):
    gla_in = 2 * GLA_QK + 2 * GLA_V + GLA_GATE_RANK
    w_gla, w_rw = w_in[:, :gla_in], w_in[:, gla_in:]
    o = 3 * RW_W
    rw_parts = [w_rw[:, :o],
                _pad_cols(w_rw[:, o:o + RW_DECAY_RANK], LANE),
                _pad_cols(w_rw[:, o + RW_DECAY_RANK:o + RW_DECAY_RANK + RW_A_RANK], LANE),
                _pad_cols(w_rw[:, o + RW_DECAY_RANK + RW_A_RANK:], 2 * LANE)]
    mu_parts = [mu[:o],
                jnp.pad(mu[o:o + RW_DECAY_RANK], (0, LANE - RW_DECAY_RANK)),
                jnp.pad(mu[o + RW_DECAY_RANK:o + RW_DECAY_RANK + RW_A_RANK], (0, LANE - RW_A_RANK)),
                jnp.pad(mu[o + RW_DECAY_RANK + RW_A_RANK:], (0, 2 * LANE - RW_GATE_RANK))]
    w_all = jnp.concatenate(rw_parts + [_pad_cols(w_gla, GLA_COLS)], axis=1).astype(BF16)
    return w_all, jnp.concatenate(mu_parts).reshape(1, RW_COLS)


def _mla_weights(w_in, w_q_b, w_kv_b):
    swap = np.arange(MLA_ROPE) ^ 1
    d = w_in.shape[0]
    kpe = w_in[:, MLA_Q_RANK + MLA_KV_RANK:]
    w_in_p = jnp.concatenate([w_in[:, :MLA_Q_RANK + MLA_KV_RANK],
                              jnp.zeros((d, MLA_NOPE), F32), kpe, kpe[:, swap]], axis=1).astype(BF16)
    wq = w_q_b.reshape(MLA_Q_RANK, MLA_HEADS, MLA_NOPE + MLA_ROPE)
    wq = jnp.concatenate([wq, wq[:, :, MLA_NOPE:][:, :, swap]], axis=-1)
    wq = wq.reshape(MLA_Q_RANK, MLA_HEADS * MLA_HD).astype(BF16)
    wkv = w_kv_b.reshape(MLA_KV_RANK, MLA_HEADS, MLA_NOPE + MLA_V)
    wk = jnp.pad(wkv[:, :, :MLA_NOPE], ((0, 0), (0, 0), (0, MLA_HD - MLA_NOPE)))
    wk = wk.reshape(MLA_KV_RANK, MLA_HEADS * MLA_HD).astype(BF16)
    wv = jnp.pad(wkv[:, :, MLA_NOPE:], ((0, 0), (0, 0), (0, ATT_VROWS - MLA_V)))
    wvt = wv.reshape(MLA_KV_RANK, MLA_HEADS * ATT_VROWS).T.astype(BF16)
    return w_in_p, wq, wk, wvt


def _rope_tables():
    inv_freq = ROPE_THETA ** (-jnp.arange(0, MLA_ROPE, 2, dtype=jnp.float32) / MLA_ROPE)
    d = np.arange(MLA_ROPE)
    freq = jnp.zeros((LANE,), F32).at[MLA_ROPE_OFF:MLA_ROPE_OFF + MLA_ROPE].set(inv_freq[d // 2])
    sgn = np.zeros((LANE,), np.float32)
    sgn[MLA_ROPE_OFF:MLA_ROPE_OFF + MLA_ROPE] = np.where(d % 2 == 0, -1.0, 1.0)
    return freq.reshape(1, LANE), jnp.asarray(sgn).reshape(1, LANE)


def kernel(x, positions, even_w_in, gla_gate_w2, gla_gate_b, gla_norm_g, rwkv_mu, rwkv_w0, rwkv_w2,
           rwkv_a0, rwkv_a2, rwkv_g2, rwkv_k_k, rwkv_k_a, rwkv_r_k, rwkv_ln_g, rwkv_ln_b, even_w_out,
           mla_w_in, mla_q_norm_g, mla_w_q_b, mla_kv_norm_g, mla_w_kv_b, mla_w_out,
           ffn_w_gate_up, ffn_w_down, ln_g, ln_b):
    bsz, seq, d = x.shape
    m = bsz * seq
    x2d = x.reshape(m, d)
    wgu = ffn_w_gate_up.astype(BF16)
    wdn = ffn_w_down.astype(BF16)

    w_all, mu_p = _even_weights(even_w_in[0], rwkv_mu[0])
    p_rw, p_gla = _inproj(x2d, w_all, (RW_COLS, GLA_COLS), F32, tm=256)
    o_a = _gla(p_gla.reshape(bsz, seq, GLA_COLS),
               _pad_rows(gla_gate_w2[0], LANE).astype(BF16), gla_gate_b[0], gla_norm_g[0], tt=GLA_TT)
    vecs = jnp.stack([rwkv_w0[0], rwkv_a0[0], rwkv_k_k[0], rwkv_k_a[0], rwkv_r_k[0].reshape(RW_W),
                      rwkv_ln_g[0], rwkv_ln_b[0], jnp.zeros((RW_W,), F32)])
    o_b = _rwkv(p_rw.reshape(bsz, seq, RW_COLS), mu_p, vecs,
                _pad_rows(rwkv_w2[0], LANE).astype(BF16), _pad_rows(rwkv_a2[0], LANE).astype(BF16),
                _pad_rows(rwkv_g2[0], 2 * LANE).astype(BF16), tt=RW_TT)
    w_out = even_w_out[0].astype(BF16)
    x2d = _mix_ffn([o_a.reshape(m, GLA_V), o_b.reshape(m, RW_W)], [w_out[:GLA_V], w_out[GLA_V:]],
                   x2d, ln_g[0, 0], ln_b[0, 0], wgu[0], wdn[0], ln_g[0, 1], ln_b[0, 1],
                   tm=FFN_TM, th=FFN_TH)

    w_in_p, wq, wk, wvt = _mla_weights(mla_w_in[0], mla_w_q_b[0], mla_w_kv_b[0])
    freq, sgn = _rope_tables()
    q, k, vt = _mla_proj(x2d, positions.reshape(m, 1), freq, sgn, w_in_p, mla_q_norm_g[0], wq,
                         mla_kv_norm_g[0], wk, wvt, bsz, seq, tm=MLA_TM)
    nq = MLA_HEADS * MLA_HD
    o = _attention(q.reshape(bsz, seq, nq), k.reshape(bsz, seq, nq), vt, tq=ATT_TQ, n_heads=ATT_HEADS)
    x2d = _mix_ffn([o.reshape(m, MLA_HEADS * MLA_V)], [mla_w_out[0].astype(BF16)],
                   x2d, ln_g[1, 0], ln_b[1, 0], wgu[1], wdn[1], ln_g[1, 1], ln_b[1, 1],
                   tm=FFN_TM, th=FFN_TH)
    return x2d.reshape(bsz, seq, d)
```

```python
import functools
import math

import jax
import jax.numpy as jnp
import numpy as np
from jax import lax
from jax.experimental import pallas as pl
from jax.experimental.pallas import tpu as pltpu

F32 = jnp.float32
BF16 = jnp.bfloat16

DEPTH = 2
CHUNK = 64
DN_ALPHA = (2.0 * DEPTH) ** 0.25
LN_EPS = 1e-5
RMS_EPS = 1e-6

GLA_HEADS = 4
GLA_DK = 64
GLA_DV = 128
GLA_GATE_RANK = 16
GLA_TAU = 16.0
GLA_QK = GLA_HEADS * GLA_DK
GLA_V = GLA_HEADS * GLA_DV
GLA_SUB = 16
GLA_EXP_CLAMP = 60.0 * math.log2(math.e)

RW_HEADS = 8
RW_N = 64
RW_W = RW_HEADS * RW_N
RW_DECAY_RANK = 64
RW_A_RANK = 64
RW_GATE_RANK = 160
RW_GN_EPS = 64e-5

MLA_HEADS = 16
MLA_NOPE = 64
MLA_ROPE = 32
MLA_V = 64
MLA_Q_RANK = 768
MLA_KV_RANK = 256
ROPE_THETA = 10000.0

LOG2E = math.log2(math.e)
LANE = 128
VMEM_LIMIT = 48 * 1024 * 1024
FFN_TM, FFN_TH = 1024, 256
FFN_SPLIT = 2
MLA_TM = 512
ATT_TQ, ATT_HEADS = 1024, 4
ATT_SUB = 256
ATT_VROWS = 80

RW_COLS = 3 * RW_W + LANE + LANE + 2 * LANE
GLA_COLS = 2 * GLA_QK + 2 * GLA_V + LANE


def _cparams(sem):
    return pltpu.CompilerParams(dimension_semantics=sem, vmem_limit_bytes=VMEM_LIMIT)


def _bdot(a, b):
    return jnp.dot(a.astype(BF16), b.astype(BF16), preferred_element_type=F32)


def _dot_nt(a, b):
    return lax.dot_general(a.astype(BF16), b.astype(BF16), (((1,), (1,)), ((), ())),
                           preferred_element_type=F32)


def _dot_tn(a, b):
    return lax.dot_general(a.astype(BF16), b.astype(BF16), (((0,), (0,)), ((), ())),
                           preferred_element_type=F32)


def _dot_split(m01, x):
    hi = x.astype(BF16)
    lo = (x - hi.astype(F32)).astype(BF16)
    m = m01.astype(BF16)
    return (jnp.dot(m, hi, preferred_element_type=F32)
            + jnp.dot(m, lo, preferred_element_type=F32))


def _softplus(x):
    return jnp.maximum(x, 0.0) + jnp.log(1.0 + jnp.exp(-jnp.abs(x)))


def _layer_norm(y, g, b):
    mu = jnp.mean(y, axis=-1, keepdims=True)
    d = y - mu
    var = jnp.mean(d * d, axis=-1, keepdims=True)
    return d * lax.rsqrt(var + LN_EPS) * g + b


def _inproj_kernel(x_ref, w_ref, *o_refs, widths, tn):
    xb = x_ref[...].astype(BF16)
    off = 0
    for o_ref, width in zip(o_refs, widths):
        for j in range(0, width, tn):
            w = min(tn, width - j)
            o_ref[:, j:j + w] = jnp.dot(xb, w_ref[:, off + j:off + j + w],
                                        preferred_element_type=F32).astype(o_ref.dtype)
        off += width


def _inproj(x2d, w_bf16, widths, out_dtype, tm):
    m, k = x2d.shape
    tm = min(tm, m)
    n = sum(widths)
    return pl.pallas_call(
        functools.partial(_inproj_kernel, widths=tuple(widths), tn=512),
        grid=(m // tm,),
        in_specs=[pl.BlockSpec((tm, k), lambda i: (i, 0)),
                  pl.BlockSpec((k, n), lambda i: (0, 0))],
        out_specs=[pl.BlockSpec((tm, wd), lambda i: (i, 0)) for wd in widths],
        out_shape=[jax.ShapeDtypeStruct((m, wd), out_dtype) for wd in widths],
        compiler_params=_cparams(("parallel",)),
    )(x2d, w_bf16)


def _mix_ffn_kernel(*refs, n_in):
    a_refs = refs[:n_in]
    wo_refs = refs[n_in:2 * n_in]
    (x_ref, g0_ref, b0_ref, wg_ref, wu_ref, wd_ref, g1_ref, b1_ref, o_ref,
     x1_ref, xb_ref, acc_ref) = refs[2 * n_in:]
    j = pl.program_id(1)
    last = pl.num_programs(1) - 1
    rows_per = x_ref.shape[0] // FFN_SPLIT
    groups = [slice(r * rows_per, (r + 1) * rows_per) for r in range(FFN_SPLIT)]

    def ffn(xb):
        gate = jnp.dot(xb, wg_ref[...], preferred_element_type=F32)
        up = jnp.dot(xb, wu_ref[...], preferred_element_type=F32)
        h = (gate * jax.nn.sigmoid(gate) * up).astype(BF16)
        return jnp.dot(h, wd_ref[...], preferred_element_type=F32)

    @pl.when(j == 0)
    def _():
        for rows in groups:
            y = DN_ALPHA * x_ref[rows, :]
            for a_ref, w_ref in zip(a_refs, wo_refs):
                y = y + jnp.dot(a_ref[rows, :], w_ref[...], preferred_element_type=F32)
            x1 = _layer_norm(y, g0_ref[...], b0_ref[...])
            xb = x1.astype(BF16)
            x1_ref[rows, :] = x1
            xb_ref[rows, :] = xb
            acc_ref[rows, :] = ffn(xb)

    @pl.when((j > 0) & (j < last))
    def _():
        acc_ref[...] += ffn(xb_ref[...])

    @pl.when(j == last)
    def _():
        for rows in groups:
            y = DN_ALPHA * x1_ref[rows, :] + acc_ref[rows, :] + ffn(xb_ref[rows, :])
            o_ref[rows, :] = _layer_norm(y, g1_ref[...], b1_ref[...])


def _mix_ffn(acts, w_outs, x2d, g0, b0, w_gate_up, w_down, g1, b1, tm, th):
    m, d = x2d.shape
    hidden = w_down.shape[0]
    tm = min(tm, m)
    nh = hidden // th
    n_in = len(acts)
    vec = pl.BlockSpec((1, d), lambda i, j: (0, 0))
    in_specs = ([pl.BlockSpec((tm, a.shape[1]), lambda i, j: (i, 0)) for a in acts]
                + [pl.BlockSpec(w.shape, lambda i, j: (0, 0)) for w in w_outs]
                + [pl.BlockSpec((tm, d), lambda i, j: (i, 0)), vec, vec,
                   pl.BlockSpec((d, th), lambda i, j: (0, j)),
                   pl.BlockSpec((d, th), lambda i, j: (0, j + nh)),
                   pl.BlockSpec((th, d), lambda i, j: (j, 0)), vec, vec])
    return pl.pallas_call(
        functools.partial(_mix_ffn_kernel, n_in=n_in),
        grid=(m // tm, nh),
        in_specs=in_specs,
        out_specs=pl.BlockSpec((tm, d), lambda i, j: (i, 0)),
        out_shape=jax.ShapeDtypeStruct((m, d), F32),
        scratch_shapes=[pltpu.VMEM((tm, d), F32), pltpu.VMEM((tm, d), BF16), pltpu.VMEM((tm, d), F32)],
        name="mix_ffn",
        compiler_params=_cparams(("parallel", "arbitrary")),
    )(*acts, *w_outs, x2d, g0.reshape(1, d), b0.reshape(1, d), w_gate_up, w_gate_up, w_down,
      g1.reshape(1, d), b1.reshape(1, d))


def _gla_kernel(p_ref, w2_ref, gb_ref, ng_ref, o_ref, st_ref, *, n_chunks):
    @pl.when(pl.program_id(1) == 0)
    def _():
        st_ref[...] = jnp.zeros_like(st_ref)

    c_row = lax.broadcasted_iota(jnp.int32, (CHUNK, CHUNK), 0)
    c_col = lax.broadcasted_iota(jnp.int32, (CHUNK, CHUNK), 1)
    tri = (c_col <= c_row)
    tri_loc = tri & (c_col >= (c_row // GLA_SUB) * GLA_SUB)
    cum_mat = jnp.concatenate([tri, tri_loc], axis=0).astype(BF16)
    lane = lax.broadcasted_iota(jnp.int32, (1, GLA_QK), 1)
    head_masks = [(lane // GLA_DK == h).astype(F32) for h in range(GLA_HEADS)]
    n_sub = CHUNK // GLA_SUB

    n_par = min(GLA_PAR_CHUNKS, n_chunks)
    o_v = 2 * GLA_QK
    o_g = o_v + GLA_V
    o_l = o_g + GLA_V

    def group(gi, carry):
        base = gi * (n_par * CHUNK)
        rows = [pl.ds(pl.multiple_of(base + cc * CHUNK, CHUNK), CHUNK) for cc in range(n_par)]
        q = [p_ref[r, 0:GLA_QK] * (GLA_DK ** -0.5) for r in rows]
        k = [p_ref[r, GLA_QK:o_v] for r in rows]
        vb = [p_ref[r, o_v:o_g].astype(BF16) for r in rows]
        gg = [p_ref[r, o_g:o_l] for r in rows]
        z = [_bdot(p_ref[r, o_l:GLA_COLS], w2_ref[...]) + gb_ref[...] for r in rows]
        log_a = [-_softplus(-x) * (LOG2E / GLA_TAU) for x in z]
        cums = [_dot_split(cum_mat, x) for x in log_a]
        b = [x[:CHUNK] for x in cums]
        b_last = [x[CHUNK - 1:CHUNK, :] for x in b]
        q_loc = [x * jnp.exp2(c[CHUNK:]) for x, c in zip(q, cums)]
        a_blocks = []
        for cc in range(n_par):
            blocks = []
            for i in range(n_sub):
                ref_i = (jnp.zeros((1, GLA_QK), F32) if i == 0
                         else b[cc][i * GLA_SUB - 1:i * GLA_SUB, :])
                k_i = k[cc] * jnp.exp2(jnp.minimum(ref_i - b[cc], GLA_EXP_CLAMP))
                q_i = q_loc[cc][i * GLA_SUB:(i + 1) * GLA_SUB, :]
                lhs = jnp.concatenate([q_i * m for m in head_masks], axis=0)
                blocks.append(_dot_nt(lhs, k_i))
            a_blocks.append(blocks)
        o_intra = []
        for cc in range(n_par):
            per_head = []
            for h in range(GLA_HEADS):
                a_h = jnp.concatenate(
                    [blk[h * GLA_SUB:(h + 1) * GLA_SUB, :] for blk in a_blocks[cc]], axis=0)
                a_h = jnp.where(tri, a_h, 0.0)
                per_head.append(_bdot(a_h, vb[cc][:, h * GLA_DV:(h + 1) * GLA_DV]))
            o_intra.append(per_head)
        lhs_inter = [jnp.concatenate([x * jnp.exp2(bb) * m for m in head_masks], axis=0)
                     for x, bb in zip(q, b)]
        kv = [_dot_tn(v, x * jnp.exp2(bl - bb)) for v, x, bl, bb in zip(vb, k, b_last, b)]
        kv = [sum(x[h * GLA_DV:(h + 1) * GLA_DV, :] * head_masks[h] for h in range(GLA_HEADS)) for x in kv]
        decay = [jnp.exp2(x) for x in b_last]

        st = st_ref[...]
        outs = []
        for cc in range(n_par):
            o_inter = _dot_nt(lhs_inter[cc], st)
            st = st * decay[cc] + kv[cc]
            for h in range(GLA_HEADS):
                o_h = o_inter[h * CHUNK:(h + 1) * CHUNK, :] + o_intra[cc][h]
                ms = jnp.mean(o_h * o_h, axis=-1, keepdims=True)
                o_h = o_h * lax.rsqrt(ms + RMS_EPS) * ng_ref[...]
                g_h = gg[cc][:, h * GLA_DV:(h + 1) * GLA_DV]
                outs.append((o_h * (g_h * jax.nn.sigmoid(g_h))).astype(o_ref.dtype))
        for cc in range(n_par):
            for h in range(GLA_HEADS):
                o_ref[rows[cc], h * GLA_DV:(h + 1) * GLA_DV] = outs[cc * GLA_HEADS + h]
        st_ref[...] = st
        return carry

    lax.fori_loop(0, n_chunks // n_par, group, 0)


def _gla(p_gla, w2p, gate_b, norm_g, tt):
    bsz, seq, _ = p_gla.shape
    tt = min(tt, seq)
    return pl.pallas_call(
        functools.partial(_gla_kernel, n_chunks=tt // CHUNK),
        grid=(bsz, seq // tt),
        in_specs=[pl.BlockSpec((None, tt, GLA_COLS), lambda b, t: (b, t, 0)),
                  pl.BlockSpec((LANE, GLA_QK), lambda b, t: (0, 0)),
                  pl.BlockSpec((1, GLA_QK), lambda b, t: (0, 0)),
                  pl.BlockSpec((1, GLA_DV), lambda b, t: (0, 0))],
        out_specs=pl.BlockSpec((None, tt, GLA_V), lambda b, t: (b, t, 0)),
        out_shape=jax.ShapeDtypeStruct((bsz, seq, GLA_V), BF16),
        scratch_shapes=[pltpu.VMEM((GLA_DV, GLA_QK), F32)],
        compiler_params=_cparams(("parallel", "arbitrary")),
    )(p_gla, w2p, gate_b.reshape(1, GLA_QK), norm_g.reshape(1, GLA_DV))


RW_PAIR = 2 * RW_N
RW_NPAIR = RW_HEADS // 2
RW_PAR_CHUNKS = 4
RW_TT = 256
GLA_TT = 512
GLA_PAR_CHUNKS = 8


def _rwkv_kernel(p_ref, prev_ref, mu_ref, vec_ref, w2_ref, a2_ref, g2_ref, seg_ref, o_ref,
                 st_ref, r_s, k_s, v_s, lw_s, av_s, bv_s, y_s, bonus_s, g_s, *, n_chunks):
    t_idx = pl.program_id(1)

    @pl.when(t_idx == 0)
    def _():
        st_ref[...] = jnp.zeros_like(st_ref)

    tt = p_ref.shape[0]
    w0, a0, k_k, k_a, r_k, ln_g, ln_b = [vec_ref[i:i + 1, :] for i in range(7)]
    seg = seg_ref[...]

    def seg_sum(x):
        return jnp.concatenate(
            [_dot_split_rhs(x[:, i * LANE:(i + 1) * LANE], seg) for i in range(RW_W // LANE)], axis=1)

    p = p_ref[...]
    rows = lax.broadcasted_iota(jnp.int32, (tt, 1), 0)
    first = jnp.where(t_idx > 0, prev_ref[7:8, :], 0.0)
    shifted = jnp.where(rows == 0, first, pltpu.roll(p, 1, 0))
    p = p + mu_ref[...] * (shifted - p)
    r = p[:, 0:RW_W]
    k = p[:, RW_W:2 * RW_W]
    v = p[:, 2 * RW_W:3 * RW_W]
    wl = p[:, 3 * RW_W:3 * RW_W + LANE]
    al = p[:, 3 * RW_W + LANE:3 * RW_W + 2 * LANE]
    gl = p[:, 3 * RW_W + 2 * LANE:RW_COLS]
    w_raw = -_softplus(-(w0 + _bdot(jnp.tanh(wl), w2_ref[...]))) - 0.5
    lw = -jnp.exp(w_raw) * LOG2E
    a = jax.nn.sigmoid(a0 + _bdot(al, a2_ref[...]))
    g = _bdot(jax.nn.sigmoid(gl), g2_ref[...])
    kk = k * k_k
    kk = kk * lax.rsqrt(jnp.maximum(seg_sum(kk * kk), 1e-24))
    k2 = k * (1.0 + (a - 1.0) * k_a)
    bonus_s[...] = seg_sum(r * k2 * r_k) * v
    g_s[...] = g
    r_s[...] = r
    k_s[...] = k2
    v_s[...] = v
    lw_s[...] = lw
    av_s[...] = -kk
    bv_s[...] = kk * a

    c_row = lax.broadcasted_iota(jnp.int32, (CHUNK, CHUNK), 0)
    c_col = lax.broadcasted_iota(jnp.int32, (CHUNK, CHUNK), 1)
    tri = c_col <= c_row
    tri_strict = c_col < c_row
    tri_b = tri.astype(BF16)
    eye = (c_col == c_row).astype(F32)
    lane = lax.broadcasted_iota(jnp.int32, (1, RW_PAIR), 1)
    hmask = [(lane < RW_N).astype(F32), (lane >= RW_N).astype(F32)]
    p_row = lax.broadcasted_iota(jnp.int32, (RW_PAIR, RW_PAIR), 0)
    p_col = lax.broadcasted_iota(jnp.int32, (RW_PAIR, RW_PAIR), 1)
    bd = ((p_row < RW_N) == (p_col < RW_N)).astype(F32)

    n_par = min(RW_PAR_CHUNKS, n_chunks)
    items = [(cc, pi) for cc in range(n_par) for pi in range(RW_NPAIR)]
    r4 = lax.broadcasted_iota(jnp.int32, (CHUNK, 4 * CHUNK), 0)
    c4 = lax.broadcasted_iota(jnp.int32, (CHUNK, 4 * CHUNK), 1) % CHUNK
    tri4, tri_strict4 = c4 <= r4, c4 < r4
    eye4 = (c4 == r4).astype(F32)
    hmask2 = [jnp.concatenate([mk, mk], axis=1) for mk in hmask]
    q_row = lax.broadcasted_iota(jnp.int32, (4 * CHUNK, 4 * CHUNK), 0) // CHUNK
    q_col = lax.broadcasted_iota(jnp.int32, (4 * CHUNK, 4 * CHUNK), 1) // CHUNK
    bd4 = q_row == q_col

    def group(gi, carry):
        base = gi * (n_par * CHUNK)

        def ld(ref):
            return [ref[pl.ds(pl.multiple_of(base + cc * CHUNK, CHUNK), CHUNK),
                        pi * RW_PAIR:(pi + 1) * RW_PAIR] for cc, pi in items]

        rr, kc, vc, lwc, avc, bvc = ld(r_s), ld(k_s), ld(v_s), ld(lw_s), ld(av_s), ld(bv_s)
        st0 = [st_ref[pi] for pi in range(RW_NPAIR)]
        gcum = [_dot_split(tri_b, x) for x in lwc]
        g_last = [g[CHUNK - 1:CHUNK, :] for g in gcum]
        e_neg = [jnp.exp2(-g) for g in gcum]
        e_end = [jnp.exp2(gl - g) for gl, g in zip(g_last, gcum)]
        r_t = [x * jnp.exp2(g) for x, g in zip(rr, gcum)]
        a_t = [x * jnp.exp2(g - lw) for x, g, lw in zip(avc, gcum, lwc)]
        b_t = [x * e for x, e in zip(bvc, e_neg)]
        k_t = [x * e for x, e in zip(kc, e_neg)]
        b_bar = [x * e for x, e in zip(bvc, e_end)]
        k_bar = [x * e for x, e in zip(kc, e_end)]

        ar = [jnp.concatenate([a, r], axis=0) for a, r in zip(a_t, r_t)]
        w_rows = [jnp.concatenate([b * hmask[0], b * hmask[1], k * hmask[0], k * hmask[1]], axis=0)
                  for b, k in zip(b_t, k_t)]
        g_all = [_dot_nt(x, w) for x, w in zip(ar, w_rows)]
        l_all = [jnp.where(tri_strict4, g[:CHUNK], 0.0) for g in g_all]
        m_all = [jnp.where(tri4, g[CHUNK:], 0.0) for g in g_all]

        def blockdiag(p4):
            p4 = p4.astype(BF16)
            return jnp.where(bd4, jnp.concatenate([p4, p4, p4, p4], axis=0), jnp.zeros((), BF16))
        pw = [jnp.concatenate([l_all[2 * qd][:, :RW_PAIR], l_all[2 * qd + 1][:, :RW_PAIR]], axis=1)
              for qd in range(len(items) // 2)]
        t_inv = [eye4 + x for x in pw]
        pw_bd = [blockdiag(x) for x in pw]
        n = 2
        while n < CHUNK:
            pw = [jnp.dot(x.astype(BF16), xb, preferred_element_type=F32) for x, xb in zip(pw, pw_bd)]
            pw_bd = [blockdiag(x) for x in pw]
            t_inv = [t + jnp.dot(t.astype(BF16), xb, preferred_element_type=F32)
                     for t, xb in zip(t_inv, pw_bd)]
            n *= 2
        t_pair = [t_inv[it // 2][:, (it % 2) * RW_PAIR:(it % 2 + 1) * RW_PAIR] for it in range(len(items))]

        v_rows = [jnp.concatenate([v * hmask[0], v * hmask[1]], axis=0) for v in vc]
        lm = [_bdot(jnp.concatenate([l[:, RW_PAIR:], mm[:, RW_PAIR:]], axis=0), v)
              for l, mm, v in zip(l_all, m_all, v_rows)]
        x_rows = [jnp.concatenate(
            [jnp.concatenate([a * hmask[h], x[:CHUNK] * hmask[h]], axis=1) for h in range(2)], axis=0)
            for a, x in zip(a_t, lm)]
        x_p = [_bdot(t, x) for t, x in zip(t_pair, x_rows)]
        ry = [_bdot(mm[:, :RW_PAIR], jnp.concatenate([x * hmask2[0], x * hmask2[1]], axis=0))
              for mm, x in zip(m_all, x_p)]
        a_hat = [x[:, :RW_PAIR] for x in x_p]
        u_hat = [x[:, RW_PAIR:] for x in x_p]
        r_eff = [r + y[:, :RW_PAIR] for r, y in zip(r_t, ry)]
        y0 = [y[:, RW_PAIR:] + x[CHUNK:] for y, x in zip(ry, lm)]
        p_t = [_dot_tn(a, b) * bd for a, b in zip(a_hat, b_bar)]
        q_t = [_dot_tn(jnp.concatenate([u, v], axis=0), jnp.concatenate([b, k], axis=0)) * bd
               for u, v, b, k in zip(u_hat, vc, b_bar, k_bar)]
        w_end = [jnp.exp2(g) for g in g_last]

        st = list(st0)
        ys = []
        for it, (cc, pi) in enumerate(items):
            stb = st[pi].astype(BF16)
            ys.append(_dot_nt(r_eff[it], stb) + y0[it])
            st[pi] = st[pi] * w_end[it] + _bdot(stb, p_t[it]) + q_t[it]
        for it, (cc, pi) in enumerate(items):
            y_s[pl.ds(pl.multiple_of(base + cc * CHUNK, CHUNK), CHUNK),
                pi * RW_PAIR:(pi + 1) * RW_PAIR] = ys[it]
        for pi in range(RW_NPAIR):
            st_ref[pi] = st[pi]
        return carry

    lax.fori_loop(0, n_chunks // n_par, group, 0)

    y = y_s[...]
    inv_n = 1.0 / RW_N
    mu_y = seg_sum(y) * inv_n
    d = y - mu_y
    var = seg_sum(d * d) * inv_n
    yn = d * lax.rsqrt(var + RW_GN_EPS) * ln_g + ln_b
    o_ref[...] = ((yn + bonus_s[...]) * g_s[...]).astype(o_ref.dtype)


def _dot_split_rhs(x, m01):
    hi = x.astype(BF16)
    lo = (x - hi.astype(F32)).astype(BF16)
    return (jnp.dot(hi, m01, preferred_element_type=F32)
            + jnp.dot(lo, m01, preferred_element_type=F32))


def _rwkv(p_rw, mu, vecs, w2p, a2p, g2p, tt):
    bsz, seq, _ = p_rw.shape
    tt = min(tt, seq)
    seg = (np.arange(LANE)[:, None] // RW_N == np.arange(LANE)[None, :] // RW_N)
    seg = jnp.asarray(seg, BF16)
    row_blocks = tt // 8
    scratch = ([pltpu.VMEM((RW_NPAIR, RW_PAIR, RW_PAIR), F32)]
               + [pltpu.VMEM((tt, RW_W), F32) for _ in range(9)])
    return pl.pallas_call(
        functools.partial(_rwkv_kernel, n_chunks=tt // CHUNK),
        grid=(bsz, seq // tt),
        in_specs=[pl.BlockSpec((None, tt, RW_COLS), lambda b, t: (b, t, 0)),
                  pl.BlockSpec((None, 8, RW_COLS),
                               lambda b, t: (b, jnp.maximum(t * row_blocks - 1, 0), 0)),
                  pl.BlockSpec((1, RW_COLS), lambda b, t: (0, 0)),
                  pl.BlockSpec((8, RW_W), lambda b, t: (0, 0)),
                  pl.BlockSpec((LANE, RW_W), lambda b, t: (0, 0)),
                  pl.BlockSpec((LANE, RW_W), lambda b, t: (0, 0)),
                  pl.BlockSpec((2 * LANE, RW_W), lambda b, t: (0, 0)),
                  pl.BlockSpec((LANE, LANE), lambda b, t: (0, 0))],
        out_specs=pl.BlockSpec((None, tt, RW_W), lambda b, t: (b, t, 0)),
        out_shape=jax.ShapeDtypeStruct((bsz, seq, RW_W), BF16),
        scratch_shapes=scratch,
        compiler_params=_cparams(("parallel", "arbitrary")),
    )(p_rw, p_rw, mu, vecs, w2p, a2p, g2p, seg)


MLA_HD = LANE
MLA_ROPE_OFF = MLA_NOPE
ROLL_ROPE = LANE - MLA_ROPE


def _mla_proj_kernel(x_ref, pos_ref, freq_ref, sgn_ref, win_ref, qg_ref, wq_ref, kvg_ref, wk_ref,
                     wvt_ref, q_ref, k_ref, vt_ref):
    xb = x_ref[...].astype(BF16)
    p = jnp.dot(xb, win_ref[...], preferred_element_type=F32)
    cq = p[:, :MLA_Q_RANK]
    ckv = p[:, MLA_Q_RANK:MLA_Q_RANK + MLA_KV_RANK]
    kpe = p[:, MLA_Q_RANK + MLA_KV_RANK:]
    cqn = cq * lax.rsqrt(jnp.mean(cq * cq, axis=-1, keepdims=True) + RMS_EPS) * qg_ref[...]
    ckvn = ckv * lax.rsqrt(jnp.mean(ckv * ckv, axis=-1, keepdims=True) + RMS_EPS) * kvg_ref[...]
    cqb = cqn.astype(BF16)
    ckvb = ckvn.astype(BF16)

    ang = pos_ref[...].astype(F32) * freq_ref[...]
    lane = lax.broadcasted_iota(jnp.int32, (1, LANE), 1)
    is_rope = (lane >= MLA_ROPE_OFF) & (lane < MLA_ROPE_OFF + MLA_ROPE)
    cos_t = jnp.where(is_rope, jnp.cos(ang), 0.0)
    sin_t = jnp.sin(ang) * sgn_ref[...]
    scale = (MLA_NOPE + MLA_ROPE) ** -0.5 * math.log2(math.e)
    q_c = jnp.where(lane < MLA_NOPE, 1.0, cos_t) * scale
    q_s = sin_t * scale

    k_rot = kpe * cos_t + pltpu.roll(kpe, ROLL_ROPE, 1) * sin_t
    pair = lambda t: jnp.concatenate([t, t], axis=1)
    q_c2, q_s2, k_rot2 = pair(q_c), pair(q_s), pair(k_rot)
    for hp in range(MLA_HEADS // 2):
        cols = slice(2 * hp * MLA_HD, 2 * (hp + 1) * MLA_HD)
        qh = jnp.dot(cqb, wq_ref[:, cols], preferred_element_type=F32)
        q_ref[:, cols] = (qh * q_c2 + pltpu.roll(qh, 2 * MLA_HD - MLA_ROPE, 1) * q_s2).astype(q_ref.dtype)
        kh = jnp.dot(ckvb, wk_ref[:, cols], preferred_element_type=F32)
        k_ref[:, cols] = (kh + k_rot2).astype(k_ref.dtype)
    vt = lax.dot_general(wvt_ref[...], ckvb, (((1,), (1,)), ((), ())),
                         preferred_element_type=F32)
    vrow = lax.broadcasted_iota(jnp.int32, (vt.shape[0], 1), 0)
    vt_ref[...] = (vt + ((vrow % ATT_VROWS) >= MLA_V).astype(F32)).astype(vt_ref.dtype)


def _mla_proj(x2d, pos2d, freq, sgn, w_in, q_g, w_q, kv_g, w_k, w_vt, bsz, seq, tm):
    m, d = x2d.shape
    tm = min(tm, seq)
    nq = MLA_HEADS * MLA_HD
    nv = MLA_HEADS * ATT_VROWS
    spb = seq // tm
    const = lambda shape: pl.BlockSpec(shape, lambda i: (0,) * len(shape))
    return pl.pallas_call(
        _mla_proj_kernel,
        grid=(m // tm,),
        in_specs=[pl.BlockSpec((tm, d), lambda i: (i, 0)),
                  pl.BlockSpec((tm, 1), lambda i: (i, 0)),
                  const((1, LANE)), const((1, LANE)),
                  const(w_in.shape), const((1, MLA_Q_RANK)), const(w_q.shape),
                  const((1, MLA_KV_RANK)), const(w_k.shape), const(w_vt.shape)],
        out_specs=[pl.BlockSpec((tm, nq), lambda i: (i, 0)),
                   pl.BlockSpec((tm, nq), lambda i: (i, 0)),
                   pl.BlockSpec((None, nv, tm), lambda i: (i // spb, 0, i % spb))],
        out_shape=[jax.ShapeDtypeStruct((m, nq), BF16),
                   jax.ShapeDtypeStruct((m, nq), BF16),
                   jax.ShapeDtypeStruct((bsz, nv, seq), BF16)],
        name="mla_proj",
        compiler_params=_cparams(("parallel",)),
    )(x2d, pos2d, freq, sgn, w_in, q_g.reshape(1, -1), w_q, kv_g.reshape(1, -1), w_k, w_vt)


def _attn_kernel(q_ref, k_ref, vt_ref, o_ref, *, tq, n_heads):
    i = pl.program_id(2)
    q0 = i * tq
    nsub = tq // ATT_SUB
    qhs = [q_ref[:, h * MLA_HD:(h + 1) * MLA_HD] for h in range(n_heads)]
    kr = lax.broadcasted_iota(jnp.int32, (ATT_SUB, ATT_SUB), 0) // CHUNK
    qc = lax.broadcasted_iota(jnp.int32, (ATT_SUB, ATT_SUB), 1) // CHUNK
    diag_ok = kr <= qc

    def scores(k0, h, q_rows):
        return lax.dot_general(k_ref[pl.ds(k0, ATT_SUB), h * MLA_HD:(h + 1) * MLA_HD], q_rows,
                               (((1,), (1,)), ((), ())), preferred_element_type=F32)

    def update(h, k0, m_run, acc, s):
        m_new = jnp.maximum(m_run, jnp.max(s, axis=0, keepdims=True))
        alpha = jnp.exp2(m_run - m_new)
        pr = jnp.exp2(s - m_new).astype(BF16)
        vt = vt_ref[h * ATT_VROWS:(h + 1) * ATT_VROWS, pl.ds(k0, ATT_SUB)]
        return m_new, acc * alpha + jnp.dot(vt, pr, preferred_element_type=F32)

    def full_tile(j, carry):
        k0 = pl.multiple_of(j * tq, tq)
        ss = [lax.dot_general(k_ref[pl.ds(k0, tq), h * MLA_HD:(h + 1) * MLA_HD], qhs[h],
                              (((1,), (1,)), ((), ())), preferred_element_type=F32)
              for h in range(n_heads)]
        cur = list(carry)
        for u in range(nsub):
            for h in range(n_heads):
                cur[h] = update(h, k0 + u * ATT_SUB, *cur[h], ss[h][u * ATT_SUB:(u + 1) * ATT_SUB, :])
        return tuple(cur)

    init = tuple((jnp.full((1, tq), -jnp.inf, F32), jnp.zeros((ATT_VROWS, tq), F32))
                 for _ in range(n_heads))
    cur = list(lax.fori_loop(0, i, full_tile, init))

    ds = [[scores(pl.multiple_of(q0 + u * ATT_SUB, ATT_SUB), h, qhs[h][u * ATT_SUB:, :])
           for h in range(n_heads)] for u in range(nsub)]
    for u in range(nsub):
        c0 = u * ATT_SUB
        k0 = pl.multiple_of(q0 + c0, ATT_SUB)
        for h in range(n_heads):
            m_run, acc = cur[h]
            s = ds[u][h]
            s_diag = jnp.where(diag_ok, s[:, :ATT_SUB], -jnp.inf)
            s = s_diag if u == nsub - 1 else jnp.concatenate([s_diag, s[:, ATT_SUB:]], axis=1)
            m_part, acc_part = update(h, k0, m_run[:, c0:], acc[:, c0:], s)
            if u > 0:
                m_part = jnp.concatenate([m_run[:, :c0], m_part], axis=1)
                acc_part = jnp.concatenate([acc[:, :c0], acc_part], axis=1)
            cur[h] = (m_part, acc_part)

    outs = [acc[:MLA_V, :] / acc[MLA_V:MLA_V + 1, :] for _, acc in cur]
    o_ref[...] = jnp.concatenate(outs, axis=0).T.astype(o_ref.dtype)


def _attention(q, k, vt, tq, n_heads):
    bsz, seq, nq = q.shape
    tq = min(tq, seq)
    width = n_heads * MLA_HD
    return pl.pallas_call(
        functools.partial(_attn_kernel, tq=tq, n_heads=n_heads),
        grid=(bsz, MLA_HEADS // n_heads, seq // tq),
        in_specs=[pl.BlockSpec((None, tq, width), lambda b, h, i: (b, i, h)),
                  pl.BlockSpec((None, seq, width), lambda b, h, i: (b, 0, h)),
                  pl.BlockSpec((None, n_heads * ATT_VROWS, seq), lambda b, h, i: (b, h, 0))],
        out_specs=pl.BlockSpec((None, tq, n_heads * MLA_V), lambda b, h, i: (b, i, h)),
        out_shape=jax.ShapeDtypeStruct((bsz, seq, MLA_HEADS * MLA_V), BF16),
        name="mla_attention",
        compiler_params=_cparams(("parallel", "parallel", "arbitrary")),
    )(q, k, vt)


def _pad_cols(w, width):
    return jnp.pad(w, ((0, 0), (0, width - w.shape[1])))


def _pad_rows(w, height):
    return jnp.pad(w, ((0, height - w.shape[0]), (0, 0)))


def _even_weights(w_in, mu):
    gla_in = 2 * GLA_QK + 2 * GLA_V + GLA_GATE_RANK
    w_gla, w_rw = w_in[:, :gla_in], w_in[:, gla_in:]
    o = 3 * RW_W
    rw_parts = [w_rw[:, :o],
                _pad_cols(w_rw[:, o:o + RW_DECAY_RANK], LANE),
                _pad_cols(w_rw[:, o + RW_DECAY_RANK:o + RW_DECAY_RANK + RW_A_RANK], LANE),
                _pad_cols(w_rw[:, o + RW_DECAY_RANK + RW_A_RANK:], 2 * LANE)]
    mu_parts = [mu[:o],
                jnp.pad(mu[o:o + RW_DECAY_RANK], (0, LANE - RW_DECAY_RANK)),
                jnp.pad(mu[o + RW_DECAY_RANK:o + RW_DECAY_RANK + RW_A_RANK], (0, LANE - RW_A_RANK)),
                jnp.pad(mu[o + RW_DECAY_RANK + RW_A_RANK:], (0, 2 * LANE - RW_GATE_RANK))]
    w_all = jnp.concatenate(rw_parts + [_pad_cols(w_gla, GLA_COLS)], axis=1).astype(BF16)
    return w_all, jnp.concatenate(mu_parts).reshape(1, RW_COLS)


def _mla_weights(w_in, w_q_b, w_kv_b):
    swap = np.arange(MLA_ROPE) ^ 1
    d = w_in.shape[0]
    kpe = w_in[:, MLA_Q_RANK + MLA_KV_RANK:]
    w_in_p = jnp.concatenate([w_in[:, :MLA_Q_RANK + MLA_KV_RANK],
                              jnp.zeros((d, MLA_NOPE), F32), kpe, kpe[:, swap]], axis=1).astype(BF16)
    wq = w_q_b.reshape(MLA_Q_RANK, MLA_HEADS, MLA_NOPE + MLA_ROPE)
    wq = jnp.concatenate([wq, wq[:, :, MLA_NOPE:][:, :, swap]], axis=-1)
    wq = wq.reshape(MLA_Q_RANK, MLA_HEADS * MLA_HD).astype(BF16)
    wkv = w_kv_b.reshape(MLA_KV_RANK, MLA_HEADS, MLA_NOPE + MLA_V)
    wk = jnp.pad(wkv[:, :, :MLA_NOPE], ((0, 0), (0, 0), (0, MLA_HD - MLA_NOPE)))
    wk = wk.reshape(MLA_KV_RANK, MLA_HEADS * MLA_HD).astype(BF16)
    wv = jnp.pad(wkv[:, :, MLA_NOPE:], ((0, 0), (0, 0), (0, ATT_VROWS - MLA_V)))
    wvt = wv.reshape(MLA_KV_RANK, MLA_HEADS * ATT_VROWS).T.astype(BF16)
    return w_in_p, wq, wk, wvt


def _rope_tables():
    inv_freq = ROPE_THETA ** (-jnp.arange(0, MLA_ROPE, 2, dtype=jnp.float32) / MLA_ROPE)
    d = np.arange(MLA_ROPE)
    freq = jnp.zeros((LANE,), F32).at[MLA_ROPE_OFF:MLA_ROPE_OFF + MLA_ROPE].set(inv_freq[d // 2])
    sgn = np.zeros((LANE,), np.float32)
    sgn[MLA_ROPE_OFF:MLA_ROPE_OFF + MLA_ROPE] = np.where(d % 2 == 0, -1.0, 1.0)
    return freq.reshape(1, LANE), jnp.asarray(sgn).reshape(1, LANE)


def kernel(x, positions, even_w_in, gla_gate_w2, gla_gate_b, gla_norm_g, rwkv_mu, rwkv_w0, rwkv_w2,
           rwkv_a0, rwkv_a2, rwkv_g2, rwkv_k_k, rwkv_k_a, rwkv_r_k, rwkv_ln_g, rwkv_ln_b, even_w_out,
           mla_w_in, mla_q_norm_g, mla_w_q_b, mla_kv_norm_g, mla_w_kv_b, mla_w_out,
           ffn_w_gate_up, ffn_w_down, ln_g, ln_b):
    bsz, seq, d = x.shape
    m = bsz * seq
    x2d = x.reshape(m, d)
    wgu = [ffn_w_gate_up[i].astype(BF16) for i in range(DEPTH)]
    wdn = [ffn_w_down[i].astype(BF16) for i in range(DEPTH)]

    w_all, mu_p = _even_weights(even_w_in[0], rwkv_mu[0])
    p_rw, p_gla = _inproj(x2d, w_all, (RW_COLS, GLA_COLS), F32, tm=256)
    o_a = _gla(p_gla.reshape(bsz, seq, GLA_COLS),
               _pad_rows(gla_gate_w2[0], LANE).astype(BF16), gla_gate_b[0], gla_norm_g[0], tt=GLA_TT)
    vecs = jnp.stack([rwkv_w0[0], rwkv_a0[0], rwkv_k_k[0], rwkv_k_a[0], rwkv_r_k[0].reshape(RW_W),
                      rwkv_ln_g[0], rwkv_ln_b[0], jnp.zeros((RW_W,), F32)])
    o_b = _rwkv(p_rw.reshape(bsz, seq, RW_COLS), mu_p, vecs,
                _pad_rows(rwkv_w2[0], LANE).astype(BF16), _pad_rows(rwkv_a2[0], LANE).astype(BF16),
                _pad_rows(rwkv_g2[0], 2 * LANE).astype(BF16), tt=RW_TT)
    w_out = even_w_out[0].astype(BF16)
    x2d = _mix_ffn([o_a.reshape(m, GLA_V), o_b.reshape(m, RW_W)], [w_out[:GLA_V], w_out[GLA_V:]],
                   x2d, ln_g[0, 0], ln_b[0, 0], wgu[0], wdn[0], ln_g[0, 1], ln_b[0, 1],
                   tm=FFN_TM, th=FFN_TH)

    w_in_p, wq, wk, wvt = _mla_weights(mla_w_in[0], mla_w_q_b[0], mla_w_kv_b[0])
    freq, sgn = _rope_tables()
    q, k, vt = _mla_proj(x2d, positions.reshape(m, 1), freq, sgn, w_in_p, mla_q_norm_g[0], wq,
                         mla_kv_norm_g[0], wk, wvt, bsz, seq, tm=MLA_TM)
    nq = MLA_HEADS * MLA_HD
    o = _attention(q.reshape(bsz, seq, nq), k.reshape(bsz, seq, nq), vt, tq=ATT_TQ, n_heads=ATT_HEADS)
    x2d = _mix_ffn([o.reshape(m, MLA_HEADS * MLA_V)], [mla_w_out[0].astype(BF16)],
                   x2d, ln_g[1, 0], ln_b[1, 0], wgu[1], wdn[1], ln_g[1, 1], ln_b[1, 1],
                   tm=FFN_TM, th=FFN_TH)
    return x2d.reshape(bsz, seq, d)
```

```python
import functools
import math

import jax
import jax.numpy as jnp
import numpy as np
from jax import lax
from jax.experimental import pallas as pl
from jax.experimental.pallas import tpu as pltpu

F32 = jnp.float32
BF16 = jnp.bfloat16

DEPTH = 2
CHUNK = 64
DN_ALPHA = (2.0 * DEPTH) ** 0.25
LN_EPS = 1e-5
RMS_EPS = 1e-6

GLA_HEADS = 4
GLA_DK = 64
GLA_DV = 128
GLA_GATE_RANK = 16
GLA_TAU = 16.0
GLA_QK = GLA_HEADS * GLA_DK
GLA_V = GLA_HEADS * GLA_DV
GLA_SUB = 16
GLA_EXP_CLAMP = 60.0 * math.log2(math.e)

RW_HEADS = 8
RW_N = 64
RW_W = RW_HEADS * RW_N
RW_DECAY_RANK = 64
RW_A_RANK = 64
RW_GATE_RANK = 160
RW_GN_EPS = 64e-5

MLA_HEADS = 16
MLA_NOPE = 64
MLA_ROPE = 32
MLA_V = 64
MLA_Q_RANK = 768
MLA_KV_RANK = 256
ROPE_THETA = 10000.0

LOG2E = math.log2(math.e)
LANE = 128
VMEM_LIMIT = 48 * 1024 * 1024
FFN_TM, FFN_TH = 1024, 256
FFN_SPLIT = 2
MLA_TM = 512
ATT_TQ, ATT_HEADS = 1024, 4
ATT_SUB = 256
ATT_VROWS = 80

RW_COLS = 3 * RW_W + LANE + LANE + 2 * LANE
GLA_COLS = 2 * GLA_QK + 2 * GLA_V + LANE


def _cparams(sem):
    return pltpu.CompilerParams(dimension_semantics=sem, vmem_limit_bytes=VMEM_LIMIT)


def _bdot(a, b):
    return jnp.dot(a.astype(BF16), b.astype(BF16), preferred_element_type=F32)


def _dot_nt(a, b):
    return lax.dot_general(a.astype(BF16), b.astype(BF16), (((1,), (1,)), ((), ())),
                           preferred_element_type=F32)


def _dot_tn(a, b):
    return lax.dot_general(a.astype(BF16), b.astype(BF16), (((0,), (0,)), ((), ())),
                           preferred_element_type=F32)


def _dot_split(m01, x):
    hi = x.astype(BF16)
    lo = (x - hi.astype(F32)).astype(BF16)
    m = m01.astype(BF16)
    return (jnp.dot(m, hi, preferred_element_type=F32)
            + jnp.dot(m, lo, preferred_element_type=F32))


def _softplus(x):
    return jnp.maximum(x, 0.0) + jnp.log(1.0 + jnp.exp(-jnp.abs(x)))


def _layer_norm(y, g, b):
    mu = jnp.mean(y, axis=-1, keepdims=True)
    d = y - mu
    var = jnp.mean(d * d, axis=-1, keepdims=True)
    return d * lax.rsqrt(var + LN_EPS) * g + b


def _inproj_kernel(x_ref, w_ref, *o_refs, widths, tn):
    xb = x_ref[...].astype(BF16)
    off = 0
    for o_ref, width in zip(o_refs, widths):
        for j in range(0, width, tn):
            w = min(tn, width - j)
            o_ref[:, j:j + w] = jnp.dot(xb, w_ref[:, off + j:off + j + w],
                                        preferred_element_type=F32).astype(o_ref.dtype)
        off += width


def _inproj(x2d, w_bf16, widths, out_dtype, tm):
    m, k = x2d.shape
    tm = min(tm, m)
    n = sum(widths)
    return pl.pallas_call(
        functools.partial(_inproj_kernel, widths=tuple(widths), tn=512),
        grid=(m // tm,),
        in_specs=[pl.BlockSpec((tm, k), lambda i: (i, 0)),
                  pl.BlockSpec((k, n), lambda i: (0, 0))],
        out_specs=[pl.BlockSpec((tm, wd), lambda i: (i, 0)) for wd in widths],
        out_shape=[jax.ShapeDtypeStruct((m, wd), out_dtype) for wd in widths],
        compiler_params=_cparams(("parallel",)),
    )(x2d, w_bf16)


def _mix_ffn_kernel(*refs, n_in):
    a_refs = refs[:n_in]
    wo_refs = refs[n_in:2 * n_in]
    (x_ref, g0_ref, b0_ref, wg_ref, wu_ref, wd_ref, g1_ref, b1_ref, o_ref,
     x1_ref, xb_ref, acc_ref) = refs[2 * n_in:]
    j = pl.program_id(1)
    last = pl.num_programs(1) - 1
    rows_per = x_ref.shape[0] // FFN_SPLIT
    groups = [slice(r * rows_per, (r + 1) * rows_per) for r in range(FFN_SPLIT)]

    def ffn(xb):
        gate = jnp.dot(xb, wg_ref[...], preferred_element_type=F32)
        up = jnp.dot(xb, wu_ref[...], preferred_element_type=F32)
        h = (gate * jax.nn.sigmoid(gate) * up).astype(BF16)
        return jnp.dot(h, wd_ref[...], preferred_element_type=F32)

    @pl.when(j == 0)
    def _():
        for rows in groups:
            y = DN_ALPHA * x_ref[rows, :]
            for a_ref, w_ref in zip(a_refs, wo_refs):
                y = y + jnp.dot(a_ref[rows, :], w_ref[...], preferred_element_type=F32)
            x1 = _layer_norm(y, g0_ref[...], b0_ref[...])
            xb = x1.astype(BF16)
            x1_ref[rows, :] = x1
            xb_ref[rows, :] = xb
            acc_ref[rows, :] = ffn(xb)

    @pl.when((j > 0) & (j < last))
    def _():
        acc_ref[...] += ffn(xb_ref[...])

    @pl.when(j == last)
    def _():
        for rows in groups:
            y = DN_ALPHA * x1_ref[rows, :] + acc_ref[rows, :] + ffn(xb_ref[rows, :])
            o_ref[rows, :] = _layer_norm(y, g1_ref[...], b1_ref[...])


def _mix_ffn(acts, w_outs, x2d, g0, b0, w_gate_up, w_down, layer, g1, b1, tm, th):
    m, d = x2d.shape
    hidden = w_down.shape[1]
    tm = min(tm, m)
    nh = hidden // th
    n_in = len(acts)
    vec = pl.BlockSpec((1, d), lambda i, j: (0, 0))
    in_specs = ([pl.BlockSpec((tm, a.shape[1]), lambda i, j: (i, 0)) for a in acts]
                + [pl.BlockSpec(w.shape, lambda i, j: (0, 0)) for w in w_outs]
                + [pl.BlockSpec((tm, d), lambda i, j: (i, 0)), vec, vec,
                   pl.BlockSpec((None, d, th), lambda i, j: (layer, 0, j)),
                   pl.BlockSpec((None, d, th), lambda i, j: (layer, 0, j + nh)),
                   pl.BlockSpec((None, th, d), lambda i, j: (layer, j, 0)), vec, vec])
    return pl.pallas_call(
        functools.partial(_mix_ffn_kernel, n_in=n_in),
        grid=(m // tm, nh),
        in_specs=in_specs,
        out_specs=pl.BlockSpec((tm, d), lambda i, j: (i, 0)),
        out_shape=jax.ShapeDtypeStruct((m, d), F32),
        scratch_shapes=[pltpu.VMEM((tm, d), F32), pltpu.VMEM((tm, d), BF16), pltpu.VMEM((tm, d), F32)],
        name="mix_ffn",
        compiler_params=_cparams(("parallel", "arbitrary")),
    )(*acts, *w_outs, x2d, g0.reshape(1, d), b0.reshape(1, d), w_gate_up, w_gate_up, w_down,
      g1.reshape(1, d), b1.reshape(1, d))


def _gla_kernel(p_ref, w2_ref, gb_ref, ng_ref, o_ref, st_ref, *, n_chunks):
    @pl.when(pl.program_id(1) == 0)
    def _():
        st_ref[...] = jnp.zeros_like(st_ref)

    c_row = lax.broadcasted_iota(jnp.int32, (CHUNK, CHUNK), 0)
    c_col = lax.broadcasted_iota(jnp.int32, (CHUNK, CHUNK), 1)
    tri = (c_col <= c_row)
    tri_loc = tri & (c_col >= (c_row // GLA_SUB) * GLA_SUB)
    cum_mat = jnp.concatenate([tri, tri_loc], axis=0).astype(BF16)
    lane = lax.broadcasted_iota(jnp.int32, (1, GLA_QK), 1)
    head_masks = [(lane // GLA_DK == h).astype(F32) for h in range(GLA_HEADS)]
    n_sub = CHUNK // GLA_SUB

    n_par = min(GLA_PAR_CHUNKS, n_chunks)
    o_v = 2 * GLA_QK
    o_g = o_v + GLA_V
    o_l = o_g + GLA_V

    def group(gi, carry):
        base = gi * (n_par * CHUNK)
        rows = [pl.ds(pl.multiple_of(base + cc * CHUNK, CHUNK), CHUNK) for cc in range(n_par)]
        q = [p_ref[r, 0:GLA_QK] * (GLA_DK ** -0.5) for r in rows]
        k = [p_ref[r, GLA_QK:o_v] for r in rows]
        vb = [p_ref[r, o_v:o_g].astype(BF16) for r in rows]
        gg = [p_ref[r, o_g:o_l] for r in rows]
        z = [_bdot(p_ref[r, o_l:GLA_COLS], w2_ref[...]) + gb_ref[...] for r in rows]
        log_a = [-_softplus(-x) * (LOG2E / GLA_TAU) for x in z]
        cums = [_dot_split(cum_mat, x) for x in log_a]
        b = [x[:CHUNK] for x in cums]
        b_last = [x[CHUNK - 1:CHUNK, :] for x in b]
        q_loc = [x * jnp.exp2(c[CHUNK:]) for x, c in zip(q, cums)]
        a_blocks = []
        for cc in range(n_par):
            blocks = []
            for i in range(n_sub):
                ref_i = (jnp.zeros((1, GLA_QK), F32) if i == 0
                         else b[cc][i * GLA_SUB - 1:i * GLA_SUB, :])
                k_i = k[cc] * jnp.exp2(jnp.minimum(ref_i - b[cc], GLA_EXP_CLAMP))
                q_i = q_loc[cc][i * GLA_SUB:(i + 1) * GLA_SUB, :]
                lhs = jnp.concatenate([q_i * m for m in head_masks], axis=0)
                blocks.append(_dot_nt(lhs, k_i))
            a_blocks.append(blocks)
        o_intra = []
        for cc in range(n_par):
            per_head = []
            for h in range(GLA_HEADS):
                a_h = jnp.concatenate(
                    [blk[h * GLA_SUB:(h + 1) * GLA_SUB, :] for blk in a_blocks[cc]], axis=0)
                a_h = jnp.where(tri, a_h, 0.0)
                per_head.append(_bdot(a_h, vb[cc][:, h * GLA_DV:(h + 1) * GLA_DV]))
            o_intra.append(per_head)
        lhs_inter = [jnp.concatenate([x * jnp.exp2(bb) * m for m in head_masks], axis=0)
                     for x, bb in zip(q, b)]
        kv = [_dot_tn(v, x * jnp.exp2(bl - bb)) for v, x, bl, bb in zip(vb, k, b_last, b)]
        kv = [sum(x[h * GLA_DV:(h + 1) * GLA_DV, :] * head_masks[h] for h in range(GLA_HEADS)) for x in kv]
        decay = [jnp.exp2(x) for x in b_last]

        st = st_ref[...]
        outs = []
        for cc in range(n_par):
            o_inter = _dot_nt(lhs_inter[cc], st)
            st = st * decay[cc] + kv[cc]
            for h in range(GLA_HEADS):
                o_h = o_inter[h * CHUNK:(h + 1) * CHUNK, :] + o_intra[cc][h]
                ms = jnp.mean(o_h * o_h, axis=-1, keepdims=True)
                o_h = o_h * lax.rsqrt(ms + RMS_EPS) * ng_ref[...]
                g_h = gg[cc][:, h * GLA_DV:(h + 1) * GLA_DV]
                outs.append((o_h * (g_h * jax.nn.sigmoid(g_h))).astype(o_ref.dtype))
        for cc in range(n_par):
            for h in range(GLA_HEADS):
                o_ref[rows[cc], h * GLA_DV:(h + 1) * GLA_DV] = outs[cc * GLA_HEADS + h]
        st_ref[...] = st
        return carry

    lax.fori_loop(0, n_chunks // n_par, group, 0)


def _gla(p_gla, w2p, gate_b, norm_g, tt):
    bsz, seq, _ = p_gla.shape
    tt = min(tt, seq)
    return pl.pallas_call(
        functools.partial(_gla_kernel, n_chunks=tt // CHUNK),
        grid=(bsz, seq // tt),
        in_specs=[pl.BlockSpec((None, tt, GLA_COLS), lambda b, t: (b, t, 0)),
                  pl.BlockSpec((LANE, GLA_QK), lambda b, t: (0, 0)),
                  pl.BlockSpec((1, GLA_QK), lambda b, t: (0, 0)),
                  pl.BlockSpec((1, GLA_DV), lambda b, t: (0, 0))],
        out_specs=pl.BlockSpec((None, tt, GLA_V), lambda b, t: (b, t, 0)),
        out_shape=jax.ShapeDtypeStruct((bsz, seq, GLA_V), BF16),
        scratch_shapes=[pltpu.VMEM((GLA_DV, GLA_QK), F32)],
        compiler_params=_cparams(("parallel", "arbitrary")),
    )(p_gla, w2p, gate_b.reshape(1, GLA_QK), norm_g.reshape(1, GLA_DV))


RW_PAIR = 2 * RW_N
RW_NPAIR = RW_HEADS // 2
RW_PAR_CHUNKS = 4
RW_TT = 256
GLA_TT = 512
GLA_PAR_CHUNKS = 8


def _rwkv_kernel(p_ref, prev_ref, mu_ref, vec_ref, w2_ref, a2_ref, g2_ref, seg_ref, o_ref,
                 st_ref, r_s, k_s, v_s, lw_s, av_s, bv_s, y_s, bonus_s, g_s, *, n_chunks):
    t_idx = pl.program_id(1)

    @pl.when(t_idx == 0)
    def _():
        st_ref[...] = jnp.zeros_like(st_ref)

    tt = p_ref.shape[0]
    w0, a0, k_k, k_a, r_k, ln_g, ln_b = [vec_ref[i:i + 1, :] for i in range(7)]
    seg = seg_ref[...]

    def seg_sum(x):
        return jnp.concatenate(
            [_dot_split_rhs(x[:, i * LANE:(i + 1) * LANE], seg) for i in range(RW_W // LANE)], axis=1)

    p = p_ref[...]
    rows = lax.broadcasted_iota(jnp.int32, (tt, 1), 0)
    first = jnp.where(t_idx > 0, prev_ref[7:8, :], 0.0)
    shifted = jnp.where(rows == 0, first, pltpu.roll(p, 1, 0))
    p = p + mu_ref[...] * (shifted - p)
    r = p[:, 0:RW_W]
    k = p[:, RW_W:2 * RW_W]
    v = p[:, 2 * RW_W:3 * RW_W]
    wl = p[:, 3 * RW_W:3 * RW_W + LANE]
    al = p[:, 3 * RW_W + LANE:3 * RW_W + 2 * LANE]
    gl = p[:, 3 * RW_W + 2 * LANE:RW_COLS]
    w_raw = -_softplus(-(w0 + _bdot(jnp.tanh(wl), w2_ref[...]))) - 0.5
    lw = -jnp.exp(w_raw) * LOG2E
    a = jax.nn.sigmoid(a0 + _bdot(al, a2_ref[...]))
    g = _bdot(jax.nn.sigmoid(gl), g2_ref[...])
    kk = k * k_k
    kk = kk * lax.rsqrt(jnp.maximum(seg_sum(kk * kk), 1e-24))
    k2 = k * (1.0 + (a - 1.0) * k_a)
    bonus_s[...] = seg_sum(r * k2 * r_k) * v
    g_s[...] = g
    r_s[...] = r
    k_s[...] = k2
    v_s[...] = v
    lw_s[...] = lw
    av_s[...] = -kk
    bv_s[...] = kk * a

    c_row = lax.broadcasted_iota(jnp.int32, (CHUNK, CHUNK), 0)
    c_col = lax.broadcasted_iota(jnp.int32, (CHUNK, CHUNK), 1)
    tri = c_col <= c_row
    tri_strict = c_col < c_row
    tri_b = tri.astype(BF16)
    eye = (c_col == c_row).astype(F32)
    lane = lax.broadcasted_iota(jnp.int32, (1, RW_PAIR), 1)
    hmask = [(lane < RW_N).astype(F32), (lane >= RW_N).astype(F32)]
    p_row = lax.broadcasted_iota(jnp.int32, (RW_PAIR, RW_PAIR), 0)
    p_col = lax.broadcasted_iota(jnp.int32, (RW_PAIR, RW_PAIR), 1)
    bd = ((p_row < RW_N) == (p_col < RW_N)).astype(F32)

    n_par = min(RW_PAR_CHUNKS, n_chunks)
    items = [(cc, pi) for cc in range(n_par) for pi in range(RW_NPAIR)]
    r4 = lax.broadcasted_iota(jnp.int32, (CHUNK, 4 * CHUNK), 0)
    c4 = lax.broadcasted_iota(jnp.int32, (CHUNK, 4 * CHUNK), 1) % CHUNK
    tri4, tri_strict4 = c4 <= r4, c4 < r4
    eye4 = (c4 == r4).astype(F32)
    hmask2 = [jnp.concatenate([mk, mk], axis=1) for mk in hmask]
    q_row = lax.broadcasted_iota(jnp.int32, (4 * CHUNK, 4 * CHUNK), 0) // CHUNK
    q_col = lax.broadcasted_iota(jnp.int32, (4 * CHUNK, 4 * CHUNK), 1) // CHUNK
    bd4 = q_row == q_col

    def group(gi, carry):
        base = gi * (n_par * CHUNK)

        def ld(ref):
            return [ref[pl.ds(pl.multiple_of(base + cc * CHUNK, CHUNK), CHUNK),
                        pi * RW_PAIR:(pi + 1) * RW_PAIR] for cc, pi in items]

        rr, kc, vc, lwc, avc, bvc = ld(r_s), ld(k_s), ld(v_s), ld(lw_s), ld(av_s), ld(bv_s)
        st0 = [st_ref[pi] for pi in range(RW_NPAIR)]
        gcum = [_dot_split(tri_b, x) for x in lwc]
        g_last = [g[CHUNK - 1:CHUNK, :] for g in gcum]
        e_neg = [jnp.exp2(-g) for g in gcum]
        e_end = [jnp.exp2(gl - g) for gl, g in zip(g_last, gcum)]
        r_t = [x * jnp.exp2(g) for x, g in zip(rr, gcum)]
        a_t = [x * jnp.exp2(g - lw) for x, g, lw in zip(avc, gcum, lwc)]
        b_t = [x * e for x, e in zip(bvc, e_neg)]
        k_t = [x * e for x, e in zip(kc, e_neg)]
        b_bar = [x * e for x, e in zip(bvc, e_end)]
        k_bar = [x * e for x, e in zip(kc, e_end)]

        ar = [jnp.concatenate([a, r], axis=0) for a, r in zip(a_t, r_t)]
        w_rows = [jnp.concatenate([b * hmask[0], b * hmask[1], k * hmask[0], k * hmask[1]], axis=0)
                  for b, k in zip(b_t, k_t)]
        g_all = [_dot_nt(x, w) for x, w in zip(ar, w_rows)]
        l_all = [jnp.where(tri_strict4, g[:CHUNK], 0.0) for g in g_all]
        m_all = [jnp.where(tri4, g[CHUNK:], 0.0) for g in g_all]

        def blockdiag(p4):
            p4 = p4.astype(BF16)
            return jnp.where(bd4, jnp.concatenate([p4, p4, p4, p4], axis=0), jnp.zeros((), BF16))
        pw = [jnp.concatenate([l_all[2 * qd][:, :RW_PAIR], l_all[2 * qd + 1][:, :RW_PAIR]], axis=1)
              for qd in range(len(items) // 2)]
        t_inv = [eye4 + x for x in pw]
        pw_bd = [blockdiag(x) for x in pw]
        n = 2
        while n < CHUNK:
            pw = [jnp.dot(x.astype(BF16), xb, preferred_element_type=F32) for x, xb in zip(pw, pw_bd)]
            pw_bd = [blockdiag(x) for x in pw]
            t_inv = [t + jnp.dot(t.astype(BF16), xb, preferred_element_type=F32)
                     for t, xb in zip(t_inv, pw_bd)]
            n *= 2
        t_pair = [t_inv[it // 2][:, (it % 2) * RW_PAIR:(it % 2 + 1) * RW_PAIR] for it in range(len(items))]

        v_rows = [jnp.concatenate([v * hmask[0], v * hmask[1]], axis=0) for v in vc]
        lm = [_bdot(jnp.concatenate([l[:, RW_PAIR:], mm[:, RW_PAIR:]], axis=0), v)
              for l, mm, v in zip(l_all, m_all, v_rows)]
        x_rows = [jnp.concatenate(
            [jnp.concatenate([a * hmask[h], x[:CHUNK] * hmask[h]], axis=1) for h in range(2)], axis=0)
            for a, x in zip(a_t, lm)]
        x_p = [_bdot(t, x) for t, x in zip(t_pair, x_rows)]
        ry = [_bdot(mm[:, :RW_PAIR], jnp.concatenate([x * hmask2[0], x * hmask2[1]], axis=0))
              for mm, x in zip(m_all, x_p)]
        a_hat = [x[:, :RW_PAIR] for x in x_p]
        u_hat = [x[:, RW_PAIR:] for x in x_p]
        r_eff = [r + y[:, :RW_PAIR] for r, y in zip(r_t, ry)]
        y0 = [y[:, RW_PAIR:] + x[CHUNK:] for y, x in zip(ry, lm)]
        p_t = [_dot_tn(a, b) * bd for a, b in zip(a_hat, b_bar)]
        q_t = [_dot_tn(jnp.concatenate([u, v], axis=0), jnp.concatenate([b, k], axis=0)) * bd
               for u, v, b, k in zip(u_hat, vc, b_bar, k_bar)]
        w_end = [jnp.exp2(g) for g in g_last]

        st = list(st0)
        ys = []
        for it, (cc, pi) in enumerate(items):
            stb = st[pi].astype(BF16)
            ys.append(_dot_nt(r_eff[it], stb) + y0[it])
            st[pi] = st[pi] * w_end[it] + _bdot(stb, p_t[it]) + q_t[it]
        for it, (cc, pi) in enumerate(items):
            y_s[pl.ds(pl.multiple_of(base + cc * CHUNK, CHUNK), CHUNK),
                pi * RW_PAIR:(pi + 1) * RW_PAIR] = ys[it]
        for pi in range(RW_NPAIR):
            st_ref[pi] = st[pi]
        return carry

    lax.fori_loop(0, n_chunks // n_par, group, 0)

    y = y_s[...]
    inv_n = 1.0 / RW_N
    mu_y = seg_sum(y) * inv_n
    d = y - mu_y
    var = seg_sum(d * d) * inv_n
    yn = d * lax.rsqrt(var + RW_GN_EPS) * ln_g + ln_b
    o_ref[...] = ((yn + bonus_s[...]) * g_s[...]).astype(o_ref.dtype)


def _dot_split_rhs(x, m01):
    hi = x.astype(BF16)
    lo = (x - hi.astype(F32)).astype(BF16)
    return (jnp.dot(hi, m01, preferred_element_type=F32)
            + jnp.dot(lo, m01, preferred_element_type=F32))


def _rwkv(p_rw, mu, vecs, w2p, a2p, g2p, tt):
    bsz, seq, _ = p_rw.shape
    tt = min(tt, seq)
    seg = (np.arange(LANE)[:, None] // RW_N == np.arange(LANE)[None, :] // RW_N)
    seg = jnp.asarray(seg, BF16)
    row_blocks = tt // 8
    scratch = ([pltpu.VMEM((RW_NPAIR, RW_PAIR, RW_PAIR), F32)]
               + [pltpu.VMEM((tt, RW_W), F32) for _ in range(9)])
    return pl.pallas_call(
        functools.partial(_rwkv_kernel, n_chunks=tt // CHUNK),
        grid=(bsz, seq // tt),
        in_specs=[pl.BlockSpec((None, tt, RW_COLS), lambda b, t: (b, t, 0)),
                  pl.BlockSpec((None, 8, RW_COLS),
                               lambda b, t: (b, jnp.maximum(t * row_blocks - 1, 0), 0)),
                  pl.BlockSpec((1, RW_COLS), lambda b, t: (0, 0)),
                  pl.BlockSpec((8, RW_W), lambda b, t: (0, 0)),
                  pl.BlockSpec((LANE, RW_W), lambda b, t: (0, 0)),
                  pl.BlockSpec((LANE, RW_W), lambda b, t: (0, 0)),
                  pl.BlockSpec((2 * LANE, RW_W), lambda b, t: (0, 0)),
                  pl.BlockSpec((LANE, LANE), lambda b, t: (0, 0))],
        out_specs=pl.BlockSpec((None, tt, RW_W), lambda b, t: (b, t, 0)),
        out_shape=jax.ShapeDtypeStruct((bsz, seq, RW_W), BF16),
        scratch_shapes=scratch,
        compiler_params=_cparams(("parallel", "arbitrary")),
    )(p_rw, p_rw, mu, vecs, w2p, a2p, g2p, seg)


MLA_HD = LANE
MLA_ROPE_OFF = MLA_NOPE
ROLL_ROPE = LANE - MLA_ROPE


def _mla_proj_kernel(x_ref, pos_ref, freq_ref, sgn_ref, win_ref, qg_ref, wq_ref, kvg_ref, wk_ref,
                     wvt_ref, q_ref, k_ref, vt_ref):
    xb = x_ref[...].astype(BF16)
    p = jnp.dot(xb, win_ref[...], preferred_element_type=F32)
    cq = p[:, :MLA_Q_RANK]
    ckv = p[:, MLA_Q_RANK:MLA_Q_RANK + MLA_KV_RANK]
    kpe = p[:, MLA_Q_RANK + MLA_KV_RANK:]
    cqn = cq * lax.rsqrt(jnp.mean(cq * cq, axis=-1, keepdims=True) + RMS_EPS) * qg_ref[...]
    ckvn = ckv * lax.rsqrt(jnp.mean(ckv * ckv, axis=-1, keepdims=True) + RMS_EPS) * kvg_ref[...]
    cqb = cqn.astype(BF16)
    ckvb = ckvn.astype(BF16)

    ang = pos_ref[...].astype(F32) * freq_ref[...]
    lane = lax.broadcasted_iota(jnp.int32, (1, LANE), 1)
    is_rope = (lane >= MLA_ROPE_OFF) & (lane < MLA_ROPE_OFF + MLA_ROPE)
    cos_t = jnp.where(is_rope, jnp.cos(ang), 0.0)
    sin_t = jnp.sin(ang) * sgn_ref[...]
    scale = (MLA_NOPE + MLA_ROPE) ** -0.5 * math.log2(math.e)
    q_c = jnp.where(lane < MLA_NOPE, 1.0, cos_t) * scale
    q_s = sin_t * scale

    k_rot = kpe * cos_t + pltpu.roll(kpe, ROLL_ROPE, 1) * sin_t
    pair = lambda t: jnp.concatenate([t, t], axis=1)
    q_c2, q_s2, k_rot2 = pair(q_c), pair(q_s), pair(k_rot)
    for hp in range(MLA_HEADS // 2):
        cols = slice(2 * hp * MLA_HD, 2 * (hp + 1) * MLA_HD)
        qh = jnp.dot(cqb, wq_ref[:, cols], preferred_element_type=F32)
        q_ref[:, cols] = (qh * q_c2 + pltpu.roll(qh, 2 * MLA_HD - MLA_ROPE, 1) * q_s2).astype(q_ref.dtype)
        kh = jnp.dot(ckvb, wk_ref[:, cols], preferred_element_type=F32)
        k_ref[:, cols] = (kh + k_rot2).astype(k_ref.dtype)
    vt = lax.dot_general(wvt_ref[...], ckvb, (((1,), (1,)), ((), ())),
                         preferred_element_type=F32)
    vrow = lax.broadcasted_iota(jnp.int32, (vt.shape[0], 1), 0)
    vt_ref[...] = (vt + ((vrow % ATT_VROWS) >= MLA_V).astype(F32)).astype(vt_ref.dtype)


def _mla_proj(x2d, pos2d, freq, sgn, w_in, q_g, w_q, kv_g, w_k, w_vt, bsz, seq, tm):
    m, d = x2d.shape
    tm = min(tm, seq)
    nq = MLA_HEADS * MLA_HD
    nv = MLA_HEADS * ATT_VROWS
    spb = seq // tm
    const = lambda shape: pl.BlockSpec(shape, lambda i: (0,) * len(shape))
    return pl.pallas_call(
        _mla_proj_kernel,
        grid=(m // tm,),
        in_specs=[pl.BlockSpec((tm, d), lambda i: (i, 0)),
                  pl.BlockSpec((tm, 1), lambda i: (i, 0)),
                  const((1, LANE)), const((1, LANE)),
                  const(w_in.shape), const((1, MLA_Q_RANK)), const(w_q.shape),
                  const((1, MLA_KV_RANK)), const(w_k.shape), const(w_vt.shape)],
        out_specs=[pl.BlockSpec((tm, nq), lambda i: (i, 0)),
                   pl.BlockSpec((tm, nq), lambda i: (i, 0)),
                   pl.BlockSpec((None, nv, tm), lambda i: (i // spb, 0, i % spb))],
        out_shape=[jax.ShapeDtypeStruct((m, nq), BF16),
                   jax.ShapeDtypeStruct((m, nq), BF16),
                   jax.ShapeDtypeStruct((bsz, nv, seq), BF16)],
        name="mla_proj",
        compiler_params=_cparams(("parallel",)),
    )(x2d, pos2d, freq, sgn, w_in, q_g.reshape(1, -1), w_q, kv_g.reshape(1, -1), w_k, w_vt)


def _attn_kernel(q_ref, k_ref, vt_ref, o_ref, *, tq, n_heads):
    i = pl.program_id(2)
    q0 = i * tq
    nsub = tq // ATT_SUB
    qhs = [q_ref[:, h * MLA_HD:(h + 1) * MLA_HD] for h in range(n_heads)]
    kr = lax.broadcasted_iota(jnp.int32, (ATT_SUB, ATT_SUB), 0) // CHUNK
    qc = lax.broadcasted_iota(jnp.int32, (ATT_SUB, ATT_SUB), 1) // CHUNK
    diag_ok = kr <= qc

    def scores(k0, h, q_rows):
        return lax.dot_general(k_ref[pl.ds(k0, ATT_SUB), h * MLA_HD:(h + 1) * MLA_HD], q_rows,
                               (((1,), (1,)), ((), ())), preferred_element_type=F32)

    def update(h, k0, m_run, acc, s):
        m_new = jnp.maximum(m_run, jnp.max(s, axis=0, keepdims=True))
        alpha = jnp.exp2(m_run - m_new)
        pr = jnp.exp2(s - m_new).astype(BF16)
        vt = vt_ref[h * ATT_VROWS:(h + 1) * ATT_VROWS, pl.ds(k0, ATT_SUB)]
        return m_new, acc * alpha + jnp.dot(vt, pr, preferred_element_type=F32)

    def full_tile(j, carry):
        k0 = pl.multiple_of(j * tq, tq)
        ss = [lax.dot_general(k_ref[pl.ds(k0, tq), h * MLA_HD:(h + 1) * MLA_HD], qhs[h],
                              (((1,), (1,)), ((), ())), preferred_element_type=F32)
              for h in range(n_heads)]
        cur = list(carry)
        for u in range(nsub):
            for h in range(n_heads):
                cur[h] = update(h, k0 + u * ATT_SUB, *cur[h], ss[h][u * ATT_SUB:(u + 1) * ATT_SUB, :])
        return tuple(cur)

    init = tuple((jnp.full((1, tq), -jnp.inf, F32), jnp.zeros((ATT_VROWS, tq), F32))
                 for _ in range(n_heads))
    cur = list(lax.fori_loop(0, i, full_tile, init))

    ds = [[scores(pl.multiple_of(q0 + u * ATT_SUB, ATT_SUB), h, qhs[h][u * ATT_SUB:, :])
           for h in range(n_heads)] for u in range(nsub)]
    for u in range(nsub):
        c0 = u * ATT_SUB
        k0 = pl.multiple_of(q0 + c0, ATT_SUB)
        for h in range(n_heads):
            m_run, acc = cur[h]
            s = ds[u][h]
            s_diag = jnp.where(diag_ok, s[:, :ATT_SUB], -jnp.inf)
            s = s_diag if u == nsub - 1 else jnp.concatenate([s_diag, s[:, ATT_SUB:]], axis=1)
            m_part, acc_part = update(h, k0, m_run[:, c0:], acc[:, c0:], s)
            if u > 0:
                m_part = jnp.concatenate([m_run[:, :c0], m_part], axis=1)
                acc_part = jnp.concatenate([acc[:, :c0], acc_part], axis=1)
            cur[h] = (m_part, acc_part)

    outs = [acc[:MLA_V, :] / acc[MLA_V:MLA_V + 1, :] for _, acc in cur]
    o_ref[...] = jnp.concatenate(outs, axis=0).T.astype(o_ref.dtype)


def _attention(q, k, vt, tq, n_heads):
    bsz, seq, nq = q.shape
    tq = min(tq, seq)
    width = n_heads * MLA_HD
    return pl.pallas_call(
        functools.partial(_attn_kernel, tq=tq, n_heads=n_heads),
        grid=(bsz, MLA_HEADS // n_heads, seq // tq),
        in_specs=[pl.BlockSpec((None, tq, width), lambda b, h, i: (b, i, h)),
                  pl.BlockSpec((None, seq, width), lambda b, h, i: (b, 0, h)),
                  pl.BlockSpec((None, n_heads * ATT_VROWS, seq), lambda b, h, i: (b, h, 0))],
        out_specs=pl.BlockSpec((None, tq, n_heads * MLA_V), lambda b, h, i: (b, i, h)),
        out_shape=jax.ShapeDtypeStruct((bsz, seq, MLA_HEADS * MLA_V), BF16),
        name="mla_attention",
        compiler_params=_cparams(("parallel", "parallel", "arbitrary")),
    )(q, k, vt)


def _pad_cols(w, width):
    return jnp.pad(w, ((0, 0), (0, width - w.shape[1])))


def _pad_rows(w, height):
    return jnp.pad(w, ((0, height - w.shape[0]), (0, 0)))


def _even_weights(w_in, mu):
    gla_in = 2 * GLA_QK + 2 * GLA_V + GLA_GATE_RANK
    w_gla, w_rw = w_in[:, :gla_in], w_in[:, gla_in:]
    o = 3 * RW_W
    rw_parts = [w_rw[:, :o],
                _pad_cols(w_rw[:, o:o + RW_DECAY_RANK], LANE),
                _pad_cols(w_rw[:, o + RW_DECAY_RANK:o + RW_DECAY_RANK + RW_A_RANK], LANE),
                _pad_cols(w_rw[:, o + RW_DECAY_RANK + RW_A_RANK:], 2 * LANE)]
    mu_parts = [mu[:o],
                jnp.pad(mu[o:o + RW_DECAY_RANK], (0, LANE - RW_DECAY_RANK)),
                jnp.pad(mu[o + RW_DECAY_RANK:o + RW_DECAY_RANK + RW_A_RANK], (0, LANE - RW_A_RANK)),
                jnp.pad(mu[o + RW_DECAY_RANK + RW_A_RANK:], (0, 2 * LANE - RW_GATE_RANK))]
    w_all = jnp.concatenate(rw_parts + [_pad_cols(w_gla, GLA_COLS)], axis=1).astype(BF16)
    return w_all, jnp.concatenate(mu_parts).reshape(1, RW_COLS)


def _mla_weights(w_in, w_q_b, w_kv_b):
    swap = np.arange(MLA_ROPE) ^ 1
    d = w_in.shape[0]
    kpe = w_in[:, MLA_Q_RANK + MLA_KV_RANK:]
    w_in_p = jnp.concatenate([w_in[:, :MLA_Q_RANK + MLA_KV_RANK],
                              jnp.zeros((d, MLA_NOPE), F32), kpe, kpe[:, swap]], axis=1).astype(BF16)
    wq = w_q_b.reshape(MLA_Q_RANK, MLA_HEADS, MLA_NOPE + MLA_ROPE)
    wq = jnp.concatenate([wq, wq[:, :, MLA_NOPE:][:, :, swap]], axis=-1)
    wq = wq.reshape(MLA_Q_RANK, MLA_HEADS * MLA_HD).astype(BF16)
    wkv = w_kv_b.reshape(MLA_KV_RANK, MLA_HEADS, MLA_NOPE + MLA_V)
    wk = jnp.pad(wkv[:, :, :MLA_NOPE], ((0, 0), (0, 0), (0, MLA_HD - MLA_NOPE)))
    wk = wk.reshape(MLA_KV_RANK, MLA_HEADS * MLA_HD).astype(BF16)
    wv = jnp.pad(wkv[:, :, MLA_NOPE:], ((0, 0), (0, 0), (0, ATT_VROWS - MLA_V)))
    wvt = wv.reshape(MLA_KV_RANK, MLA_HEADS * ATT_VROWS).T.astype(BF16)
    return w_in_p, wq, wk, wvt


def _rope_tables():
    inv_freq = ROPE_THETA ** (-jnp.arange(0, MLA_ROPE, 2, dtype=jnp.float32) / MLA_ROPE)
    d = np.arange(MLA_ROPE)
    freq = jnp.zeros((LANE,), F32).at[MLA_ROPE_OFF:MLA_ROPE_OFF + MLA_ROPE].set(inv_freq[d // 2])
    sgn = np.zeros((LANE,), np.float32)
    sgn[MLA_ROPE_OFF:MLA_ROPE_OFF + MLA_ROPE] = np.where(d % 2 == 0, -1.0, 1.0)
    return freq.reshape(1, LANE), jnp.asarray(sgn).reshape(1, LANE)


def kernel(x, positions, even_w_in, gla_gate_w2, gla_gate_b, gla_norm_g, rwkv_mu, rwkv_w0, rwkv_w2,
           rwkv_a0, rwkv_a2, rwkv_g2, rwkv_k_k, rwkv_k_a, rwkv_r_k, rwkv_ln_g, rwkv_ln_b, even_w_out,
           mla_w_in, mla_q_norm_g, mla_w_q_b, mla_kv_norm_g, mla_w_kv_b, mla_w_out,
           ffn_w_gate_up, ffn_w_down, ln_g, ln_b):
    bsz, seq, d = x.shape
    m = bsz * seq
    x2d = x.reshape(m, d)
    wgu = ffn_w_gate_up.astype(BF16)
    wdn = ffn_w_down.astype(BF16)

    w_all, mu_p = _even_weights(even_w_in[0], rwkv_mu[0])
    p_rw, p_gla = _inproj(x2d, w_all, (RW_COLS, GLA_COLS), F32, tm=256)
    o_a = _gla(p_gla.reshape(bsz, seq, GLA_COLS),
               _pad_rows(gla_gate_w2[0], LANE).astype(BF16), gla_gate_b[0], gla_norm_g[0], tt=GLA_TT)
    vecs = jnp.stack([rwkv_w0[0], rwkv_a0[0], rwkv_k_k[0], rwkv_k_a[0], rwkv_r_k[0].reshape(RW_W),
                      rwkv_ln_g[0], rwkv_ln_b[0], jnp.zeros((RW_W,), F32)])
    o_b = _rwkv(p_rw.reshape(bsz, seq, RW_COLS), mu_p, vecs,
                _pad_rows(rwkv_w2[0], LANE).astype(BF16), _pad_rows(rwkv_a2[0], LANE).astype(BF16),
                _pad_rows(rwkv_g2[0], 2 * LANE).astype(BF16), tt=RW_TT)
    w_out = even_w_out[0].astype(BF16)
    x2d = _mix_ffn([o_a.reshape(m, GLA_V), o_b.reshape(m, RW_W)], [w_out[:GLA_V], w_out[GLA_V:]],
                   x2d, ln_g[0, 0], ln_b[0, 0], wgu, wdn, 0, ln_g[0, 1], ln_b[0, 1],
                   tm=FFN_TM, th=FFN_TH)

    w_in_p, wq, wk, wvt = _mla_weights(mla_w_in[0], mla_w_q_b[0], mla_w_kv_b[0])
    freq, sgn = _rope_tables()
    q, k, vt = _mla_proj(x2d, positions.reshape(m, 1), freq, sgn, w_in_p, mla_q_norm_g[0], wq,
                         mla_kv_norm_g[0], wk, wvt, bsz, seq, tm=MLA_TM)
    nq = MLA_HEADS * MLA_HD
    o = _attention(q.reshape(bsz, seq, nq), k.reshape(bsz, seq, nq), vt, tq=ATT_TQ, n_heads=ATT_HEADS)
    x2d = _mix_ffn([o.reshape(m, MLA_HEADS * MLA_V)], [mla_w_out[0].astype(BF16)],
                   x2d, ln_g[1, 0], ln_b[1, 0], wgu, wdn, 1, ln_g[1, 1], ln_b[1, 1],
                   tm=FFN_TM, th=FFN_TH)
    return x2d.reshape(bsz, seq, d)
```

```python
import functools
import math

import jax
import jax.numpy as jnp
import numpy as np
from jax import lax
from jax.experimental import pallas as pl
from jax.experimental.pallas import tpu as pltpu

F32 = jnp.float32
BF16 = jnp.bfloat16

DEPTH = 2
CHUNK = 64
DN_ALPHA = (2.0 * DEPTH) ** 0.25
LN_EPS = 1e-5
RMS_EPS = 1e-6

GLA_HEADS = 4
GLA_DK = 64
GLA_DV = 128
GLA_GATE_RANK = 16
GLA_TAU = 16.0
GLA_QK = GLA_HEADS * GLA_DK
GLA_V = GLA_HEADS * GLA_DV
GLA_SUB = 16
GLA_EXP_CLAMP = 60.0 * math.log2(math.e)

RW_HEADS = 8
RW_N = 64
RW_W = RW_HEADS * RW_N
RW_DECAY_RANK = 64
RW_A_RANK = 64
RW_GATE_RANK = 160
RW_GN_EPS = 64e-5

MLA_HEADS = 16
MLA_NOPE = 64
MLA_ROPE = 32
MLA_V = 64
MLA_Q_RANK = 768
MLA_KV_RANK = 256
ROPE_THETA = 10000.0

LOG2E = math.log2(math.e)
LANE = 128
VMEM_LIMIT = 48 * 1024 * 1024
FFN_TM, FFN_TH = 1024, 256
FFN_SPLIT = 2
MLA_TM = 512
ATT_TQ, ATT_HEADS = 1024, 4
ATT_SUB = 256
ATT_VROWS = 80

RW_COLS = 3 * RW_W + LANE + LANE + 2 * LANE
GLA_COLS = 2 * GLA_QK + 2 * GLA_V + LANE


def _cparams(sem):
    return pltpu.CompilerParams(dimension_semantics=sem, vmem_limit_bytes=VMEM_LIMIT)


def _bdot(a, b):
    return jnp.dot(a.astype(BF16), b.astype(BF16), preferred_element_type=F32)


def _dot_nt(a, b):
    return lax.dot_general(a.astype(BF16), b.astype(BF16), (((1,), (1,)), ((), ())),
                           preferred_element_type=F32)


def _dot_tn(a, b):
    return lax.dot_general(a.astype(BF16), b.astype(BF16), (((0,), (0,)), ((), ())),
                           preferred_element_type=F32)


def _dot_split(m01, x):
    hi = x.astype(BF16)
    lo = (x - hi.astype(F32)).astype(BF16)
    m = m01.astype(BF16)
    return (jnp.dot(m, hi, preferred_element_type=F32)
            + jnp.dot(m, lo, preferred_element_type=F32))


def _softplus(x):
    return jnp.maximum(x, 0.0) + jnp.log(1.0 + jnp.exp(-jnp.abs(x)))


def _layer_norm(y, g, b):
    mu = jnp.mean(y, axis=-1, keepdims=True)
    d = y - mu
    var = jnp.mean(d * d, axis=-1, keepdims=True)
    return d * lax.rsqrt(var + LN_EPS) * g + b


def _inproj_kernel(x_ref, w_ref, *o_refs, widths, tn):
    xb = x_ref[...].astype(BF16)
    off = 0
    for o_ref, width in zip(o_refs, widths):
        for j in range(0, width, tn):
            w = min(tn, width - j)
            o_ref[:, j:j + w] = jnp.dot(xb, w_ref[:, off + j:off + j + w],
                                        preferred_element_type=F32).astype(o_ref.dtype)
        off += width


def _inproj(x2d, w_bf16, widths, out_dtype, tm):
    m, k = x2d.shape
    tm = min(tm, m)
    n = sum(widths)
    return pl.pallas_call(
        functools.partial(_inproj_kernel, widths=tuple(widths), tn=512),
        grid=(m // tm,),
        in_specs=[pl.BlockSpec((tm, k), lambda i: (i, 0)),
                  pl.BlockSpec((k, n), lambda i: (0, 0))],
        out_specs=[pl.BlockSpec((tm, wd), lambda i: (i, 0)) for wd in widths],
        out_shape=[jax.ShapeDtypeStruct((m, wd), out_dtype) for wd in widths],
        compiler_params=_cparams(("parallel",)),
    )(x2d, w_bf16)


def _mix_ffn_kernel(*refs, n_in):
    a_refs = refs[:n_in]
    wo_refs = refs[n_in:2 * n_in]
    (x_ref, g0_ref, b0_ref, wg_ref, wu_ref, wd_ref, g1_ref, b1_ref, o_ref,
     x1_ref, xb_ref, acc_ref) = refs[2 * n_in:]
    j = pl.program_id(1)
    last = pl.num_programs(1) - 1
    rows_per = x_ref.shape[0] // FFN_SPLIT
    groups = [slice(r * rows_per, (r + 1) * rows_per) for r in range(FFN_SPLIT)]

    def ffn(xb):
        gate = jnp.dot(xb, wg_ref[...], preferred_element_type=F32)
        up = jnp.dot(xb, wu_ref[...], preferred_element_type=F32)
        h = (gate * jax.nn.sigmoid(gate) * up).astype(BF16)
        return jnp.dot(h, wd_ref[...], preferred_element_type=F32)

    @pl.when(j == 0)
    def _():
        for rows in groups:
            y = DN_ALPHA * x_ref[rows, :]
            for a_ref, w_ref in zip(a_refs, wo_refs):
                y = y + jnp.dot(a_ref[rows, :], w_ref[...], preferred_element_type=F32)
            x1 = _layer_norm(y, g0_ref[...], b0_ref[...])
            xb = x1.astype(BF16)
            x1_ref[rows, :] = x1
            xb_ref[rows, :] = xb
            acc_ref[rows, :] = ffn(xb)

    @pl.when((j > 0) & (j < last))
    def _():
        acc_ref[...] += ffn(xb_ref[...])

    @pl.when(j == last)
    def _():
        for rows in groups:
            y = DN_ALPHA * x1_ref[rows, :] + acc_ref[rows, :] + ffn(xb_ref[rows, :])
            o_ref[rows, :] = _layer_norm(y, g1_ref[...], b1_ref[...])


def _mix_ffn(acts, w_outs, x2d, g0, b0, w_gate_up, w_down, layer, g1, b1, tm, th):
    m, d = x2d.shape
    hidden = w_down.shape[1]
    tm = min(tm, m)
    nh = hidden // th
    n_in = len(acts)
    vec = pl.BlockSpec((1, d), lambda i, j: (0, 0))
    in_specs = ([pl.BlockSpec((tm, a.shape[1]), lambda i, j: (i, 0)) for a in acts]
                + [pl.BlockSpec(w.shape, lambda i, j: (0, 0)) for w in w_outs]
                + [pl.BlockSpec((tm, d), lambda i, j: (i, 0)), vec, vec,
                   pl.BlockSpec((None, d, th), lambda i, j: (layer, 0, j)),
                   pl.BlockSpec((None, d, th), lambda i, j: (layer, 0, j + nh)),
                   pl.BlockSpec((None, th, d), lambda i, j: (layer, j, 0)), vec, vec])
    return pl.pallas_call(
        functools.partial(_mix_ffn_kernel, n_in=n_in),
        grid=(m // tm, nh),
        in_specs=in_specs,
        out_specs=pl.BlockSpec((tm, d), lambda i, j: (i, 0)),
        out_shape=jax.ShapeDtypeStruct((m, d), F32),
        scratch_shapes=[pltpu.VMEM((tm, d), F32), pltpu.VMEM((tm, d), BF16), pltpu.VMEM((tm, d), F32)],
        name="mix_ffn",
        compiler_params=_cparams(("parallel", "arbitrary")),
    )(*acts, *w_outs, x2d, g0.reshape(1, d), b0.reshape(1, d), w_gate_up, w_gate_up, w_down,
      g1.reshape(1, d), b1.reshape(1, d))


def _gla_kernel(p_ref, w2_ref, gb_ref, ng_ref, o_ref, st_ref, *, n_chunks):
    @pl.when(pl.program_id(1) == 0)
    def _():
        st_ref[...] = jnp.zeros_like(st_ref)

    c_row = lax.broadcasted_iota(jnp.int32, (CHUNK, CHUNK), 0)
    c_col = lax.broadcasted_iota(jnp.int32, (CHUNK, CHUNK), 1)
    tri = (c_col <= c_row)
    tri_loc = tri & (c_col >= (c_row // GLA_SUB) * GLA_SUB)
    cum_mat = jnp.concatenate([tri, tri_loc], axis=0).astype(BF16)
    lane = lax.broadcasted_iota(jnp.int32, (1, GLA_QK), 1)
    head_masks = [(lane // GLA_DK == h).astype(F32) for h in range(GLA_HEADS)]
    n_sub = CHUNK // GLA_SUB

    n_par = min(GLA_PAR_CHUNKS, n_chunks)
    o_v = 2 * GLA_QK
    o_g = o_v + GLA_V
    o_l = o_g + GLA_V

    def group(gi, carry):
        base = gi * (n_par * CHUNK)
        rows = [pl.ds(pl.multiple_of(base + cc * CHUNK, CHUNK), CHUNK) for cc in range(n_par)]
        q = [p_ref[r, 0:GLA_QK] * (GLA_DK ** -0.5) for r in rows]
        k = [p_ref[r, GLA_QK:o_v] for r in rows]
        vb = [p_ref[r, o_v:o_g].astype(BF16) for r in rows]
        gg = [p_ref[r, o_g:o_l] for r in rows]
        z = [_bdot(p_ref[r, o_l:GLA_COLS], w2_ref[...]) + gb_ref[...] for r in rows]
        log_a = [-_softplus(-x) * (LOG2E / GLA_TAU) for x in z]
        cums = [_dot_split(cum_mat, x) for x in log_a]
        b = [x[:CHUNK] for x in cums]
        b_last = [x[CHUNK - 1:CHUNK, :] for x in b]
        q_loc = [x * jnp.exp2(c[CHUNK:]) for x, c in zip(q, cums)]
        a_blocks = []
        for cc in range(n_par):
            blocks = []
            for i in range(n_sub):
                ref_i = (jnp.zeros((1, GLA_QK), F32) if i == 0
                         else b[cc][i * GLA_SUB - 1:i * GLA_SUB, :])
                k_i = k[cc] * jnp.exp2(jnp.minimum(ref_i - b[cc], GLA_EXP_CLAMP))
                q_i = q_loc[cc][i * GLA_SUB:(i + 1) * GLA_SUB, :]
                lhs = jnp.concatenate([q_i * m for m in head_masks], axis=0)
                blocks.append(_dot_nt(lhs, k_i))
            a_blocks.append(blocks)
        o_intra = []
        for cc in range(n_par):
            per_head = []
            for h in range(GLA_HEADS):
                a_h = jnp.concatenate(
                    [blk[h * GLA_SUB:(h + 1) * GLA_SUB, :] for blk in a_blocks[cc]], axis=0)
                a_h = jnp.where(tri, a_h, 0.0)
                per_head.append(_bdot(a_h, vb[cc][:, h * GLA_DV:(h + 1) * GLA_DV]))
            o_intra.append(per_head)
        lhs_inter = [jnp.concatenate([x * jnp.exp2(bb) * m for m in head_masks], axis=0)
                     for x, bb in zip(q, b)]
        kv = [_dot_tn(v, x * jnp.exp2(bl - bb)) for v, x, bl, bb in zip(vb, k, b_last, b)]
        kv = [sum(x[h * GLA_DV:(h + 1) * GLA_DV, :] * head_masks[h] for h in range(GLA_HEADS)) for x in kv]
        decay = [jnp.exp2(x) for x in b_last]

        st = st_ref[...]
        outs = []
        for cc in range(n_par):
            o_inter = _dot_nt(lhs_inter[cc], st)
            st = st * decay[cc] + kv[cc]
            for h in range(GLA_HEADS):
                o_h = o_inter[h * CHUNK:(h + 1) * CHUNK, :] + o_intra[cc][h]
                ms = jnp.mean(o_h * o_h, axis=-1, keepdims=True)
                o_h = o_h * lax.rsqrt(ms + RMS_EPS) * ng_ref[...]
                g_h = gg[cc][:, h * GLA_DV:(h + 1) * GLA_DV]
                outs.append((o_h * (g_h * jax.nn.sigmoid(g_h))).astype(o_ref.dtype))
        for cc in range(n_par):
            for h in range(GLA_HEADS):
                o_ref[rows[cc], h * GLA_DV:(h + 1) * GLA_DV] = outs[cc * GLA_HEADS + h]
        st_ref[...] = st
        return carry

    lax.fori_loop(0, n_chunks // n_par, group, 0)


def _gla(p_gla, w2p, gate_b, norm_g, tt):
    bsz, seq, _ = p_gla.shape
    tt = min(tt, seq)
    return pl.pallas_call(
        functools.partial(_gla_kernel, n_chunks=tt // CHUNK),
        grid=(bsz, seq // tt),
        in_specs=[pl.BlockSpec((None, tt, GLA_COLS), lambda b, t: (b, t, 0)),
                  pl.BlockSpec((LANE, GLA_QK), lambda b, t: (0, 0)),
                  pl.BlockSpec((1, GLA_QK), lambda b, t: (0, 0)),
                  pl.BlockSpec((1, GLA_DV), lambda b, t: (0, 0))],
        out_specs=pl.BlockSpec((None, tt, GLA_V), lambda b, t: (b, t, 0)),
        out_shape=jax.ShapeDtypeStruct((bsz, seq, GLA_V), BF16),
        scratch_shapes=[pltpu.VMEM((GLA_DV, GLA_QK), F32)],
        compiler_params=_cparams(("parallel", "arbitrary")),
    )(p_gla, w2p, gate_b.reshape(1, GLA_QK), norm_g.reshape(1, GLA_DV))


RW_PAIR = 2 * RW_N
RW_NPAIR = RW_HEADS // 2
RW_PAR_CHUNKS = 4
RW_TT = 256
GLA_TT = 512
GLA_PAR_CHUNKS = 8


def _rwkv_kernel(p_ref, prev_ref, mu_ref, vec_ref, w2_ref, a2_ref, g2_ref, seg_ref, o_ref,
                 st_ref, r_s, k_s, v_s, lw_s, av_s, bv_s, y_s, bonus_s, g_s, *, n_chunks):
    t_idx = pl.program_id(1)

    @pl.when(t_idx == 0)
    def _():
        st_ref[...] = jnp.zeros_like(st_ref)

    tt = p_ref.shape[0]
    w0, a0, k_k, k_a, r_k, ln_g, ln_b = [vec_ref[i:i + 1, :] for i in range(7)]
    seg = seg_ref[...]

    def seg_sum(x):
        return jnp.concatenate(
            [_dot_split_rhs(x[:, i * LANE:(i + 1) * LANE], seg) for i in range(RW_W // LANE)], axis=1)

    p = p_ref[...]
    rows = lax.broadcasted_iota(jnp.int32, (tt, 1), 0)
    first = jnp.where(t_idx > 0, prev_ref[7:8, :], 0.0)
    shifted = jnp.where(rows == 0, first, pltpu.roll(p, 1, 0))
    p = p + mu_ref[...] * (shifted - p)
    r = p[:, 0:RW_W]
    k = p[:, RW_W:2 * RW_W]
    v = p[:, 2 * RW_W:3 * RW_W]
    wl = p[:, 3 * RW_W:3 * RW_W + LANE]
    al = p[:, 3 * RW_W + LANE:3 * RW_W + 2 * LANE]
    gl = p[:, 3 * RW_W + 2 * LANE:RW_COLS]
    w_raw = -_softplus(-(w0 + _bdot(jnp.tanh(wl), w2_ref[...]))) - 0.5
    lw = -jnp.exp(w_raw) * LOG2E
    a = jax.nn.sigmoid(a0 + _bdot(al, a2_ref[...]))
    g = _bdot(jax.nn.sigmoid(gl), g2_ref[...])
    kk = k * k_k
    kk = kk * lax.rsqrt(jnp.maximum(seg_sum(kk * kk), 1e-24))
    k2 = k * (1.0 + (a - 1.0) * k_a)
    bonus_s[...] = seg_sum(r * k2 * r_k) * v
    g_s[...] = g
    r_s[...] = r
    k_s[...] = k2
    v_s[...] = v
    lw_s[...] = lw
    av_s[...] = -kk
    bv_s[...] = kk * a

    c_row = lax.broadcasted_iota(jnp.int32, (CHUNK, CHUNK), 0)
    c_col = lax.broadcasted_iota(jnp.int32, (CHUNK, CHUNK), 1)
    tri_b = (c_col <= c_row).astype(BF16)
    lane = lax.broadcasted_iota(jnp.int32, (1, RW_PAIR), 1)
    hmask = [(lane < RW_N).astype(F32), (lane >= RW_N).astype(F32)]
    p_row = lax.broadcasted_iota(jnp.int32, (RW_PAIR, RW_PAIR), 0)
    p_col = lax.broadcasted_iota(jnp.int32, (RW_PAIR, RW_PAIR), 1)
    bd = ((p_row < RW_N) == (p_col < RW_N)).astype(F32)

    n_par = min(RW_PAR_CHUNKS, n_chunks)
    items = [(cc, pi) for cc in range(n_par) for pi in range(RW_NPAIR)]
    r4 = lax.broadcasted_iota(jnp.int32, (CHUNK, 4 * CHUNK), 0)
    c4 = lax.broadcasted_iota(jnp.int32, (CHUNK, 4 * CHUNK), 1) % CHUNK
    tri4, tri_strict4 = c4 <= r4, c4 < r4
    eye4 = (c4 == r4).astype(F32)
    hmask2 = [jnp.concatenate([mk, mk], axis=1) for mk in hmask]
    q_row = lax.broadcasted_iota(jnp.int32, (4 * CHUNK, 4 * CHUNK), 0) // CHUNK
    q_col = lax.broadcasted_iota(jnp.int32, (4 * CHUNK, 4 * CHUNK), 1) // CHUNK
    bd4 = q_row == q_col

    def group(gi, carry):
        base = gi * (n_par * CHUNK)

        def ld(ref):
            return [ref[pl.ds(pl.multiple_of(base + cc * CHUNK, CHUNK), CHUNK),
                        pi * RW_PAIR:(pi + 1) * RW_PAIR] for cc, pi in items]

        rr, kc, vc, lwc, avc, bvc = ld(r_s), ld(k_s), ld(v_s), ld(lw_s), ld(av_s), ld(bv_s)
        st0 = [st_ref[pi] for pi in range(RW_NPAIR)]
        gcum = [_dot_split(tri_b, x) for x in lwc]
        g_last = [g[CHUNK - 1:CHUNK, :] for g in gcum]
        e_neg = [jnp.exp2(-g) for g in gcum]
        e_end = [jnp.exp2(gl - g) for gl, g in zip(g_last, gcum)]
        r_t = [x * jnp.exp2(g) for x, g in zip(rr, gcum)]
        a_t = [x * jnp.exp2(g - lw) for x, g, lw in zip(avc, gcum, lwc)]
        b_t = [x * e for x, e in zip(bvc, e_neg)]
        k_t = [x * e for x, e in zip(kc, e_neg)]
        b_bar = [x * e for x, e in zip(bvc, e_end)]
        k_bar = [x * e for x, e in zip(kc, e_end)]

        ar = [jnp.concatenate([a, r], axis=0) for a, r in zip(a_t, r_t)]
        w_rows = [jnp.concatenate([b * hmask[0], b * hmask[1], k * hmask[0], k * hmask[1]], axis=0)
                  for b, k in zip(b_t, k_t)]
        g_all = [_dot_nt(x, w) for x, w in zip(ar, w_rows)]
        l_all = [jnp.where(tri_strict4, g[:CHUNK], 0.0) for g in g_all]
        m_all = [jnp.where(tri4, g[CHUNK:], 0.0) for g in g_all]

        def blockdiag(p4):
            p4 = p4.astype(BF16)
            return jnp.where(bd4, jnp.concatenate([p4, p4, p4, p4], axis=0), jnp.zeros((), BF16))
        pw = [jnp.concatenate([l_all[2 * qd][:, :RW_PAIR], l_all[2 * qd + 1][:, :RW_PAIR]], axis=1)
              for qd in range(len(items) // 2)]
        t_inv = [eye4 + x for x in pw]
        pw_bd = [blockdiag(x) for x in pw]
        n = 2
        while n < CHUNK:
            pw = [jnp.dot(x.astype(BF16), xb, preferred_element_type=F32) for x, xb in zip(pw, pw_bd)]
            pw_bd = [blockdiag(x) for x in pw]
            t_inv = [t + jnp.dot(t.astype(BF16), xb, preferred_element_type=F32)
                     for t, xb in zip(t_inv, pw_bd)]
            n *= 2
        t_pair = [t_inv[it // 2][:, (it % 2) * RW_PAIR:(it % 2 + 1) * RW_PAIR] for it in range(len(items))]

        v_rows = [jnp.concatenate([v * hmask[0], v * hmask[1]], axis=0) for v in vc]
        lm = [_bdot(jnp.concatenate([l[:, RW_PAIR:], mm[:, RW_PAIR:]], axis=0), v)
              for l, mm, v in zip(l_all, m_all, v_rows)]
        x_rows = [jnp.concatenate(
            [jnp.concatenate([a * hmask[h], x[:CHUNK] * hmask[h]], axis=1) for h in range(2)], axis=0)
            for a, x in zip(a_t, lm)]
        x_p = [_bdot(t, x) for t, x in zip(t_pair, x_rows)]
        ry = [_bdot(mm[:, :RW_PAIR], jnp.concatenate([x * hmask2[0], x * hmask2[1]], axis=0))
              for mm, x in zip(m_all, x_p)]
        a_hat = [x[:, :RW_PAIR] for x in x_p]
        u_hat = [x[:, RW_PAIR:] for x in x_p]
        r_eff = [r + y[:, :RW_PAIR] for r, y in zip(r_t, ry)]
        y0 = [y[:, RW_PAIR:] + x[CHUNK:] for y, x in zip(ry, lm)]
        p_t = [_dot_tn(a, b) * bd for a, b in zip(a_hat, b_bar)]
        q_t = [_dot_tn(jnp.concatenate([u, v], axis=0), jnp.concatenate([b, k], axis=0)) * bd
               for u, v, b, k in zip(u_hat, vc, b_bar, k_bar)]
        w_end = [jnp.exp2(g) for g in g_last]

        st = list(st0)
        ys = []
        for it, (cc, pi) in enumerate(items):
            stb = st[pi].astype(BF16)
            ys.append(_dot_nt(r_eff[it], stb) + y0[it])
            st[pi] = st[pi] * w_end[it] + _bdot(stb, p_t[it]) + q_t[it]
        for it, (cc, pi) in enumerate(items):
            y_s[pl.ds(pl.multiple_of(base + cc * CHUNK, CHUNK), CHUNK),
                pi * RW_PAIR:(pi + 1) * RW_PAIR] = ys[it]
        for pi in range(RW_NPAIR):
            st_ref[pi] = st[pi]
        return carry

    lax.fori_loop(0, n_chunks // n_par, group, 0)

    y = y_s[...]
    inv_n = 1.0 / RW_N
    mu_y = seg_sum(y) * inv_n
    d = y - mu_y
    var = seg_sum(d * d) * inv_n
    yn = d * lax.rsqrt(var + RW_GN_EPS) * ln_g + ln_b
    o_ref[...] = ((yn + bonus_s[...]) * g_s[...]).astype(o_ref.dtype)


def _dot_split_rhs(x, m01):
    hi = x.astype(BF16)
    lo = (x - hi.astype(F32)).astype(BF16)
    return (jnp.dot(hi, m01, preferred_element_type=F32)
            + jnp.dot(lo, m01, preferred_element_type=F32))


def _rwkv(p_rw, mu, vecs, w2p, a2p, g2p, tt):
    bsz, seq, _ = p_rw.shape
    tt = min(tt, seq)
    seg = (np.arange(LANE)[:, None] // RW_N == np.arange(LANE)[None, :] // RW_N)
    seg = jnp.asarray(seg, BF16)
    row_blocks = tt // 8
    scratch = ([pltpu.VMEM((RW_NPAIR, RW_PAIR, RW_PAIR), F32)]
               + [pltpu.VMEM((tt, RW_W), F32) for _ in range(9)])
    return pl.pallas_call(
        functools.partial(_rwkv_kernel, n_chunks=tt // CHUNK),
        grid=(bsz, seq // tt),
        in_specs=[pl.BlockSpec((None, tt, RW_COLS), lambda b, t: (b, t, 0)),
                  pl.BlockSpec((None, 8, RW_COLS),
                               lambda b, t: (b, jnp.maximum(t * row_blocks - 1, 0), 0)),
                  pl.BlockSpec((1, RW_COLS), lambda b, t: (0, 0)),
                  pl.BlockSpec((8, RW_W), lambda b, t: (0, 0)),
                  pl.BlockSpec((LANE, RW_W), lambda b, t: (0, 0)),
                  pl.BlockSpec((LANE, RW_W), lambda b, t: (0, 0)),
                  pl.BlockSpec((2 * LANE, RW_W), lambda b, t: (0, 0)),
                  pl.BlockSpec((LANE, LANE), lambda b, t: (0, 0))],
        out_specs=pl.BlockSpec((None, tt, RW_W), lambda b, t: (b, t, 0)),
        out_shape=jax.ShapeDtypeStruct((bsz, seq, RW_W), BF16),
        scratch_shapes=scratch,
        compiler_params=_cparams(("parallel", "arbitrary")),
    )(p_rw, p_rw, mu, vecs, w2p, a2p, g2p, seg)


MLA_HD = LANE
MLA_ROPE_OFF = MLA_NOPE
ROLL_ROPE = LANE - MLA_ROPE


def _mla_proj_kernel(x_ref, pos_ref, freq_ref, sgn_ref, win_ref, qg_ref, wq_ref, kvg_ref, wk_ref,
                     wvt_ref, q_ref, k_ref, vt_ref):
    xb = x_ref[...].astype(BF16)
    p = jnp.dot(xb, win_ref[...], preferred_element_type=F32)
    cq = p[:, :MLA_Q_RANK]
    ckv = p[:, MLA_Q_RANK:MLA_Q_RANK + MLA_KV_RANK]
    kpe = p[:, MLA_Q_RANK + MLA_KV_RANK:]
    cqn = cq * lax.rsqrt(jnp.mean(cq * cq, axis=-1, keepdims=True) + RMS_EPS) * qg_ref[...]
    ckvn = ckv * lax.rsqrt(jnp.mean(ckv * ckv, axis=-1, keepdims=True) + RMS_EPS) * kvg_ref[...]
    cqb = cqn.astype(BF16)
    ckvb = ckvn.astype(BF16)

    ang = pos_ref[...].astype(F32) * freq_ref[...]
    lane = lax.broadcasted_iota(jnp.int32, (1, LANE), 1)
    is_rope = (lane >= MLA_ROPE_OFF) & (lane < MLA_ROPE_OFF + MLA_ROPE)
    cos_t = jnp.where(is_rope, jnp.cos(ang), 0.0)
    sin_t = jnp.sin(ang) * sgn_ref[...]
    scale = (MLA_NOPE + MLA_ROPE) ** -0.5 * math.log2(math.e)
    q_c = jnp.where(lane < MLA_NOPE, 1.0, cos_t) * scale
    q_s = sin_t * scale

    k_rot = kpe * cos_t + pltpu.roll(kpe, ROLL_ROPE, 1) * sin_t
    pair = lambda t: jnp.concatenate([t, t], axis=1)
    q_c2, q_s2, k_rot2 = pair(q_c), pair(q_s), pair(k_rot)
    for hp in range(MLA_HEADS // 2):
        cols = slice(2 * hp * MLA_HD, 2 * (hp + 1) * MLA_HD)
        qh = jnp.dot(cqb, wq_ref[:, cols], preferred_element_type=F32)
        q_ref[:, cols] = (qh * q_c2 + pltpu.roll(qh, 2 * MLA_HD - MLA_ROPE, 1) * q_s2).astype(q_ref.dtype)
        kh = jnp.dot(ckvb, wk_ref[:, cols], preferred_element_type=F32)
        k_ref[:, cols] = (kh + k_rot2).astype(k_ref.dtype)
    vt = lax.dot_general(wvt_ref[...], ckvb, (((1,), (1,)), ((), ())),
                         preferred_element_type=F32)
    vrow = lax.broadcasted_iota(jnp.int32, (vt.shape[0], 1), 0)
    vt_ref[...] = (vt + ((vrow % ATT_VROWS) >= MLA_V).astype(F32)).astype(vt_ref.dtype)


def _mla_proj(x2d, pos2d, freq, sgn, w_in, q_g, w_q, kv_g, w_k, w_vt, bsz, seq, tm):
    m, d = x2d.shape
    tm = min(tm, seq)
    nq = MLA_HEADS * MLA_HD
    nv = MLA_HEADS * ATT_VROWS
    spb = seq // tm
    const = lambda shape: pl.BlockSpec(shape, lambda i: (0,) * len(shape))
    return pl.pallas_call(
        _mla_proj_kernel,
        grid=(m // tm,),
        in_specs=[pl.BlockSpec((tm, d), lambda i: (i, 0)),
                  pl.BlockSpec((tm, 1), lambda i: (i, 0)),
                  const((1, LANE)), const((1, LANE)),
                  const(w_in.shape), const((1, MLA_Q_RANK)), const(w_q.shape),
                  const((1, MLA_KV_RANK)), const(w_k.shape), const(w_vt.shape)],
        out_specs=[pl.BlockSpec((tm, nq), lambda i: (i, 0)),
                   pl.BlockSpec((tm, nq), lambda i: (i, 0)),
                   pl.BlockSpec((None, nv, tm), lambda i: (i // spb, 0, i % spb))],
        out_shape=[jax.ShapeDtypeStruct((m, nq), BF16),
                   jax.ShapeDtypeStruct((m, nq), BF16),
                   jax.ShapeDtypeStruct((bsz, nv, seq), BF16)],
        name="mla_proj",
        compiler_params=_cparams(("parallel",)),
    )(x2d, pos2d, freq, sgn, w_in, q_g.reshape(1, -1), w_q, kv_g.reshape(1, -1), w_k, w_vt)


def _attn_kernel(q_ref, k_ref, vt_ref, o_ref, *, tq, n_heads):
    i = pl.program_id(2)
    q0 = i * tq
    nsub = tq // ATT_SUB
    qhs = [q_ref[:, h * MLA_HD:(h + 1) * MLA_HD] for h in range(n_heads)]
    kr = lax.broadcasted_iota(jnp.int32, (ATT_SUB, ATT_SUB), 0) // CHUNK
    qc = lax.broadcasted_iota(jnp.int32, (ATT_SUB, ATT_SUB), 1) // CHUNK
    diag_ok = kr <= qc

    def scores(k0, h, q_rows):
        return lax.dot_general(k_ref[pl.ds(k0, ATT_SUB), h * MLA_HD:(h + 1) * MLA_HD], q_rows,
                               (((1,), (1,)), ((), ())), preferred_element_type=F32)

    def update(h, k0, m_run, acc, s):
        m_new = jnp.maximum(m_run, jnp.max(s, axis=0, keepdims=True))
        alpha = jnp.exp2(m_run - m_new)
        pr = jnp.exp2(s - m_new).astype(BF16)
        vt = vt_ref[h * ATT_VROWS:(h + 1) * ATT_VROWS, pl.ds(k0, ATT_SUB)]
        return m_new, acc * alpha + jnp.dot(vt, pr, preferred_element_type=F32)

    def full_tile(j, carry):
        k0 = pl.multiple_of(j * tq, tq)
        ss = [lax.dot_general(k_ref[pl.ds(k0, tq), h * MLA_HD:(h + 1) * MLA_HD], qhs[h],
                              (((1,), (1,)), ((), ())), preferred_element_type=F32)
              for h in range(n_heads)]
        cur = list(carry)
        for u in range(nsub):
            for h in range(n_heads):
                cur[h] = update(h, k0 + u * ATT_SUB, *cur[h], ss[h][u * ATT_SUB:(u + 1) * ATT_SUB, :])
        return tuple(cur)

    init = tuple((jnp.full((1, tq), -jnp.inf, F32), jnp.zeros((ATT_VROWS, tq), F32))
                 for _ in range(n_heads))
    cur = list(lax.fori_loop(0, i, full_tile, init))

    ds = [[scores(pl.multiple_of(q0 + u * ATT_SUB, ATT_SUB), h, qhs[h][u * ATT_SUB:, :])
           for h in range(n_heads)] for u in range(nsub)]
    for u in range(nsub):
        c0 = u * ATT_SUB
        k0 = pl.multiple_of(q0 + c0, ATT_SUB)
        for h in range(n_heads):
            m_run, acc = cur[h]
            s = ds[u][h]
            s_diag = jnp.where(diag_ok, s[:, :ATT_SUB], -jnp.inf)
            s = s_diag if u == nsub - 1 else jnp.concatenate([s_diag, s[:, ATT_SUB:]], axis=1)
            m_part, acc_part = update(h, k0, m_run[:, c0:], acc[:, c0:], s)
            if u > 0:
                m_part = jnp.concatenate([m_run[:, :c0], m_part], axis=1)
                acc_part = jnp.concatenate([acc[:, :c0], acc_part], axis=1)
            cur[h] = (m_part, acc_part)

    outs = [acc[:MLA_V, :] / acc[MLA_V:MLA_V + 1, :] for _, acc in cur]
    o_ref[...] = jnp.concatenate(outs, axis=0).T.astype(o_ref.dtype)


def _attention(q, k, vt, tq, n_heads):
    bsz, seq, _ = q.shape
    tq = min(tq, seq)
    width = n_heads * MLA_HD
    return pl.pallas_call(
        functools.partial(_attn_kernel, tq=tq, n_heads=n_heads),
        grid=(bsz, MLA_HEADS // n_heads, seq // tq),
        in_specs=[pl.BlockSpec((None, tq, width), lambda b, h, i: (b, i, h)),
                  pl.BlockSpec((None, seq, width), lambda b, h, i: (b, 0, h)),
                  pl.BlockSpec((None, n_heads * ATT_VROWS, seq), lambda b, h, i: (b, h, 0))],
        out_specs=pl.BlockSpec((None, tq, n_heads * MLA_V), lambda b, h, i: (b, i, h)),
        out_shape=jax.ShapeDtypeStruct((bsz, seq, MLA_HEADS * MLA_V), BF16),
        name="mla_attention",
        compiler_params=_cparams(("parallel", "parallel", "arbitrary")),
    )(q, k, vt)


def _pad_cols(w, width):
    return jnp.pad(w, ((0, 0), (0, width - w.shape[1])))


def _pad_rows(w, height):
    return jnp.pad(w, ((0, height - w.shape[0]), (0, 0)))


def _even_weights(w_in, mu):
    gla_in = 2 * GLA_QK + 2 * GLA_V + GLA_GATE_RANK
    w_gla, w_rw = w_in[:, :gla_in], w_in[:, gla_in:]
    o = 3 * RW_W
    rw_parts = [w_rw[:, :o],
                _pad_cols(w_rw[:, o:o + RW_DECAY_RANK], LANE),
                _pad_cols(w_rw[:, o + RW_DECAY_RANK:o + RW_DECAY_RANK + RW_A_RANK], LANE),
                _pad_cols(w_rw[:, o + RW_DECAY_RANK + RW_A_RANK:], 2 * LANE)]
    mu_parts = [mu[:o],
                jnp.pad(mu[o:o + RW_DECAY_RANK], (0, LANE - RW_DECAY_RANK)),
                jnp.pad(mu[o + RW_DECAY_RANK:o + RW_DECAY_RANK + RW_A_RANK], (0, LANE - RW_A_RANK)),
                jnp.pad(mu[o + RW_DECAY_RANK + RW_A_RANK:], (0, 2 * LANE - RW_GATE_RANK))]
    w_all = jnp.concatenate(rw_parts + [_pad_cols(w_gla, GLA_COLS)], axis=1).astype(BF16)
    return w_all, jnp.concatenate(mu_parts).reshape(1, RW_COLS)


def _mla_weights(w_in, w_q_b, w_kv_b):
    swap = np.arange(MLA_ROPE) ^ 1
    d = w_in.shape[0]
    kpe = w_in[:, MLA_Q_RANK + MLA_KV_RANK:]
    w_in_p = jnp.concatenate([w_in[:, :MLA_Q_RANK + MLA_KV_RANK],
                              jnp.zeros((d, MLA_NOPE), F32), kpe, kpe[:, swap]], axis=1).astype(BF16)
    wq = w_q_b.reshape(MLA_Q_RANK, MLA_HEADS, MLA_NOPE + MLA_ROPE)
    wq = jnp.concatenate([wq, wq[:, :, MLA_NOPE:][:, :, swap]], axis=-1)
    wq = wq.reshape(MLA_Q_RANK, MLA_HEADS * MLA_HD).astype(BF16)
    wkv = w_kv_b.reshape(MLA_KV_RANK, MLA_HEADS, MLA_NOPE + MLA_V)
    wk = jnp.pad(wkv[:, :, :MLA_NOPE], ((0, 0), (0, 0), (0, MLA_HD - MLA_NOPE)))
    wk = wk.reshape(MLA_KV_RANK, MLA_HEADS * MLA_HD).astype(BF16)
    wv = jnp.pad(wkv[:, :, MLA_NOPE:], ((0, 0), (0, 0), (0, ATT_VROWS - MLA_V)))
    wvt = wv.reshape(MLA_KV_RANK, MLA_HEADS * ATT_VROWS).T.astype(BF16)
    return w_in_p, wq, wk, wvt


def _rope_tables():
    inv_freq = ROPE_THETA ** (-jnp.arange(0, MLA_ROPE, 2, dtype=jnp.float32) / MLA_ROPE)
    d = np.arange(MLA_ROPE)
    freq = jnp.zeros((LANE,), F32).at[MLA_ROPE_OFF:MLA_ROPE_OFF + MLA_ROPE].set(inv_freq[d // 2])
    sgn = np.zeros((LANE,), np.float32)
    sgn[MLA_ROPE_OFF:MLA_ROPE_OFF + MLA_ROPE] = np.where(d % 2 == 0, -1.0, 1.0)
    return freq.reshape(1, LANE), jnp.asarray(sgn).reshape(1, LANE)


def kernel(x, positions, even_w_in, gla_gate_w2, gla_gate_b, gla_norm_g, rwkv_mu, rwkv_w0, rwkv_w2,
           rwkv_a0, rwkv_a2, rwkv_g2, rwkv_k_k, rwkv_k_a, rwkv_r_k, rwkv_ln_g, rwkv_ln_b, even_w_out,
           mla_w_in, mla_q_norm_g, mla_w_q_b, mla_kv_norm_g, mla_w_kv_b, mla_w_out,
           ffn_w_gate_up, ffn_w_down, ln_g, ln_b):
    bsz, seq, d = x.shape
    m = bsz * seq
    x2d = x.reshape(m, d)
    wgu = ffn_w_gate_up.astype(BF16)
    wdn = ffn_w_down.astype(BF16)

    w_all, mu_p = _even_weights(even_w_in[0], rwkv_mu[0])
    p_rw, p_gla = _inproj(x2d, w_all, (RW_COLS, GLA_COLS), F32, tm=256)
    o_a = _gla(p_gla.reshape(bsz, seq, GLA_COLS),
               _pad_rows(gla_gate_w2[0], LANE).astype(BF16), gla_gate_b[0], gla_norm_g[0], tt=GLA_TT)
    vecs = jnp.stack([rwkv_w0[0], rwkv_a0[0], rwkv_k_k[0], rwkv_k_a[0], rwkv_r_k[0].reshape(RW_W),
                      rwkv_ln_g[0], rwkv_ln_b[0], jnp.zeros((RW_W,), F32)])
    o_b = _rwkv(p_rw.reshape(bsz, seq, RW_COLS), mu_p, vecs,
                _pad_rows(rwkv_w2[0], LANE).astype(BF16), _pad_rows(rwkv_a2[0], LANE).astype(BF16),
                _pad_rows(rwkv_g2[0], 2 * LANE).astype(BF16), tt=RW_TT)
    w_out = even_w_out[0].astype(BF16)
    x2d = _mix_ffn([o_a.reshape(m, GLA_V), o_b.reshape(m, RW_W)], [w_out[:GLA_V], w_out[GLA_V:]],
                   x2d, ln_g[0, 0], ln_b[0, 0], wgu, wdn, 0, ln_g[0, 1], ln_b[0, 1],
                   tm=FFN_TM, th=FFN_TH)

    w_in_p, wq, wk, wvt = _mla_weights(mla_w_in[0], mla_w_q_b[0], mla_w_kv_b[0])
    freq, sgn = _rope_tables()
    q, k, vt = _mla_proj(x2d, positions.reshape(m, 1), freq, sgn, w_in_p, mla_q_norm_g[0], wq,
                         mla_kv_norm_g[0], wk, wvt, bsz, seq, tm=MLA_TM)
    nq = MLA_HEADS * MLA_HD
    o = _attention(q.reshape(bsz, seq, nq), k.reshape(bsz, seq, nq), vt, tq=ATT_TQ, n_heads=ATT_HEADS)
    x2d = _mix_ffn([o.reshape(m, MLA_HEADS * MLA_V)], [mla_w_out[0].astype(BF16)],
                   x2d, ln_g[1, 0], ln_b[1, 0], wgu, wdn, 1, ln_g[1, 1], ln_b[1, 1],
                   tm=FFN_TM, th=FFN_TH)
    return x2d.reshape(bsz, seq, d)
```

```python
import functools
import math

import jax
import jax.numpy as jnp
import numpy as np
from jax import lax
from jax.experimental import pallas as pl
from jax.experimental.pallas import tpu as pltpu

F32 = jnp.float32
BF16 = jnp.bfloat16

DEPTH = 2
CHUNK = 64
DN_ALPHA = (2.0 * DEPTH) ** 0.25
LN_EPS = 1e-5
RMS_EPS = 1e-6

GLA_HEADS = 4
GLA_DK = 64
GLA_DV = 128
GLA_GATE_RANK = 16
GLA_TAU = 16.0
GLA_QK = GLA_HEADS * GLA_DK
GLA_V = GLA_HEADS * GLA_DV
GLA_SUB = 16
GLA_EXP_CLAMP = 60.0 * math.log2(math.e)

RW_HEADS = 8
RW_N = 64
RW_W = RW_HEADS * RW_N
RW_DECAY_RANK = 64
RW_A_RANK = 64
RW_GATE_RANK = 160
RW_GN_EPS = 64e-5

MLA_HEADS = 16
MLA_NOPE = 64
MLA_ROPE = 32
MLA_V = 64
MLA_Q_RANK = 768
MLA_KV_RANK = 256
ROPE_THETA = 10000.0

LOG2E = math.log2(math.e)
LANE = 128
VMEM_LIMIT = 48 * 1024 * 1024
FFN_TM, FFN_TH = 1024, 256
FFN_SPLIT = 2
MLA_TM = 512
ATT_TQ, ATT_HEADS = 1024, 4
ATT_SUB = 256
ATT_VROWS = 80

RW_COLS = 3 * RW_W + LANE + LANE + 2 * LANE
GLA_COLS = 2 * GLA_QK + 2 * GLA_V + LANE


def _cparams(sem):
    return pltpu.CompilerParams(dimension_semantics=sem, vmem_limit_bytes=VMEM_LIMIT)


def _bdot(a, b):
    return jnp.dot(a.astype(BF16), b.astype(BF16), preferred_element_type=F32)


def _dot_nt(a, b):
    return lax.dot_general(a.astype(BF16), b.astype(BF16), (((1,), (1,)), ((), ())),
                           preferred_element_type=F32)


def _dot_tn(a, b):
    return lax.dot_general(a.astype(BF16), b.astype(BF16), (((0,), (0,)), ((), ())),
                           preferred_element_type=F32)


def _dot_split(m01, x):
    hi = x.astype(BF16)
    lo = (x - hi.astype(F32)).astype(BF16)
    m = m01.astype(BF16)
    return (jnp.dot(m, hi, preferred_element_type=F32)
            + jnp.dot(m, lo, preferred_element_type=F32))


def _softplus(x):
    return jnp.maximum(x, 0.0) + jnp.log(1.0 + jnp.exp(-jnp.abs(x)))


def _layer_norm(y, g, b):
    mu = jnp.mean(y, axis=-1, keepdims=True)
    d = y - mu
    var = jnp.mean(d * d, axis=-1, keepdims=True)
    return d * lax.rsqrt(var + LN_EPS) * g + b


def _inproj_kernel(x_ref, w_ref, *o_refs, widths, tn):
    xb = x_ref[...].astype(BF16)
    off = 0
    for o_ref, width in zip(o_refs, widths):
        for j in range(0, width, tn):
            w = min(tn, width - j)
            o_ref[:, j:j + w] = jnp.dot(xb, w_ref[:, off + j:off + j + w],
                                        preferred_element_type=F32).astype(o_ref.dtype)
        off += width


def _inproj(x2d, w_bf16, widths, out_dtype, tm):
    m, k = x2d.shape
    tm = min(tm, m)
    n = sum(widths)
    return pl.pallas_call(
        functools.partial(_inproj_kernel, widths=tuple(widths), tn=512),
        grid=(m // tm,),
        in_specs=[pl.BlockSpec((tm, k), lambda i: (i, 0)),
                  pl.BlockSpec((k, n), lambda i: (0, 0))],
        out_specs=[pl.BlockSpec((tm, wd), lambda i: (i, 0)) for wd in widths],
        out_shape=[jax.ShapeDtypeStruct((m, wd), out_dtype) for wd in widths],
        compiler_params=_cparams(("parallel",)),
    )(x2d, w_bf16)


def _mix_ffn_kernel(*refs, n_in):
    a_refs = refs[:n_in]
    wo_refs = refs[n_in:2 * n_in]
    (x_ref, g0_ref, b0_ref, wg_ref, wu_ref, wd_ref, g1_ref, b1_ref, o_ref,
     x1_ref, xb_ref, acc_ref) = refs[2 * n_in:]
    j = pl.program_id(1)
    last = pl.num_programs(1) - 1
    rows_per = x_ref.shape[0] // FFN_SPLIT
    groups = [slice(r * rows_per, (r + 1) * rows_per) for r in range(FFN_SPLIT)]

    def ffn(xb):
        gate = jnp.dot(xb, wg_ref[...], preferred_element_type=F32)
        up = jnp.dot(xb, wu_ref[...], preferred_element_type=F32)
        h = (gate * jax.nn.sigmoid(gate) * up).astype(BF16)
        return jnp.dot(h, wd_ref[...], preferred_element_type=F32)

    @pl.when(j == 0)
    def _():
        for rows in groups:
            y = DN_ALPHA * x_ref[rows, :]
            for a_ref, w_ref in zip(a_refs, wo_refs):
                y = y + jnp.dot(a_ref[rows, :], w_ref[...], preferred_element_type=F32)
            x1 = _layer_norm(y, g0_ref[...], b0_ref[...])
            xb = x1.astype(BF16)
            x1_ref[rows, :] = x1
            xb_ref[rows, :] = xb
            acc_ref[rows, :] = ffn(xb)

    @pl.when((j > 0) & (j < last))
    def _():
        acc_ref[...] += ffn(xb_ref[...])

    @pl.when(j == last)
    def _():
        for rows in groups:
            y = DN_ALPHA * x1_ref[rows, :] + acc_ref[rows, :] + ffn(xb_ref[rows, :])
            o_ref[rows, :] = _layer_norm(y, g1_ref[...], b1_ref[...])


def _mix_ffn(acts, w_outs, x2d, g0, b0, w_gate_up, w_down, layer, g1, b1, tm, th):
    m, d = x2d.shape
    hidden = w_down.shape[1]
    tm = min(tm, m)
    nh = hidden // th
    n_in = len(acts)
    vec = pl.BlockSpec((1, d), lambda i, j: (0, 0))
    in_specs = ([pl.BlockSpec((tm, a.shape[1]), lambda i, j: (i, 0)) for a in acts]
                + [pl.BlockSpec(w.shape, lambda i, j: (0, 0)) for w in w_outs]
                + [pl.BlockSpec((tm, d), lambda i, j: (i, 0)), vec, vec,
                   pl.BlockSpec((None, d, th), lambda i, j: (layer, 0, j)),
                   pl.BlockSpec((None, d, th), lambda i, j: (layer, 0, j + nh)),
                   pl.BlockSpec((None, th, d), lambda i, j: (layer, j, 0)), vec, vec])
    return pl.pallas_call(
        functools.partial(_mix_ffn_kernel, n_in=n_in),
        grid=(m // tm, nh),
        in_specs=in_specs,
        out_specs=pl.BlockSpec((tm, d), lambda i, j: (i, 0)),
        out_shape=jax.ShapeDtypeStruct((m, d), F32),
        scratch_shapes=[pltpu.VMEM((tm, d), F32), pltpu.VMEM((tm, d), BF16), pltpu.VMEM((tm, d), F32)],
        name="mix_ffn",
        compiler_params=_cparams(("parallel", "arbitrary")),
    )(*acts, *w_outs, x2d, g0.reshape(1, d), b0.reshape(1, d), w_gate_up, w_gate_up, w_down,
      g1.reshape(1, d), b1.reshape(1, d))


def _gla_kernel(p_ref, w2_ref, gb_ref, ng_ref, o_ref, st_ref, *, n_chunks):
    @pl.when(pl.program_id(1) == 0)
    def _():
        st_ref[...] = jnp.zeros_like(st_ref)

    c_row = lax.broadcasted_iota(jnp.int32, (CHUNK, CHUNK), 0)
    c_col = lax.broadcasted_iota(jnp.int32, (CHUNK, CHUNK), 1)
    tri = (c_col <= c_row)
    tri_loc = tri & (c_col >= (c_row // GLA_SUB) * GLA_SUB)
    cum_mat = jnp.concatenate([tri, tri_loc], axis=0).astype(BF16)
    lane = lax.broadcasted_iota(jnp.int32, (1, GLA_QK), 1)
    head_masks = [(lane // GLA_DK == h).astype(F32) for h in range(GLA_HEADS)]
    n_sub = CHUNK // GLA_SUB

    n_par = min(GLA_PAR_CHUNKS, n_chunks)
    o_v = 2 * GLA_QK
    o_g = o_v + GLA_V
    o_l = o_g + GLA_V

    def group(gi, carry):
        base = gi * (n_par * CHUNK)
        rows = [pl.ds(pl.multiple_of(base + cc * CHUNK, CHUNK), CHUNK) for cc in range(n_par)]
        q = [p_ref[r, 0:GLA_QK].astype(F32) * (GLA_DK ** -0.5) for r in rows]
        k = [p_ref[r, GLA_QK:o_v].astype(F32) for r in rows]
        vb = [p_ref[r, o_v:o_g].astype(BF16) for r in rows]
        gg = [p_ref[r, o_g:o_l].astype(F32) for r in rows]
        z = [_bdot(p_ref[r, o_l:GLA_COLS], w2_ref[...]) + gb_ref[...] for r in rows]
        log_a = [-_softplus(-x) * (LOG2E / GLA_TAU) for x in z]
        cums = [_dot_split(cum_mat, x) for x in log_a]
        b = [x[:CHUNK] for x in cums]
        b_last = [x[CHUNK - 1:CHUNK, :] for x in b]
        q_loc = [x * jnp.exp2(c[CHUNK:]) for x, c in zip(q, cums)]
        a_blocks = []
        for cc in range(n_par):
            blocks = []
            for i in range(n_sub):
                ref_i = (jnp.zeros((1, GLA_QK), F32) if i == 0
                         else b[cc][i * GLA_SUB - 1:i * GLA_SUB, :])
                k_i = k[cc] * jnp.exp2(jnp.minimum(ref_i - b[cc], GLA_EXP_CLAMP))
                q_i = q_loc[cc][i * GLA_SUB:(i + 1) * GLA_SUB, :]
                lhs = jnp.concatenate([q_i * m for m in head_masks], axis=0)
                blocks.append(_dot_nt(lhs, k_i))
            a_blocks.append(blocks)
        o_intra = []
        for cc in range(n_par):
            per_head = []
            for h in range(GLA_HEADS):
                a_h = jnp.concatenate(
                    [blk[h * GLA_SUB:(h + 1) * GLA_SUB, :] for blk in a_blocks[cc]], axis=0)
                a_h = jnp.where(tri, a_h, 0.0)
                per_head.append(_bdot(a_h, vb[cc][:, h * GLA_DV:(h + 1) * GLA_DV]))
            o_intra.append(per_head)
        lhs_inter = [jnp.concatenate([x * jnp.exp2(bb) * m for m in head_masks], axis=0)
                     for x, bb in zip(q, b)]
        kv = [_dot_tn(v, x * jnp.exp2(bl - bb)) for v, x, bl, bb in zip(vb, k, b_last, b)]
        kv = [sum(x[h * GLA_DV:(h + 1) * GLA_DV, :] * head_masks[h] for h in range(GLA_HEADS)) for x in kv]
        decay = [jnp.exp2(x) for x in b_last]

        st = st_ref[...]
        outs = []
        for cc in range(n_par):
            o_inter = _dot_nt(lhs_inter[cc], st)
            st = st * decay[cc] + kv[cc]
            for h in range(GLA_HEADS):
                o_h = o_inter[h * CHUNK:(h + 1) * CHUNK, :] + o_intra[cc][h]
                ms = jnp.mean(o_h * o_h, axis=-1, keepdims=True)
                o_h = o_h * lax.rsqrt(ms + RMS_EPS) * ng_ref[...]
                g_h = gg[cc][:, h * GLA_DV:(h + 1) * GLA_DV]
                outs.append((o_h * (g_h * jax.nn.sigmoid(g_h))).astype(o_ref.dtype))
        for cc in range(n_par):
            for h in range(GLA_HEADS):
                o_ref[rows[cc], h * GLA_DV:(h + 1) * GLA_DV] = outs[cc * GLA_HEADS + h]
        st_ref[...] = st
        return carry

    lax.fori_loop(0, n_chunks // n_par, group, 0)


def _gla(p_gla, w2p, gate_b, norm_g, tt):
    bsz, seq, _ = p_gla.shape
    tt = min(tt, seq)
    return pl.pallas_call(
        functools.partial(_gla_kernel, n_chunks=tt // CHUNK),
        grid=(bsz, seq // tt),
        in_specs=[pl.BlockSpec((None, tt, GLA_COLS), lambda b, t: (b, t, 0)),
                  pl.BlockSpec((LANE, GLA_QK), lambda b, t: (0, 0)),
                  pl.BlockSpec((1, GLA_QK), lambda b, t: (0, 0)),
                  pl.BlockSpec((1, GLA_DV), lambda b, t: (0, 0))],
        out_specs=pl.BlockSpec((None, tt, GLA_V), lambda b, t: (b, t, 0)),
        out_shape=jax.ShapeDtypeStruct((bsz, seq, GLA_V), BF16),
        scratch_shapes=[pltpu.VMEM((GLA_DV, GLA_QK), F32)],
        compiler_params=_cparams(("parallel", "arbitrary")),
    )(p_gla, w2p, gate_b.reshape(1, GLA_QK), norm_g.reshape(1, GLA_DV))


RW_PAIR = 2 * RW_N
RW_NPAIR = RW_HEADS // 2
RW_PAR_CHUNKS = 4
RW_TT = 256
PREV_ROWS = 16
GLA_TT = 512
GLA_PAR_CHUNKS = 8


def _rwkv_kernel(p_ref, prev_ref, mu_ref, vec_ref, w2_ref, a2_ref, g2_ref, seg_ref, o_ref,
                 st_ref, r_s, k_s, v_s, lw_s, av_s, bv_s, y_s, bonus_s, g_s, *, n_chunks):
    t_idx = pl.program_id(1)

    @pl.when(t_idx == 0)
    def _():
        st_ref[...] = jnp.zeros_like(st_ref)

    tt = p_ref.shape[0]
    w0, a0, k_k, k_a, r_k, ln_g, ln_b = [vec_ref[i:i + 1, :] for i in range(7)]
    seg = seg_ref[...]

    def seg_sum(x):
        return jnp.concatenate(
            [_dot_split_rhs(x[:, i * LANE:(i + 1) * LANE], seg) for i in range(RW_W // LANE)], axis=1)

    p = p_ref[...].astype(F32)
    rows = lax.broadcasted_iota(jnp.int32, (tt, 1), 0)
    first = jnp.where(t_idx > 0, prev_ref[...].astype(F32)[PREV_ROWS - 1:PREV_ROWS, :], 0.0)
    shifted = jnp.where(rows == 0, first, pltpu.roll(p, 1, 0))
    p = p + mu_ref[...] * (shifted - p)
    r = p[:, 0:RW_W]
    k = p[:, RW_W:2 * RW_W]
    v = p[:, 2 * RW_W:3 * RW_W]
    wl = p[:, 3 * RW_W:3 * RW_W + LANE]
    al = p[:, 3 * RW_W + LANE:3 * RW_W + 2 * LANE]
    gl = p[:, 3 * RW_W + 2 * LANE:RW_COLS]
    w_raw = -_softplus(-(w0 + _bdot(jnp.tanh(wl), w2_ref[...]))) - 0.5
    lw = -jnp.exp(w_raw) * LOG2E
    a = jax.nn.sigmoid(a0 + _bdot(al, a2_ref[...]))
    g = _bdot(jax.nn.sigmoid(gl), g2_ref[...])
    kk = k * k_k
    kk = kk * lax.rsqrt(jnp.maximum(seg_sum(kk * kk), 1e-24))
    k2 = k * (1.0 + (a - 1.0) * k_a)
    bonus_s[...] = seg_sum(r * k2 * r_k) * v
    g_s[...] = g
    r_s[...] = r
    k_s[...] = k2
    v_s[...] = v
    lw_s[...] = lw
    av_s[...] = -kk
    bv_s[...] = kk * a

    c_row = lax.broadcasted_iota(jnp.int32, (CHUNK, CHUNK), 0)
    c_col = lax.broadcasted_iota(jnp.int32, (CHUNK, CHUNK), 1)
    tri_b = (c_col <= c_row).astype(BF16)
    lane = lax.broadcasted_iota(jnp.int32, (1, RW_PAIR), 1)
    hmask = [(lane < RW_N).astype(F32), (lane >= RW_N).astype(F32)]
    p_row = lax.broadcasted_iota(jnp.int32, (RW_PAIR, RW_PAIR), 0)
    p_col = lax.broadcasted_iota(jnp.int32, (RW_PAIR, RW_PAIR), 1)
    bd = ((p_row < RW_N) == (p_col < RW_N)).astype(F32)

    n_par = min(RW_PAR_CHUNKS, n_chunks)
    items = [(cc, pi) for cc in range(n_par) for pi in range(RW_NPAIR)]
    r4 = lax.broadcasted_iota(jnp.int32, (CHUNK, 4 * CHUNK), 0)
    c4 = lax.broadcasted_iota(jnp.int32, (CHUNK, 4 * CHUNK), 1) % CHUNK
    tri4, tri_strict4 = c4 <= r4, c4 < r4
    eye4 = (c4 == r4).astype(F32)
    hmask2 = [jnp.concatenate([mk, mk], axis=1) for mk in hmask]
    q_row = lax.broadcasted_iota(jnp.int32, (4 * CHUNK, 4 * CHUNK), 0) // CHUNK
    q_col = lax.broadcasted_iota(jnp.int32, (4 * CHUNK, 4 * CHUNK), 1) // CHUNK
    bd4 = q_row == q_col

    def group(gi, carry):
        base = gi * (n_par * CHUNK)

        def ld(ref):
            return [ref[pl.ds(pl.multiple_of(base + cc * CHUNK, CHUNK), CHUNK),
                        pi * RW_PAIR:(pi + 1) * RW_PAIR] for cc, pi in items]

        rr, kc, vc, lwc, avc, bvc = ld(r_s), ld(k_s), ld(v_s), ld(lw_s), ld(av_s), ld(bv_s)
        st0 = [st_ref[pi] for pi in range(RW_NPAIR)]
        gcum = [_dot_split(tri_b, x) for x in lwc]
        g_last = [g[CHUNK - 1:CHUNK, :] for g in gcum]
        e_neg = [jnp.exp2(-g) for g in gcum]
        e_end = [jnp.exp2(gl - g) for gl, g in zip(g_last, gcum)]
        r_t = [x * jnp.exp2(g) for x, g in zip(rr, gcum)]
        a_t = [x * jnp.exp2(g - lw) for x, g, lw in zip(avc, gcum, lwc)]
        b_t = [x * e for x, e in zip(bvc, e_neg)]
        k_t = [x * e for x, e in zip(kc, e_neg)]
        b_bar = [x * e for x, e in zip(bvc, e_end)]
        k_bar = [x * e for x, e in zip(kc, e_end)]

        ar = [jnp.concatenate([a, r], axis=0) for a, r in zip(a_t, r_t)]
        w_rows = [jnp.concatenate([b * hmask[0], b * hmask[1], k * hmask[0], k * hmask[1]], axis=0)
                  for b, k in zip(b_t, k_t)]
        g_all = [_dot_nt(x, w) for x, w in zip(ar, w_rows)]
        l_all = [jnp.where(tri_strict4, g[:CHUNK], 0.0) for g in g_all]
        m_all = [jnp.where(tri4, g[CHUNK:], 0.0) for g in g_all]

        def blockdiag(p4):
            p4 = p4.astype(BF16)
            return jnp.where(bd4, jnp.concatenate([p4, p4, p4, p4], axis=0), jnp.zeros((), BF16))
        pw = [jnp.concatenate([l_all[2 * qd][:, :RW_PAIR], l_all[2 * qd + 1][:, :RW_PAIR]], axis=1)
              for qd in range(len(items) // 2)]
        t_inv = [eye4 + x for x in pw]
        pw_bd = [blockdiag(x) for x in pw]
        n = 2
        while n < CHUNK:
            pw = [jnp.dot(x.astype(BF16), xb, preferred_element_type=F32) for x, xb in zip(pw, pw_bd)]
            pw_bd = [blockdiag(x) for x in pw]
            t_inv = [t + jnp.dot(t.astype(BF16), xb, preferred_element_type=F32)
                     for t, xb in zip(t_inv, pw_bd)]
            n *= 2
        t_pair = [t_inv[it // 2][:, (it % 2) * RW_PAIR:(it % 2 + 1) * RW_PAIR] for it in range(len(items))]

        v_rows = [jnp.concatenate([v * hmask[0], v * hmask[1]], axis=0) for v in vc]
        lm = [_bdot(jnp.concatenate([l[:, RW_PAIR:], mm[:, RW_PAIR:]], axis=0), v)
              for l, mm, v in zip(l_all, m_all, v_rows)]
        x_rows = [jnp.concatenate(
            [jnp.concatenate([a * hmask[h], x[:CHUNK] * hmask[h]], axis=1) for h in range(2)], axis=0)
            for a, x in zip(a_t, lm)]
        x_p = [_bdot(t, x) for t, x in zip(t_pair, x_rows)]
        ry = [_bdot(mm[:, :RW_PAIR], jnp.concatenate([x * hmask2[0], x * hmask2[1]], axis=0))
              for mm, x in zip(m_all, x_p)]
        a_hat = [x[:, :RW_PAIR] for x in x_p]
        u_hat = [x[:, RW_PAIR:] for x in x_p]
        r_eff = [r + y[:, :RW_PAIR] for r, y in zip(r_t, ry)]
        y0 = [y[:, RW_PAIR:] + x[CHUNK:] for y, x in zip(ry, lm)]
        p_t = [_dot_tn(a, b) * bd for a, b in zip(a_hat, b_bar)]
        q_t = [_dot_tn(jnp.concatenate([u, v], axis=0), jnp.concatenate([b, k], axis=0)) * bd
               for u, v, b, k in zip(u_hat, vc, b_bar, k_bar)]
        w_end = [jnp.exp2(g) for g in g_last]

        st = list(st0)
        ys = []
        for it, (cc, pi) in enumerate(items):
            stb = st[pi].astype(BF16)
            ys.append(_dot_nt(r_eff[it], stb) + y0[it])
            st[pi] = st[pi] * w_end[it] + _bdot(stb, p_t[it]) + q_t[it]
        for it, (cc, pi) in enumerate(items):
            y_s[pl.ds(pl.multiple_of(base + cc * CHUNK, CHUNK), CHUNK),
                pi * RW_PAIR:(pi + 1) * RW_PAIR] = ys[it]
        for pi in range(RW_NPAIR):
            st_ref[pi] = st[pi]
        return carry

    lax.fori_loop(0, n_chunks // n_par, group, 0)

    y = y_s[...]
    inv_n = 1.0 / RW_N
    mu_y = seg_sum(y) * inv_n
    d = y - mu_y
    var = seg_sum(d * d) * inv_n
    yn = d * lax.rsqrt(var + RW_GN_EPS) * ln_g + ln_b
    o_ref[...] = ((yn + bonus_s[...]) * g_s[...]).astype(o_ref.dtype)


def _dot_split_rhs(x, m01):
    hi = x.astype(BF16)
    lo = (x - hi.astype(F32)).astype(BF16)
    return (jnp.dot(hi, m01, preferred_element_type=F32)
            + jnp.dot(lo, m01, preferred_element_type=F32))


def _rwkv(p_rw, mu, vecs, w2p, a2p, g2p, tt):
    bsz, seq, _ = p_rw.shape
    tt = min(tt, seq)
    seg = (np.arange(LANE)[:, None] // RW_N == np.arange(LANE)[None, :] // RW_N)
    seg = jnp.asarray(seg, BF16)
    row_blocks = tt // PREV_ROWS
    scratch = ([pltpu.VMEM((RW_NPAIR, RW_PAIR, RW_PAIR), F32)]
               + [pltpu.VMEM((tt, RW_W), F32) for _ in range(9)])
    return pl.pallas_call(
        functools.partial(_rwkv_kernel, n_chunks=tt // CHUNK),
        grid=(bsz, seq // tt),
        in_specs=[pl.BlockSpec((None, tt, RW_COLS), lambda b, t: (b, t, 0)),
                  pl.BlockSpec((None, PREV_ROWS, RW_COLS),
                               lambda b, t: (b, jnp.maximum(t * row_blocks - 1, 0), 0)),
                  pl.BlockSpec((1, RW_COLS), lambda b, t: (0, 0)),
                  pl.BlockSpec((8, RW_W), lambda b, t: (0, 0)),
                  pl.BlockSpec((LANE, RW_W), lambda b, t: (0, 0)),
                  pl.BlockSpec((LANE, RW_W), lambda b, t: (0, 0)),
                  pl.BlockSpec((2 * LANE, RW_W), lambda b, t: (0, 0)),
                  pl.BlockSpec((LANE, LANE), lambda b, t: (0, 0))],
        out_specs=pl.BlockSpec((None, tt, RW_W), lambda b, t: (b, t, 0)),
        out_shape=jax.ShapeDtypeStruct((bsz, seq, RW_W), BF16),
        scratch_shapes=scratch,
        compiler_params=_cparams(("parallel", "arbitrary")),
    )(p_rw, p_rw, mu, vecs, w2p, a2p, g2p, seg)


MLA_HD = LANE
MLA_ROPE_OFF = MLA_NOPE
ROLL_ROPE = LANE - MLA_ROPE


def _mla_proj_kernel(x_ref, pos_ref, freq_ref, sgn_ref, win_ref, qg_ref, wq_ref, kvg_ref, wk_ref,
                     wvt_ref, q_ref, k_ref, vt_ref):
    xb = x_ref[...].astype(BF16)
    p = jnp.dot(xb, win_ref[...], preferred_element_type=F32)
    cq = p[:, :MLA_Q_RANK]
    ckv = p[:, MLA_Q_RANK:MLA_Q_RANK + MLA_KV_RANK]
    kpe = p[:, MLA_Q_RANK + MLA_KV_RANK:]
    cqn = cq * lax.rsqrt(jnp.mean(cq * cq, axis=-1, keepdims=True) + RMS_EPS) * qg_ref[...]
    ckvn = ckv * lax.rsqrt(jnp.mean(ckv * ckv, axis=-1, keepdims=True) + RMS_EPS) * kvg_ref[...]
    cqb = cqn.astype(BF16)
    ckvb = ckvn.astype(BF16)

    ang = pos_ref[...].astype(F32) * freq_ref[...]
    lane = lax.broadcasted_iota(jnp.int32, (1, LANE), 1)
    is_rope = (lane >= MLA_ROPE_OFF) & (lane < MLA_ROPE_OFF + MLA_ROPE)
    cos_t = jnp.where(is_rope, jnp.cos(ang), 0.0)
    sin_t = jnp.sin(ang) * sgn_ref[...]
    scale = (MLA_NOPE + MLA_ROPE) ** -0.5 * math.log2(math.e)
    q_c = jnp.where(lane < MLA_NOPE, 1.0, cos_t) * scale
    q_s = sin_t * scale

    k_rot = kpe * cos_t + pltpu.roll(kpe, ROLL_ROPE, 1) * sin_t
    pair = lambda t: jnp.concatenate([t, t], axis=1)
    q_c2, q_s2, k_rot2 = pair(q_c), pair(q_s), pair(k_rot)
    for hp in range(MLA_HEADS // 2):
        cols = slice(2 * hp * MLA_HD, 2 * (hp + 1) * MLA_HD)
        qh = jnp.dot(cqb, wq_ref[:, cols], preferred_element_type=F32)
        q_ref[:, cols] = (qh * q_c2 + pltpu.roll(qh, 2 * MLA_HD - MLA_ROPE, 1) * q_s2).astype(q_ref.dtype)
        kh = jnp.dot(ckvb, wk_ref[:, cols], preferred_element_type=F32)
        k_ref[:, cols] = (kh + k_rot2).astype(k_ref.dtype)
    vt = lax.dot_general(wvt_ref[...], ckvb, (((1,), (1,)), ((), ())),
                         preferred_element_type=F32)
    vrow = lax.broadcasted_iota(jnp.int32, (vt.shape[0], 1), 0)
    vt_ref[...] = (vt + ((vrow % ATT_VROWS) >= MLA_V).astype(F32)).astype(vt_ref.dtype)


def _mla_proj(x2d, pos2d, freq, sgn, w_in, q_g, w_q, kv_g, w_k, w_vt, bsz, seq, tm):
    m, d = x2d.shape
    tm = min(tm, seq)
    nq = MLA_HEADS * MLA_HD
    nv = MLA_HEADS * ATT_VROWS
    spb = seq // tm
    const = lambda shape: pl.BlockSpec(shape, lambda i: (0,) * len(shape))
    return pl.pallas_call(
        _mla_proj_kernel,
        grid=(m // tm,),
        in_specs=[pl.BlockSpec((tm, d), lambda i: (i, 0)),
                  pl.BlockSpec((tm, 1), lambda i: (i, 0)),
                  const((1, LANE)), const((1, LANE)),
                  const(w_in.shape), const((1, MLA_Q_RANK)), const(w_q.shape),
                  const((1, MLA_KV_RANK)), const(w_k.shape), const(w_vt.shape)],
        out_specs=[pl.BlockSpec((tm, nq), lambda i: (i, 0)),
                   pl.BlockSpec((tm, nq), lambda i: (i, 0)),
                   pl.BlockSpec((None, nv, tm), lambda i: (i // spb, 0, i % spb))],
        out_shape=[jax.ShapeDtypeStruct((m, nq), BF16),
                   jax.ShapeDtypeStruct((m, nq), BF16),
                   jax.ShapeDtypeStruct((bsz, nv, seq), BF16)],
        name="mla_proj",
        compiler_params=_cparams(("parallel",)),
    )(x2d, pos2d, freq, sgn, w_in, q_g.reshape(1, -1), w_q, kv_g.reshape(1, -1), w_k, w_vt)


def _attn_kernel(q_ref, k_ref, vt_ref, o_ref, *, tq, n_heads):
    i = pl.program_id(2)
    q0 = i * tq
    nsub = tq // ATT_SUB
    qhs = [q_ref[:, h * MLA_HD:(h + 1) * MLA_HD] for h in range(n_heads)]
    kr = lax.broadcasted_iota(jnp.int32, (ATT_SUB, ATT_SUB), 0) // CHUNK
    qc = lax.broadcasted_iota(jnp.int32, (ATT_SUB, ATT_SUB), 1) // CHUNK
    diag_ok = kr <= qc

    def scores(k0, h, q_rows):
        return lax.dot_general(k_ref[pl.ds(k0, ATT_SUB), h * MLA_HD:(h + 1) * MLA_HD], q_rows,
                               (((1,), (1,)), ((), ())), preferred_element_type=F32)

    def update(h, k0, m_run, acc, s):
        m_new = jnp.maximum(m_run, jnp.max(s, axis=0, keepdims=True))
        alpha = jnp.exp2(m_run - m_new)
        pr = jnp.exp2(s - m_new).astype(BF16)
        vt = vt_ref[h * ATT_VROWS:(h + 1) * ATT_VROWS, pl.ds(k0, ATT_SUB)]
        return m_new, acc * alpha + jnp.dot(vt, pr, preferred_element_type=F32)

    def full_tile(j, carry):
        k0 = pl.multiple_of(j * tq, tq)
        ss = [lax.dot_general(k_ref[pl.ds(k0, tq), h * MLA_HD:(h + 1) * MLA_HD], qhs[h],
                              (((1,), (1,)), ((), ())), preferred_element_type=F32)
              for h in range(n_heads)]
        cur = list(carry)
        for u in range(nsub):
            for h in range(n_heads):
                cur[h] = update(h, k0 + u * ATT_SUB, *cur[h], ss[h][u * ATT_SUB:(u + 1) * ATT_SUB, :])
        return tuple(cur)

    init = tuple((jnp.full((1, tq), -jnp.inf, F32), jnp.zeros((ATT_VROWS, tq), F32))
                 for _ in range(n_heads))
    cur = list(lax.fori_loop(0, i, full_tile, init))

    ds = [[scores(pl.multiple_of(q0 + u * ATT_SUB, ATT_SUB), h, qhs[h][u * ATT_SUB:, :])
           for h in range(n_heads)] for u in range(nsub)]
    for u in range(nsub):
        c0 = u * ATT_SUB
        k0 = pl.multiple_of(q0 + c0, ATT_SUB)
        for h in range(n_heads):
            m_run, acc = cur[h]
            s = ds[u][h]
            s_diag = jnp.where(diag_ok, s[:, :ATT_SUB], -jnp.inf)
            s = s_diag if u == nsub - 1 else jnp.concatenate([s_diag, s[:, ATT_SUB:]], axis=1)
            m_part, acc_part = update(h, k0, m_run[:, c0:], acc[:, c0:], s)
            if u > 0:
                m_part = jnp.concatenate([m_run[:, :c0], m_part], axis=1)
                acc_part = jnp.concatenate([acc[:, :c0], acc_part], axis=1)
            cur[h] = (m_part, acc_part)

    outs = [acc[:MLA_V, :] / acc[MLA_V:MLA_V + 1, :] for _, acc in cur]
    o_ref[...] = jnp.concatenate(outs, axis=0).T.astype(o_ref.dtype)


def _attention(q, k, vt, tq, n_heads):
    bsz, seq, _ = q.shape
    tq = min(tq, seq)
    width = n_heads * MLA_HD
    return pl.pallas_call(
        functools.partial(_attn_kernel, tq=tq, n_heads=n_heads),
        grid=(bsz, MLA_HEADS // n_heads, seq // tq),
        in_specs=[pl.BlockSpec((None, tq, width), lambda b, h, i: (b, i, h)),
                  pl.BlockSpec((None, seq, width), lambda b, h, i: (b, 0, h)),
                  pl.BlockSpec((None, n_heads * ATT_VROWS, seq), lambda b, h, i: (b, h, 0))],
        out_specs=pl.BlockSpec((None, tq, n_heads * MLA_V), lambda b, h, i: (b, i, h)),
        out_shape=jax.ShapeDtypeStruct((bsz, seq, MLA_HEADS * MLA_V), BF16),
        name="mla_attention",
        compiler_params=_cparams(("parallel", "parallel", "arbitrary")),
    )(q, k, vt)


def _pad_cols(w, width):
    return jnp.pad(w, ((0, 0), (0, width - w.shape[1])))


def _pad_rows(w, height):
    return jnp.pad(w, ((0, height - w.shape[0]), (0, 0)))


def _even_weights(w_in, mu):
    gla_in = 2 * GLA_QK + 2 * GLA_V + GLA_GATE_RANK
    w_gla, w_rw = w_in[:, :gla_in], w_in[:, gla_in:]
    o = 3 * RW_W
    rw_parts = [w_rw[:, :o],
                _pad_cols(w_rw[:, o:o + RW_DECAY_RANK], LANE),
                _pad_cols(w_rw[:, o + RW_DECAY_RANK:o + RW_DECAY_RANK + RW_A_RANK], LANE),
                _pad_cols(w_rw[:, o + RW_DECAY_RANK + RW_A_RANK:], 2 * LANE)]
    mu_parts = [mu[:o],
                jnp.pad(mu[o:o + RW_DECAY_RANK], (0, LANE - RW_DECAY_RANK)),
                jnp.pad(mu[o + RW_DECAY_RANK:o + RW_DECAY_RANK + RW_A_RANK], (0, LANE - RW_A_RANK)),
                jnp.pad(mu[o + RW_DECAY_RANK + RW_A_RANK:], (0, 2 * LANE - RW_GATE_RANK))]
    w_all = jnp.concatenate(rw_parts + [_pad_cols(w_gla, GLA_COLS)], axis=1).astype(BF16)
    return w_all, jnp.concatenate(mu_parts).reshape(1, RW_COLS)


def _mla_weights(w_in, w_q_b, w_kv_b):
    swap = np.arange(MLA_ROPE) ^ 1
    d = w_in.shape[0]
    kpe = w_in[:, MLA_Q_RANK + MLA_KV_RANK:]
    w_in_p = jnp.concatenate([w_in[:, :MLA_Q_RANK + MLA_KV_RANK],
                              jnp.zeros((d, MLA_NOPE), F32), kpe, kpe[:, swap]], axis=1).astype(BF16)
    wq = w_q_b.reshape(MLA_Q_RANK, MLA_HEADS, MLA_NOPE + MLA_ROPE)
    wq = jnp.concatenate([wq, wq[:, :, MLA_NOPE:][:, :, swap]], axis=-1)
    wq = wq.reshape(MLA_Q_RANK, MLA_HEADS * MLA_HD).astype(BF16)
    wkv = w_kv_b.reshape(MLA_KV_RANK, MLA_HEADS, MLA_NOPE + MLA_V)
    wk = jnp.pad(wkv[:, :, :MLA_NOPE], ((0, 0), (0, 0), (0, MLA_HD - MLA_NOPE)))
    wk = wk.reshape(MLA_KV_RANK, MLA_HEADS * MLA_HD).astype(BF16)
    wv = jnp.pad(wkv[:, :, MLA_NOPE:], ((0, 0), (0, 0), (0, ATT_VROWS - MLA_V)))
    wvt = wv.reshape(MLA_KV_RANK, MLA_HEADS * ATT_VROWS).T.astype(BF16)
    return w_in_p, wq, wk, wvt


def _rope_tables():
    inv_freq = ROPE_THETA ** (-jnp.arange(0, MLA_ROPE, 2, dtype=jnp.float32) / MLA_ROPE)
    d = np.arange(MLA_ROPE)
    freq = jnp.zeros((LANE,), F32).at[MLA_ROPE_OFF:MLA_ROPE_OFF + MLA_ROPE].set(inv_freq[d // 2])
    sgn = np.zeros((LANE,), np.float32)
    sgn[MLA_ROPE_OFF:MLA_ROPE_OFF + MLA_ROPE] = np.where(d % 2 == 0, -1.0, 1.0)
    return freq.reshape(1, LANE), jnp.asarray(sgn).reshape(1, LANE)


def kernel(x, positions, even_w_in, gla_gate_w2, gla_gate_b, gla_norm_g, rwkv_mu, rwkv_w0, rwkv_w2,
           rwkv_a0, rwkv_a2, rwkv_g2, rwkv_k_k, rwkv_k_a, rwkv_r_k, rwkv_ln_g, rwkv_ln_b, even_w_out,
           mla_w_in, mla_q_norm_g, mla_w_q_b, mla_kv_norm_g, mla_w_kv_b, mla_w_out,
           ffn_w_gate_up, ffn_w_down, ln_g, ln_b):
    bsz, seq, d = x.shape
    m = bsz * seq
    x2d = x.reshape(m, d)
    wgu = ffn_w_gate_up.astype(BF16)
    wdn = ffn_w_down.astype(BF16)

    w_all, mu_p = _even_weights(even_w_in[0], rwkv_mu[0])
    p_rw, p_gla = _inproj(x2d, w_all, (RW_COLS, GLA_COLS), BF16, tm=256)
    o_a = _gla(p_gla.reshape(bsz, seq, GLA_COLS),
               _pad_rows(gla_gate_w2[0], LANE).astype(BF16), gla_gate_b[0], gla_norm_g[0], tt=GLA_TT)
    vecs = jnp.stack([rwkv_w0[0], rwkv_a0[0], rwkv_k_k[0], rwkv_k_a[0], rwkv_r_k[0].reshape(RW_W),
                      rwkv_ln_g[0], rwkv_ln_b[0], jnp.zeros((RW_W,), F32)])
    o_b = _rwkv(p_rw.reshape(bsz, seq, RW_COLS), mu_p, vecs,
                _pad_rows(rwkv_w2[0], LANE).astype(BF16), _pad_rows(rwkv_a2[0], LANE).astype(BF16),
                _pad_rows(rwkv_g2[0], 2 * LANE).astype(BF16), tt=RW_TT)
    w_out = even_w_out[0].astype(BF16)
    x2d = _mix_ffn([o_a.reshape(m, GLA_V), o_b.reshape(m, RW_W)], [w_out[:GLA_V], w_out[GLA_V:]],
                   x2d, ln_g[0, 0], ln_b[0, 0], wgu, wdn, 0, ln_g[0, 1], ln_b[0, 1],
                   tm=FFN_TM, th=FFN_TH)

    w_in_p, wq, wk, wvt = _mla_weights(mla_w_in[0], mla_w_q_b[0], mla_w_kv_b[0])
    freq, sgn = _rope_tables()
    q, k, vt = _mla_proj(x2d, positions.reshape(m, 1), freq, sgn, w_in_p, mla_q_norm_g[0], wq,
                         mla_kv_norm_g[0], wk, wvt, bsz, seq, tm=MLA_TM)
    nq = MLA_HEADS * MLA_HD
    o = _attention(q.reshape(bsz, seq, nq), k.reshape(bsz, seq, nq), vt, tq=ATT_TQ, n_heads=ATT_HEADS)
    x2d = _mix_ffn([o.reshape(m, MLA_HEADS * MLA_V)], [mla_w_out[0].astype(BF16)],
                   x2d, ln_g[1, 0], ln_b[1, 0], wgu, wdn, 1, ln_g[1, 1], ln_b[1, 1],
                   tm=FFN_TM, th=FFN_TH)
    return x2d.reshape(bsz, seq, d)
```
